```python
import math
import jax, jax.numpy as jnp
from jax import lax
import numpy as np

D_MODEL = 1024
BATCH = 16
SEQ = 256
DEPTH = 2
DEC_BATCH = 4
DEC_SEQ = 1024
PAST_LEN = 256

GRID_W = 64
HY_W = D_MODEL // 2
HY_BANDS = 16
HY_FEAT = 1 + 2 * HY_BANDS
HY_HIDDEN = 64
SHORT_CONV = 3
SSM_W = D_MODEL // 2
SSM_HEAD_DIM = 64
SSM_HEADS = SSM_W // SSM_HEAD_DIM
SSM_GROUPS = 2
SSM_STATE = 64
SSM_CHUNK = 128
XBC_W = SSM_W + 2 * SSM_GROUPS * SSM_STATE
N_HEADS = 8
KV_HEADS = 2
HEAD_DIM = 64
Q_PER_KV = N_HEADS // KV_HEADS
ATT_W = N_HEADS * HEAD_DIM
KV_W = KV_HEADS * HEAD_DIM
WINDOW = 128
ATT_BLOCK = 128
BAND = ATT_BLOCK + 2 * WINDOW
ROPE_BASE = 10000.0
N_BRANCH = 3
FFN_HIDDEN = ((8 * D_MODEL + 3 * 256 - 1) // (3 * 256)) * 256
SPLIT_SIZES = (3 * HY_W, SSM_W, XBC_W, 2 * SSM_HEADS, ATT_W, KV_W, KV_W, N_BRANCH * D_MODEL)
SPLIT_IDX = tuple(int(s) for s in np.cumsum(SPLIT_SIZES)[:-1])
N_IN = sum(SPLIT_SIZES)
EPS = 1e-6

kernel_name = 'hybrid_dit_prefix_hyena_ssd_swa_step'

F32 = jnp.float32


def rmsnorm(x, g):
    xf = x.astype(F32)
    y = xf * lax.rsqrt(jnp.mean(xf * xf, axis=-1, keepdims=True) + EPS)
    return (y * g.astype(F32)).astype(x.dtype)


def dwconv(x, w, b):
    C = x.shape[-1]
    pad = SHORT_CONV // 2
    y = lax.conv_general_dilated(x, w[:, None, :].astype(x.dtype), window_strides=(1,),
                                 padding=((pad, pad),), dimension_numbers=('NWC', 'WIO', 'NWC'),
                                 feature_group_count=C)
    return y + b.astype(x.dtype)


def adaln(cond, w, b):
    mod = jax.nn.silu(cond) @ w + b
    return mod.reshape(cond.shape[0], 6, D_MODEL)


def hyena_filters(L, p):
    t = jnp.arange(L, dtype=F32) / L
    ang = 2.0 * jnp.pi * t[:, None] * jnp.arange(1, HY_BANDS + 1, dtype=F32)[None]
    feat = jnp.concatenate([t[:, None], jnp.cos(ang), jnp.sin(ang)], axis=-1)
    h = jnp.sin(p['hy_freq1'].astype(F32) * (feat @ p['hy_w1'].astype(F32) + p['hy_b1'].astype(F32)))
    h = jnp.sin(p['hy_freq2'].astype(F32) * (h @ p['hy_w2'].astype(F32) + p['hy_b2'].astype(F32)))
    h = (h @ p['hy_w3'].astype(F32)) * jnp.exp(-t[:, None] * p['hy_decay'].astype(F32)[None])
    return h.reshape(L, 2, 2, HY_W)


def long_conv(u, hf, hb, bias):
    L, C = hf.shape
    k = jnp.concatenate([hf, jnp.zeros((1, C), F32), hb[:0:-1]], axis=0)
    uf = u.astype(F32)
    spec = jnp.fft.rfft(uf, n=2 * L, axis=1) * jnp.fft.rfft(k, axis=0)[None]
    y = jnp.fft.irfft(spec, n=2 * L, axis=1)[:, :L]
    return (y + uf * bias.astype(F32)).astype(u.dtype)


def hyena_branch(hy_in, p):
    L = hy_in.shape[1]
    hc = dwconv(hy_in, p['hy_conv_w'], p['hy_conv_b'])
    v, x1, x2 = jnp.split(hc, 3, axis=-1)
    filt = hyena_filters(L, p)
    z = x1 * long_conv(v, filt[:, 0, 0], filt[:, 0, 1], p['hy_bias'][0])
    return x2 * long_conv(z, filt[:, 1, 0], filt[:, 1, 1], p['hy_bias'][1])


def ssd(x, dt, a, bm, cm, h0):
    b, L, H, P = x.shape
    nc = L // SSM_CHUNK
    rep = H // bm.shape[2]
    bm = jnp.repeat(bm, rep, axis=2).reshape(b, nc, SSM_CHUNK, H, SSM_STATE)
    cm = jnp.repeat(cm, rep, axis=2).reshape(b, nc, SSM_CHUNK, H, SSM_STATE)
    x = x.reshape(b, nc, SSM_CHUNK, H, P)
    dt = dt.reshape(b, nc, SSM_CHUNK, H)
    acum = jnp.cumsum(dt * a, axis=2)
    xdt = x * dt[..., None]
    tri = jnp.tril(jnp.ones((SSM_CHUNK, SSM_CHUNK), bool))[None, None, :, :, None]
    diff = acum[:, :, :, None, :] - acum[:, :, None, :, :]
    lmat = jnp.exp(jnp.where(tri, diff, -jnp.inf))
    scores = jnp.einsum('bcihn,bcjhn->bcijh', cm, bm) * lmat
    y_diag = jnp.einsum('bcijh,bcjhp->bcihp', scores, xdt)
    decay_end = jnp.exp(acum[:, :, -1:, :] - acum)
    chunk_states = jnp.einsum('bcjhn,bcjh,bcjhp->bchpn', bm, decay_end, xdt)
    chunk_decay = jnp.exp(acum[:, :, -1, :])

    def step(h, inp):
        s, d = inp
        return d[:, :, None, None] * h + s, h

    h_final, h_starts = lax.scan(step, h0.astype(F32),
                                 (jnp.moveaxis(chunk_states, 1, 0), jnp.moveaxis(chunk_decay, 1, 0)))
    h_starts = jnp.moveaxis(h_starts, 0, 1)
    y_off = jnp.einsum('bcihn,bchpn->bcihp', cm, h_starts) * jnp.exp(acum)[..., None]
    return (y_diag + y_off).reshape(b, L, H, P), h_final


def ssm_branch(s_z, s_xbc, s_dt, p, h0):
    B_, L, _ = s_xbc.shape
    xbc = jax.nn.silu(dwconv(s_xbc, p['ssm_conv_w'], p['ssm_conv_b']))
    xs, bm, cm = jnp.split(xbc, (SSM_W, SSM_W + SSM_GROUPS * SSM_STATE), axis=-1)
    xs = xs.astype(F32).reshape(B_, L, SSM_HEADS, SSM_HEAD_DIM)
    bm = bm.astype(F32).reshape(B_, L, SSM_GROUPS, SSM_STATE)
    cm = cm.astype(F32).reshape(B_, L, SSM_GROUPS, SSM_STATE)
    dt = jax.nn.softplus(s_dt.astype(F32).reshape(B_, L, 2, SSM_HEADS) + p['ssm_dt_bias'].astype(F32))
    a = -jnp.exp(p['ssm_a_log'].astype(F32))
    dskip = p['ssm_d'].astype(F32)
    y_f, h_f = ssd(xs, dt[:, :, 0], a[0], bm, cm, h0[:, 0])
    y_b, h_b = ssd(xs[:, ::-1], dt[:, ::-1, 1], a[1], bm[:, ::-1], cm[:, ::-1], h0[:, 1])
    y = y_f + y_b[:, ::-1] + (dskip[0] + dskip[1])[:, None] * xs
    y = y.reshape(B_, L, SSM_W).astype(s_z.dtype)
    y = rmsnorm(y * jax.nn.silu(s_z), p['ssm_norm'])
    return y, jnp.stack([h_f, h_b], axis=1)


def axial_rope_tables(L):
    rows = L // GRID_W
    row = jnp.repeat(jnp.arange(rows), GRID_W)
    col = jnp.tile(jnp.arange(GRID_W), rows)
    pos = jnp.stack([row, col], axis=-1).astype(F32)
    nq = HEAD_DIM // 4
    inv = ROPE_BASE ** (-jnp.arange(nq, dtype=F32) / nq)
    ang = pos[:, :, None] * inv
    return jnp.cos(ang), jnp.sin(ang)


def apply_rope(x, cos, sin):
    B, L, H, _ = x.shape
    xf = x.astype(F32).reshape(B, L, H, 2, 2, HEAD_DIM // 4)
    x1, x2 = xf[..., 0, :], xf[..., 1, :]
    c = cos[None, :, None]
    s = sin[None, :, None]
    out = jnp.stack([x1 * c - x2 * s, x2 * c + x1 * s], axis=-2)
    return out.reshape(B, L, H, HEAD_DIM).astype(x.dtype)


def attend(q, k, v, mask, sink):
    s = jnp.einsum('bqkgd,bskd->bkgqs', q.astype(F32), k.astype(F32)) * (HEAD_DIM ** -0.5)
    if mask is not None:
        s = jnp.where(mask, s, -jnp.inf)
    sk = sink.astype(F32)[None, :, :, None, None]
    m = jnp.maximum(jnp.max(s, axis=-1, keepdims=True), sk)
    e = jnp.exp(s - m)
    prob = e / (jnp.sum(e, axis=-1, keepdims=True) + jnp.exp(sk - m))
    o = jnp.einsum('bkgqs,bskd->bqkgd', prob, v.astype(F32))
    return o.astype(q.dtype)


def ctx_attention(q, k, v, sink):
    B, L = q.shape[:2]
    nb = L // ATT_BLOCK
    qb = q.reshape(B, nb, ATT_BLOCK, KV_HEADS, Q_PER_KV, HEAD_DIM).swapaxes(0, 1)
    out = lax.map(lambda qq: attend(qq, k, v, None, sink), qb)
    return out.swapaxes(0, 1).reshape(B, L, ATT_W)


def latent_attention(q, k, v, ck, cv, sink):
    B, L = q.shape[:2]
    Lc = ck.shape[1]
    nb = L // ATT_BLOCK
    qb = q.reshape(B, nb, ATT_BLOCK, KV_HEADS, Q_PER_KV, HEAD_DIM).swapaxes(0, 1)
    kp = jnp.pad(k, ((0, 0), (WINDOW, WINDOW), (0, 0), (0, 0)))
    vp = jnp.pad(v, ((0, 0), (WINDOW, WINDOW), (0, 0), (0, 0)))
    ctx_ok = jnp.ones((ATT_BLOCK, Lc), bool)

    def blk(args):
        qq, i = args
        start = i * ATT_BLOCK
        kb = lax.dynamic_slice_in_dim(kp, start, BAND, axis=1)
        vb = lax.dynamic_slice_in_dim(vp, start, BAND, axis=1)
        kpos = start - WINDOW + jnp.arange(BAND)
        qpos = start + jnp.arange(ATT_BLOCK)
        band_ok = (jnp.abs(qpos[:, None] - kpos[None]) <= WINDOW) & (kpos >= 0)[None] & (kpos < L)[None]
        mask = jnp.concatenate([ctx_ok, band_ok], axis=1)
        keys = jnp.concatenate([ck.astype(kb.dtype), kb], axis=1)
        vals = jnp.concatenate([cv.astype(vb.dtype), vb], axis=1)
        return attend(qq, keys, vals, mask, sink)

    out = lax.map(blk, (qb, jnp.arange(nb)))
    return out.swapaxes(0, 1).reshape(B, L, ATT_W)


def mixer(u, p, ctx):
    B, L, _ = u.shape
    proj = u @ p['w_in']
    hy_in, s_z, s_xbc, s_dt, a_q, a_k, a_v, g = jnp.split(proj, SPLIT_IDX, axis=-1)
    hy_y = hyena_branch(hy_in, p)
    if ctx is None:
        h0 = jnp.zeros((B, 2, SSM_HEADS, SSM_HEAD_DIM, SSM_STATE), F32)
    else:
        h0 = ctx[2]
    ssm_y, ssm_state = ssm_branch(s_z, s_xbc, s_dt, p, h0)
    q = a_q.reshape(B, L, N_HEADS, HEAD_DIM)
    k = a_k.reshape(B, L, KV_HEADS, HEAD_DIM)
    v = a_v.reshape(B, L, KV_HEADS, HEAD_DIM)
    sink = p['attn_sink'].reshape(KV_HEADS, Q_PER_KV)
    if ctx is None:
        att = ctx_attention(q.reshape(B, L, KV_HEADS, Q_PER_KV, HEAD_DIM), k, v, sink)
        side = (k, v, ssm_state)
    else:
        cos, sin = axial_rope_tables(L)
        q = apply_rope(q, cos, sin)
        k = apply_rope(k, cos, sin)
        att = latent_attention(q.reshape(B, L, KV_HEADS, Q_PER_KV, HEAD_DIM), k, v, ctx[0], ctx[1], sink)
        side = None
    gate = jax.nn.sigmoid(g).reshape(B, L, N_BRANCH, D_MODEL)
    merged = (gate[:, :, 0] * (hy_y @ p['hy_proj'])
              + gate[:, :, 1] * (ssm_y @ p['ssm_proj'])
              + gate[:, :, 2] * (att @ p['attn_proj']))
    return merged @ p['w_out'], side


def ffn(h, p):
    return (jax.nn.silu(h @ p['ffn_w_gate']) * (h @ p['ffn_w_up'])) @ p['ffn_w_down']


def layer(x, mod, p, ctx):
    m = mod[:, None]
    h = rmsnorm(x, p['norm_mix_pre']) * (1.0 + m[:, :, 1]) + m[:, :, 0]
    mix, side = mixer(h, p, ctx)
    x = x + m[:, :, 2] * rmsnorm(mix, p['norm_mix_post'])
    h = rmsnorm(x, p['norm_ffn_pre']) * (1.0 + m[:, :, 4]) + m[:, :, 3]
    x = x + m[:, :, 5] * rmsnorm(ffn(h, p), p['norm_ffn_post'])
    return x, side


def setup_inputs(seed: int = 0) -> dict:
    key = jax.random.key(seed)
    ks = iter(jax.random.split(key, 64))

    def nrm(shape, scale=1.0):
        return scale * jax.random.normal(next(ks), shape, F32)

    def unif(shape, lo, hi):
        return jax.random.uniform(next(ks), shape, F32, lo, hi)

    dt0 = jnp.exp(unif((DEPTH, 2, SSM_HEADS), math.log(1e-3), math.log(1e-1)))
    dt_bias = dt0 + jnp.log(-jnp.expm1(-dt0))
    return {
        'x_prompt': nrm((BATCH, SEQ, D_MODEL)),
        'x_sample': nrm((DEC_BATCH, DEC_SEQ, D_MODEL)),
        'c': nrm((DEC_BATCH, D_MODEL)),
        'cache_k': nrm((DEC_BATCH, DEPTH, PAST_LEN, KV_HEADS, HEAD_DIM)),
        'cache_v': nrm((DEC_BATCH, DEPTH, PAST_LEN, KV_HEADS, HEAD_DIM)),
        'state_ssm': nrm((DEC_BATCH, DEPTH, 2, SSM_HEADS, SSM_HEAD_DIM, SSM_STATE), 0.3),
        'c_ctx': nrm((D_MODEL,)),
        'ada_w': nrm((DEPTH, D_MODEL, 6 * D_MODEL), D_MODEL ** -0.5),
        'ada_b': nrm((DEPTH, 6 * D_MODEL), 0.02),
        'norm_mix_pre': 1.0 + nrm((DEPTH, D_MODEL), 0.05),
        'norm_mix_post': 1.0 + nrm((DEPTH, D_MODEL), 0.05),
        'norm_ffn_pre': 1.0 + nrm((DEPTH, D_MODEL), 0.05),
        'norm_ffn_post': 1.0 + nrm((DEPTH, D_MODEL), 0.05),
        'w_in': nrm((DEPTH, D_MODEL, N_IN), D_MODEL ** -0.5),
        'hy_conv_w': nrm((DEPTH, SHORT_CONV, 3 * HY_W), SHORT_CONV ** -0.5),
        'hy_conv_b': nrm((DEPTH, 3 * HY_W), 0.02),
        'hy_w1': nrm((DEPTH, HY_FEAT, HY_HIDDEN), HY_FEAT ** -0.5),
        'hy_b1': nrm((DEPTH, HY_HIDDEN), 0.1),
        'hy_freq1': 1.0 + nrm((DEPTH, HY_HIDDEN), 0.1),
        'hy_w2': nrm((DEPTH, HY_HIDDEN, HY_HIDDEN), HY_HIDDEN ** -0.5),
        'hy_b2': nrm((DEPTH, HY_HIDDEN), 0.1),
        'hy_freq2': 1.0 + nrm((DEPTH, HY_HIDDEN), 0.1),
        'hy_w3': nrm((DEPTH, HY_HIDDEN, 4 * HY_W), 0.1 * HY_HIDDEN ** -0.5),
        'hy_decay': jnp.exp(unif((DEPTH, 4 * HY_W), math.log(3.0), math.log(15.0))),
        'hy_bias': nrm((DEPTH, 2, HY_W), 0.5),
        'ssm_conv_w': nrm((DEPTH, SHORT_CONV, XBC_W), SHORT_CONV ** -0.5),
        'ssm_conv_b': nrm((DEPTH, XBC_W), 0.02),
        'ssm_a_log': jnp.log(unif((DEPTH, 2, SSM_HEADS), 1.0, 16.0)),
        'ssm_dt_bias': dt_bias,
        'ssm_d': 1.0 + nrm((DEPTH, 2, SSM_HEADS), 0.1),
        'ssm_norm': 1.0 + nrm((DEPTH, SSM_W), 0.05),
        'attn_sink': nrm((DEPTH, N_HEADS), 0.5),
        'hy_proj': nrm((DEPTH, HY_W, D_MODEL), HY_W ** -0.5),
        'ssm_proj': nrm((DEPTH, SSM_W, D_MODEL), SSM_W ** -0.5),
        'attn_proj': nrm((DEPTH, ATT_W, D_MODEL), ATT_W ** -0.5),
        'w_out': nrm((DEPTH, D_MODEL, D_MODEL), D_MODEL ** -0.5),
        'ffn_w_gate': nrm((DEPTH, D_MODEL, FFN_HIDDEN), D_MODEL ** -0.5),
        'ffn_w_up': nrm((DEPTH, D_MODEL, FFN_HIDDEN), D_MODEL ** -0.5),
        'ffn_w_down': nrm((DEPTH, FFN_HIDDEN, D_MODEL), FFN_HIDDEN ** -0.5),
    }


def reference(x_prompt, x_sample, c, cache_k, cache_v, state_ssm, c_ctx, ada_w, ada_b,
              norm_mix_pre, norm_mix_post, norm_ffn_pre, norm_ffn_post, w_in,
              hy_conv_w, hy_conv_b, hy_w1, hy_b1, hy_freq1, hy_w2, hy_b2, hy_freq2, hy_w3,
              hy_decay, hy_bias, ssm_conv_w, ssm_conv_b, ssm_a_log, ssm_dt_bias, ssm_d, ssm_norm,
              attn_sink, hy_proj, ssm_proj, attn_proj, w_out, ffn_w_gate, ffn_w_up, ffn_w_down):
    stacked = {
        'ada_w': ada_w, 'ada_b': ada_b,
        'norm_mix_pre': norm_mix_pre, 'norm_mix_post': norm_mix_post,
        'norm_ffn_pre': norm_ffn_pre, 'norm_ffn_post': norm_ffn_post,
        'w_in': w_in, 'hy_conv_w': hy_conv_w, 'hy_conv_b': hy_conv_b,
        'hy_w1': hy_w1, 'hy_b1': hy_b1, 'hy_freq1': hy_freq1,
        'hy_w2': hy_w2, 'hy_b2': hy_b2, 'hy_freq2': hy_freq2,
        'hy_w3': hy_w3, 'hy_decay': hy_decay, 'hy_bias': hy_bias,
        'ssm_conv_w': ssm_conv_w, 'ssm_conv_b': ssm_conv_b, 'ssm_a_log': ssm_a_log,
        'ssm_dt_bias': ssm_dt_bias, 'ssm_d': ssm_d, 'ssm_norm': ssm_norm,
        'attn_sink': attn_sink, 'hy_proj': hy_proj, 'ssm_proj': ssm_proj, 'attn_proj': attn_proj,
        'w_out': w_out, 'ffn_w_gate': ffn_w_gate, 'ffn_w_up': ffn_w_up, 'ffn_w_down': ffn_w_down,
    }
    y_prompt = x_prompt
    y_sample = x_sample
    ks, vs, ss = [], [], []
    for l in range(DEPTH):
        p = {name: arr[l] for name, arr in stacked.items()}
        mod_ctx = adaln(c_ctx[None], p['ada_w'], p['ada_b'])
        y_prompt, (k_l, v_l, s_l) = layer(y_prompt, mod_ctx, p, None)
        ks.append(k_l)
        vs.append(v_l)
        ss.append(s_l)
        mod_lat = adaln(c, p['ada_w'], p['ada_b'])
        y_sample, _ = layer(y_sample, mod_lat, p, (cache_k[:, l], cache_v[:, l], state_ssm[:, l]))
    new_k = jnp.stack(ks, axis=1)
    new_v = jnp.stack(vs, axis=1)
    new_ssm = jnp.stack(ss, axis=1)
    return (y_prompt, y_sample, new_k, new_v, new_ssm)
```

```python
import functools
import math

import jax
import jax.numpy as jnp
from jax import lax
from jax.experimental import pallas as pl
from jax.experimental.pallas import tpu as pltpu

F32 = jnp.float32
BF16 = jnp.bfloat16

D_MODEL = 1024
DEPTH = 2
GRID_W = 64
HY_W = 512
HY_BANDS = 16
HY_FEAT = 1 + 2 * HY_BANDS
HY_HIDDEN = 64
SSM_W = 512
SSM_HEAD_DIM = 64
SSM_HEADS = 8
SSM_GROUPS = 2
SSM_STATE = 64
SSM_CHUNK = 128
XBC_W = SSM_W + 2 * SSM_GROUPS * SSM_STATE
N_HEADS = 8
KV_HEADS = 2
HEAD_DIM = 64
Q_PER_KV = N_HEADS // KV_HEADS
ATT_W = N_HEADS * HEAD_DIM
KV_W = KV_HEADS * HEAD_DIM
WINDOW = 128
ATT_BLOCK = 128
ROPE_BASE = 10000.0
FFN_HIDDEN = ((8 * D_MODEL + 3 * 256 - 1) // (3 * 256)) * 256
EPS = 1e-6

LANES = 128
ROW_TILE = 256
DFT_TILE = 256
IN_SPLIT = (3 * HY_W, SSM_W, XBC_W, 2 * SSM_HEADS, ATT_W, KV_W, KV_W, 3 * D_MODEL)
IN_WIDTHS = (3 * HY_W, SSM_W, XBC_W, LANES, ATT_W, KV_W, KV_W, 3 * D_MODEL)
VMEM_LIMIT = 56 * 1024 * 1024


def _params(n_grid):
    return pltpu.CompilerParams(dimension_semantics=("arbitrary",) * n_grid, vmem_limit_bytes=VMEM_LIMIT)


def _resident(shape):
    nd = len(shape)
    return pl.BlockSpec(shape, lambda *_: (0,) * nd, pipeline_mode=pl.Buffered(1))


def _bdot(a, b):
    return jnp.dot(a.astype(BF16), b.astype(BF16), preferred_element_type=F32)


def _dot_nt(a, b):
    return lax.dot_general(a.astype(BF16), b.astype(BF16), (((1,), (1,)), ((), ())), preferred_element_type=F32)


def _dot_tn(a, b):
    return lax.dot_general(a.astype(BF16), b.astype(BF16), (((0,), (0,)), ((), ())), preferred_element_type=F32)


def _split2(x):
    hi = x.astype(BF16)
    lo = (x - hi.astype(F32)).astype(BF16)
    return hi, lo


def _split3(x):
    h1 = x.astype(BF16)
    r1 = x - h1.astype(F32)
    h2 = r1.astype(BF16)
    h3 = (r1 - h2.astype(F32)).astype(BF16)
    return h1, h2, h3


def _dot3(a, b):
    ah, al = _split2(a)
    bh, bl = _split2(b)
    return (jnp.dot(ah, bh, preferred_element_type=F32) + jnp.dot(ah, bl, preferred_element_type=F32)
            + jnp.dot(al, bh, preferred_element_type=F32))


def _dot3_pre(ah, al, bh, bl):
    return (jnp.dot(ah, bh, preferred_element_type=F32) + jnp.dot(ah, bl, preferred_element_type=F32)
            + jnp.dot(al, bh, preferred_element_type=F32))


def _rms(x, g):
    return x * lax.rsqrt(jnp.mean(x * x, axis=-1, keepdims=True) + EPS) * g


def _silu(x):
    return x * jax.nn.sigmoid(x)


def _dwconv(x, w, b):
    n = x.shape[0]
    row = lax.broadcasted_iota(jnp.int32, (n, 1), 0)
    prev = jnp.where(row == 0, 0.0, pltpu.roll(x, 1, 0))
    nxt = jnp.where(row == n - 1, 0.0, pltpu.roll(x, n - 1, 0))
    return prev * w[0:1, :] + x * w[1:2, :] + nxt * w[2:3, :] + b


def _ada_kernel(cond_ref, w_ref, b_ref, o_ref):
    o_ref[0] = _bdot(_silu(cond_ref[...]), w_ref[0]) + b_ref[0]


def _ada_call(cond, ada_w, ada_b):
    n = 6 * D_MODEL
    tn = n // 4
    rows = cond.shape[0]
    return pl.pallas_call(
        _ada_kernel,
        grid=(DEPTH, n // tn),
        in_specs=[
            pl.BlockSpec((rows, D_MODEL), lambda l, j: (0, 0)),
            pl.BlockSpec((1, D_MODEL, tn), lambda l, j: (l, 0, j)),
            pl.BlockSpec((1, 1, tn), lambda l, j: (l, 0, j)),
        ],
        out_specs=pl.BlockSpec((1, rows, tn), lambda l, j: (l, 0, j)),
        out_shape=jax.ShapeDtypeStruct((DEPTH, rows, n), F32),
        compiler_params=_params(2),
        name="adaln",
    )(cond, ada_w, ada_b.reshape(DEPTH, 1, n))


def _in_kernel(x_ref, mod_ref, g_ref, w_ref, *o_refs):
    h = _rms(x_ref[...], g_ref[...]) * (1.0 + mod_ref[0, 1:2, :]) + mod_ref[0, 0:1, :]
    hb = h.astype(BF16)
    off = 0
    for o_ref, wd in zip(o_refs, IN_WIDTHS):
        o_ref[...] = jnp.dot(hb, w_ref[:, off:off + wd], preferred_element_type=F32)
        off += wd


def _mod_spec(shared):
    if shared:
        return pl.BlockSpec((1, 6, D_MODEL), lambda b, i: (0, 0, 0))
    return pl.BlockSpec((1, 6, D_MODEL), lambda b, i: (b, 0, 0))


def _in_call(x, mod, g, w, nb, sl, shared_mod):
    nt = sl // ROW_TILE
    n_tok = nb * sl
    row = lambda b, i: (b * nt + i, 0)
    return pl.pallas_call(
        _in_kernel,
        grid=(nb, nt),
        in_specs=[
            pl.BlockSpec((ROW_TILE, D_MODEL), row),
            _mod_spec(shared_mod),
            _resident((1, D_MODEL)),
            _resident((D_MODEL, sum(IN_WIDTHS))),
        ],
        out_specs=[pl.BlockSpec((ROW_TILE, wd), row) for wd in IN_WIDTHS],
        out_shape=[jax.ShapeDtypeStruct((n_tok, wd), F32) for wd in IN_WIDTHS],
        compiler_params=_params(2),
        name="in_proj",
    )(x, mod, g, w)


def _filter_kernel(feat_ref, w1, b1, f1, w2, b2, f2, w3, dec, ch, cl, sh, sl, kre_ref, kim_ref, *, sl_len):
    feat = feat_ref[...]
    h = jnp.sin(f1[...] * (_dot3(feat, w1[...]) + b1[...]))
    h = jnp.sin(f2[...] * (_dot3(h, w2[...]) + b2[...]))
    t = feat[:, 0:1]
    filt = _dot3(h, w3[...]) * jnp.exp(-t * dec[...])
    row = lax.broadcasted_iota(jnp.int32, (sl_len, 1), 0)
    hf = filt[:, :HY_W]
    hb = jnp.where(row == 0, 0.0, filt[:, HY_W:])
    ksum = hf + hb
    kdiff = hb - hf
    sh_, sl_ = _split2(ksum)
    kre = _dot3_pre(ch[...], cl[...], sh_, sl_)
    dh_, dl_ = _split2(kdiff)
    kim = _dot3_pre(sh[...], sl[...], dh_, dl_)
    nyq = jnp.sum(jnp.where(row % 2 == 1, -ksum, ksum), axis=0, keepdims=True)
    kim = jnp.where(row == 0, nyq, kim)
    scale = jnp.where(row == 0, 0.5 / sl_len, 1.0 / sl_len)
    kre_ref[0] = kre * scale
    kim_ref[0] = kim * scale


def _filter_call(feat, w1, b1, f1, w2, b2, f2, w3, dec, mats, sl):
    ch, cl, sh, sl_, _, _ = mats
    hid = HY_HIDDEN
    return pl.pallas_call(
        functools.partial(_filter_kernel, sl_len=sl),
        grid=(2,),
        in_specs=[
            _resident((sl, LANES)), _resident((LANES, hid)), _resident((1, hid)), _resident((1, hid)),
            _resident((hid, hid)), _resident((1, hid)), _resident((1, hid)),
            pl.BlockSpec((hid, 2 * HY_W), lambda o: (0, o)),
            pl.BlockSpec((1, 2 * HY_W), lambda o: (0, o)),
            _resident((sl, sl)), _resident((sl, sl)), _resident((sl, sl)), _resident((sl, sl)),
        ],
        out_specs=[pl.BlockSpec((1, sl, HY_W), lambda o: (o, 0, 0))] * 2,
        out_shape=[jax.ShapeDtypeStruct((2, sl, HY_W), F32)] * 2,
        compiler_params=_params(1),
        name="hyena_filter",
    )(feat, w1, b1, f1, w2, b2, f2, w3, dec, ch, cl, sh, sl_)


def _hyena_kernel(x_ref, cw, cb, bias, kre, kim, ch, cl, sh, sl, sth, stl, o_ref,
                  hc, uh, ul, prh, prl, pih, pil, *, sl_len):
    tile = min(DFT_TILE, sl_len)
    nt = sl_len // tile
    hc[...] = _dwconv(x_ref[...], cw[...], cb[...])
    vh, vl = _split2(hc[:, 0:HY_W])
    uh[...] = vh
    ul[...] = vl

    def conv(order, epilogue):
        def fwd(i, carry):
            r = pl.multiple_of(i * tile, tile)
            rows = pl.ds(r, tile)
            a = _dot3_pre(ch[rows, :], cl[rows, :], uh[...], ul[...])
            b = _dot3_pre(sh[rows, :], sl[rows, :], uh[...], ul[...])
            kr = kre[order, rows, :]
            ki = kim[order, rows, :]
            first = (r + lax.broadcasted_iota(jnp.int32, (tile, 1), 0)) == 0
            pre = jnp.where(first, a * kr, a * kr + b * ki)
            pim = jnp.where(first, b * ki, b * kr - a * ki)
            prh[rows, :], prl[rows, :] = _split2(pre)
            pih[rows, :], pil[rows, :] = _split2(pim)
            return carry

        lax.fori_loop(0, nt, fwd, 0)

        def inv(i, carry):
            r = pl.multiple_of(i * tile, tile)
            rows = pl.ds(r, tile)
            y = (_dot3_pre(ch[rows, :], cl[rows, :], prh[...], prl[...])
                 + _dot3_pre(sth[rows, :], stl[rows, :], pih[...], pil[...]))
            epilogue(rows, y)
            return carry

        lax.fori_loop(0, nt, inv, 0)

    def after_first(rows, y):
        z = hc[rows, HY_W:2 * HY_W] * (y + hc[rows, 0:HY_W] * bias[0:1, :])
        hc[rows, 0:HY_W] = z
        uh[rows, :], ul[rows, :] = _split2(z)

    def after_second(rows, y):
        o_ref[rows, :] = hc[rows, 2 * HY_W:3 * HY_W] * (y + hc[rows, 0:HY_W] * bias[1:2, :])

    conv(0, after_first)
    conv(1, after_second)


def _hyena_call(hy_in, cw, cb, bias, kre, kim, mats, nb, sl):
    seq = lambda b: (b, 0)
    mat = _resident((sl, sl))
    return pl.pallas_call(
        functools.partial(_hyena_kernel, sl_len=sl),
        grid=(nb,),
        in_specs=[
            pl.BlockSpec((sl, 3 * HY_W), seq, pipeline_mode=pl.Buffered(1)),
            _resident((3, 3 * HY_W)), _resident((1, 3 * HY_W)), _resident((2, HY_W)),
            _resident((2, sl, HY_W)), _resident((2, sl, HY_W)),
            mat, mat, mat, mat, mat, mat,
        ],
        out_specs=pl.BlockSpec((sl, HY_W), seq),
        out_shape=jax.ShapeDtypeStruct((nb * sl, HY_W), F32),
        scratch_shapes=[pltpu.VMEM((sl, 3 * HY_W), F32)] + [pltpu.VMEM((sl, HY_W), BF16)] * 6,
        compiler_params=_params(1),
        name="hyena_conv",
    )(hy_in, cw, cb, bias, kre, kim, *mats)


def _ssd_kernel(xbc_ref, dt_ref, h0_ref, cw, cb, dtb, alog, dsk, o_ref, hfin_ref, xbc, dt, dta, hst, *, sl_len):
    nc = sl_len // SSM_CHUNK
    q = SSM_CHUNK
    xbc[...] = _silu(_dwconv(xbc_ref[...], cw[...], cb[...]))
    dtv = jax.nn.softplus(dt_ref[...] + dtb[...])
    dt[...] = dtv
    dta[...] = dtv * (-jnp.exp(alog[...]))
    hst[...] = h0_ref[0]
    ii = lax.broadcasted_iota(jnp.int32, (q, q), 0)
    jj = lax.broadcasted_iota(jnp.int32, (q, q), 1)

    def chunk(c, d):
        r = pl.multiple_of(c * q, q)
        rows = pl.ds(r, q)
        xs = xbc[rows, 0:SSM_W]
        bm = xbc[rows, SSM_W:SSM_W + SSM_GROUPS * SSM_STATE]
        cm = xbc[rows, SSM_W + SSM_GROUPS * SSM_STATE:XBC_W]
        dtc = dt[rows, :]
        tri = (jj <= ii) if d == 0 else (jj >= ii)
        trib = tri.astype(BF16)
        a1, a2, a3 = _split3(dta[rows, :])
        acum = (jnp.dot(trib, a1, preferred_element_type=F32) + jnp.dot(trib, a2, preferred_element_type=F32)
                + jnp.dot(trib, a3, preferred_element_type=F32))
        acum_t = acum.T
        last = acum[q - 1:q, :] if d == 0 else acum[0:1, :]
        ys = []
        for g in range(SSM_GROUPS):
            bg = bm[:, g * SSM_STATE:(g + 1) * SSM_STATE]
            cg = cm[:, g * SSM_STATE:(g + 1) * SSM_STATE]
            cb_t = _dot_nt(cg, bg)
            for hh in range(SSM_HEADS // SSM_GROUPS):
                h = g * (SSM_HEADS // SSM_GROUPS) + hh
                col = d * SSM_HEADS + h
                a_col = acum[:, col:col + 1]
                a_row = acum_t[col:col + 1, :]
                a_last = last[:, col:col + 1]
                lmat = jnp.exp(jnp.where(tri, a_col - a_row, -jnp.inf))
                xdt = xs[:, h * SSM_HEAD_DIM:(h + 1) * SSM_HEAD_DIM] * dtc[:, col:col + 1]
                y_diag = _bdot(cb_t * lmat, xdt)
                states = _dot_tn(xdt * jnp.exp(a_last - a_col), bg)
                h_prev = hst[col]
                y_off = _dot_nt(cg, h_prev) * jnp.exp(a_col)
                hst[col] = jnp.exp(a_last) * h_prev + states
                ys.append(y_diag + y_off)
        y = jnp.concatenate(ys, axis=1)
        if d == 0:
            o_ref[rows, :] = y + dsk[...] * xs
        else:
            o_ref[rows, :] += y

    def fwd(c, carry):
        chunk(c, 0)
        return carry

    def bwd(c, carry):
        chunk(nc - 1 - c, 1)
        return carry

    lax.fori_loop(0, nc, fwd, 0)
    lax.fori_loop(0, nc, bwd, 0)
    hfin_ref[0] = hst[...]


def _ssd_call(s_xbc, s_dt, h0, cw, cb, dtb, alog, dsk, nb, sl, shared_h0):
    seq = lambda b: (b, 0)
    n_st = 2 * SSM_HEADS
    st_shape = (1, n_st, SSM_HEAD_DIM, SSM_STATE)
    h0_map = (lambda b: (0, 0, 0, 0)) if shared_h0 else (lambda b: (b, 0, 0, 0))
    return pl.pallas_call(
        functools.partial(_ssd_kernel, sl_len=sl),
        grid=(nb,),
        in_specs=[
            pl.BlockSpec((sl, XBC_W), seq), pl.BlockSpec((sl, LANES), seq), pl.BlockSpec(st_shape, h0_map),
            _resident((3, XBC_W)), _resident((1, XBC_W)), _resident((1, LANES)), _resident((1, LANES)),
            _resident((1, SSM_W)),
        ],
        out_specs=[pl.BlockSpec((sl, SSM_W), seq), pl.BlockSpec(st_shape, lambda b: (b, 0, 0, 0))],
        out_shape=[jax.ShapeDtypeStruct((nb * sl, SSM_W), F32),
                   jax.ShapeDtypeStruct((nb, n_st, SSM_HEAD_DIM, SSM_STATE), F32)],
        scratch_shapes=[pltpu.VMEM((sl, XBC_W), F32), pltpu.VMEM((sl, LANES), F32), pltpu.VMEM((sl, LANES), F32),
                        pltpu.VMEM((n_st, SSM_HEAD_DIM, SSM_STATE), F32)],
        compiler_params=_params(1),
        name="ssd_scan",
    )(s_xbc, s_dt, h0, cw, cb, dtb, alog, dsk)


def _softmax_av(q4, sink_col, pieces):
    logits = []
    m = sink_col
    for kk, _, mask in pieces:
        s = _dot_nt(q4, kk)
        if mask is not None:
            s = jnp.where(mask, s, -jnp.inf)
        logits.append(s)
        m = jnp.maximum(m, jnp.max(s, axis=-1, keepdims=True))
    denom = jnp.exp(sink_col - m)
    acc = None
    for s, (_, vv, _) in zip(logits, pieces):
        e = jnp.exp(s - m)
        denom = denom + jnp.sum(e, axis=-1, keepdims=True)
        pv = _bdot(e, vv)
        acc = pv if acc is None else acc + pv
    return acc / denom


def _sink_col(sink_ref, g, rows):
    return jnp.concatenate(
        [jnp.broadcast_to(sink_ref[g * Q_PER_KV + hq:g * Q_PER_KV + hq + 1, 0:1], (rows, 1)) for hq in range(Q_PER_KV)],
        axis=0)


def _ctx_attn_kernel(q_ref, k_ref, v_ref, sink_ref, o_ref, *, sl_len):
    scale = HEAD_DIM ** -0.5
    for g in range(KV_HEADS):
        kg = k_ref[:, g * HEAD_DIM:(g + 1) * HEAD_DIM]
        vg = v_ref[:, g * HEAD_DIM:(g + 1) * HEAD_DIM]
        q4 = jnp.concatenate(
            [q_ref[:, (g * Q_PER_KV + hq) * HEAD_DIM:(g * Q_PER_KV + hq + 1) * HEAD_DIM] for hq in range(Q_PER_KV)],
            axis=0) * scale
        o = _softmax_av(q4, _sink_col(sink_ref, g, sl_len), [(kg, vg, None)])
        for hq in range(Q_PER_KV):
            h = g * Q_PER_KV + hq
            o_ref[:, h * HEAD_DIM:(h + 1) * HEAD_DIM] = o[hq * sl_len:(hq + 1) * sl_len, :]


def _ctx_attn_call(q, k, v, sink, nb, sl):
    seq = lambda b: (b, 0)
    return pl.pallas_call(
        functools.partial(_ctx_attn_kernel, sl_len=sl),
        grid=(nb,),
        in_specs=[pl.BlockSpec((sl, ATT_W), seq), pl.BlockSpec((sl, KV_W), seq), pl.BlockSpec((sl, KV_W), seq),
                  _resident((N_HEADS, LANES))],
        out_specs=pl.BlockSpec((sl, ATT_W), seq),
        out_shape=jax.ShapeDtypeStruct((nb * sl, ATT_W), F32),
        compiler_params=_params(1),
        name="ctx_attention",
    )(q, k, v, sink)


def _rope(x, cos, sin_signed):
    w = x.shape[1]
    lane = lax.broadcasted_iota(jnp.int32, (1, w), 1)
    partner = jnp.where(lane % 32 < 16, pltpu.roll(x, w - 16, 1), pltpu.roll(x, 16, 1))
    return x * cos + partner * sin_signed


def _lat_attn_kernel(q_ref, k_ref, v_ref, ck_ref, cv_ref, cos_ref, sin_ref, sink_ref, o_ref, qs, ks, *, sl_len):
    blk = ATT_BLOCK
    nblk = sl_len // blk
    scale = HEAD_DIM ** -0.5
    qs[...] = (_rope(q_ref[...], cos_ref[...], sin_ref[...]) * scale).astype(BF16)
    ks[...] = _rope(k_ref[...], cos_ref[:, 0:KV_W], sin_ref[:, 0:KV_W]).astype(BF16)
    qi = lax.broadcasted_iota(jnp.int32, (Q_PER_KV * blk, blk), 0) % blk
    kj = lax.broadcasted_iota(jnp.int32, (Q_PER_KV * blk, blk), 1)
    for i in range(nblk):
        rows = slice(i * blk, (i + 1) * blk)
        for g in range(KV_HEADS):
            cols = slice(g * HEAD_DIM, (g + 1) * HEAD_DIM)
            q4 = jnp.concatenate(
                [qs[rows, (g * Q_PER_KV + hq) * HEAD_DIM:(g * Q_PER_KV + hq + 1) * HEAD_DIM] for hq in range(Q_PER_KV)],
                axis=0)
            pieces = [(ck_ref[0, :, cols], cv_ref[0, :, cols], None)]
            if i > 0:
                prev = slice((i - 1) * blk, i * blk)
                pieces.append((ks[prev, cols], v_ref[prev, cols], kj >= qi))
            pieces.append((ks[rows, cols], v_ref[rows, cols], None))
            if i < nblk - 1:
                nxt = slice((i + 1) * blk, (i + 2) * blk)
                pieces.append((ks[nxt, cols], v_ref[nxt, cols], kj <= qi))
            o = _softmax_av(q4, _sink_col(sink_ref, g, blk), pieces)
            for hq in range(Q_PER_KV):
                h = g * Q_PER_KV + hq
                o_ref[rows, h * HEAD_DIM:(h + 1) * HEAD_DIM] = o[hq * blk:(hq + 1) * blk, :]


def _lat_attn_call(q, k, v, ck, cv, cos, sin, sink, nb, sl):
    seq = lambda b: (b, 0)
    past = ck.shape[1]
    cache = pl.BlockSpec((1, past, KV_W), lambda b: (b, 0, 0))
    return pl.pallas_call(
        functools.partial(_lat_attn_kernel, sl_len=sl),
        grid=(nb,),
        in_specs=[pl.BlockSpec((sl, ATT_W), seq), pl.BlockSpec((sl, KV_W), seq), pl.BlockSpec((sl, KV_W), seq),
                  cache, cache, _resident((sl, ATT_W)), _resident((sl, ATT_W)), _resident((N_HEADS, LANES))],
        out_specs=pl.BlockSpec((sl, ATT_W), seq),
        out_shape=jax.ShapeDtypeStruct((nb * sl, ATT_W), F32),
        scratch_shapes=[pltpu.VMEM((sl, ATT_W), BF16), pltpu.VMEM((sl, KV_W), BF16)],
        compiler_params=_params(1),
        name="latent_attention",
    )(q, k, v, ck, cv, cos, sin, sink)


def _post_kernel(x_ref, hy_ref, sy_ref, sz_ref, at_ref, g_ref, mod_ref, snorm, npost, nfpre, nfpost,
                 wh, ws, wa, wo, wg, wu, wd, o_ref):
    m = mod_ref[0]
    ssm_y = _rms(sy_ref[...] * _silu(sz_ref[...]), snorm[...])
    merged = (jax.nn.sigmoid(g_ref[:, 0:D_MODEL]) * _bdot(hy_ref[...], wh[...])
              + jax.nn.sigmoid(g_ref[:, D_MODEL:2 * D_MODEL]) * _bdot(ssm_y, ws[...])
              + jax.nn.sigmoid(g_ref[:, 2 * D_MODEL:3 * D_MODEL]) * _bdot(at_ref[...], wa[...]))
    x1 = x_ref[...] + m[2:3, :] * _rms(_bdot(merged, wo[...]), npost[...])
    hb = (_rms(x1, nfpre[...]) * (1.0 + m[4:5, :]) + m[3:4, :]).astype(BF16)
    gate = jnp.dot(hb, wg[...], preferred_element_type=F32)
    up = jnp.dot(hb, wu[...], preferred_element_type=F32)
    f = _bdot(_silu(gate) * up, wd[...])
    o_ref[...] = x1 + m[5:6, :] * _rms(f, nfpost[...])


def _post_call(x, hy, sy, sz, at, g, mod, snorm, npost, nfpre, nfpost, wh, ws, wa, wo, wg, wu, wd, nb, sl, shared_mod):
    nt = sl // ROW_TILE
    row = lambda b, i: (b * nt + i, 0)
    tok = lambda wdt: pl.BlockSpec((ROW_TILE, wdt), row)
    vec = lambda wdt: _resident((1, wdt))
    return pl.pallas_call(
        _post_kernel,
        grid=(nb, nt),
        in_specs=[tok(D_MODEL), tok(HY_W), tok(SSM_W), tok(SSM_W), tok(ATT_W), tok(3 * D_MODEL), _mod_spec(shared_mod),
                  vec(SSM_W), vec(D_MODEL), vec(D_MODEL), vec(D_MODEL),
                  _resident((HY_W, D_MODEL)), _resident((SSM_W, D_MODEL)), _resident((ATT_W, D_MODEL)),
                  _resident((D_MODEL, D_MODEL)), _resident((D_MODEL, FFN_HIDDEN)), _resident((D_MODEL, FFN_HIDDEN)),
                  _resident((FFN_HIDDEN, D_MODEL))],
        out_specs=tok(D_MODEL),
        out_shape=jax.ShapeDtypeStruct((nb * sl, D_MODEL), F32),
        compiler_params=_params(2),
        name="merge_ffn",
    )(x, hy, sy, sz, at, g, mod, snorm, npost, nfpre, nfpost, wh, ws, wa, wo, wg, wu, wd)


def _dft_mats(sl):
    f = jnp.arange(sl, dtype=jnp.int32)
    ang = ((f[:, None] * f[None, :]) % (2 * sl)).astype(F32) * (math.pi / sl)
    cos = jnp.cos(ang)
    sign = jnp.where(f % 2 == 1, -1.0, 1.0).astype(F32)
    sinp = jnp.sin(ang).at[0, :].set(sign)
    return (*_split2(cos), *_split2(sinp), *_split2(sinp.T))


def _filter_features(sl):
    t = jnp.arange(sl, dtype=F32) / sl
    ang = 2.0 * jnp.pi * t[:, None] * jnp.arange(1, HY_BANDS + 1, dtype=F32)[None]
    feat = jnp.concatenate([t[:, None], jnp.cos(ang), jnp.sin(ang)], axis=-1)
    return jnp.pad(feat, ((0, 0), (0, LANES - HY_FEAT)))


def _rope_tables(sl):
    rows = sl // GRID_W
    row = jnp.repeat(jnp.arange(rows), GRID_W).astype(F32)
    col = jnp.tile(jnp.arange(GRID_W), rows).astype(F32)
    nq = HEAD_DIM // 4
    inv = ROPE_BASE ** (-jnp.arange(nq, dtype=F32) / nq)
    ar = row[:, None] * inv
    ac = col[:, None] * inv
    cos = jnp.concatenate([jnp.cos(ar), jnp.cos(ar), jnp.cos(ac), jnp.cos(ac)], axis=-1)
    sin = jnp.concatenate([-jnp.sin(ar), jnp.sin(ar), -jnp.sin(ac), jnp.sin(ac)], axis=-1)
    return jnp.tile(cos, (1, N_HEADS)), jnp.tile(sin, (1, N_HEADS))


def _pack_w_in(w):
    pieces = []
    off = 0
    for wd, padded in zip(IN_SPLIT, IN_WIDTHS):
        pieces.append(jnp.pad(w[:, off:off + wd], ((0, 0), (0, padded - wd))))
        off += wd
    return jnp.concatenate(pieces, axis=1).astype(BF16)


def _pad_lanes(v):
    return jnp.pad(v.reshape(1, -1), ((0, 0), (0, LANES - v.size)))


def kernel(x_prompt, x_sample, c, cache_k, cache_v, state_ssm, c_ctx, ada_w, ada_b,
           norm_mix_pre, norm_mix_post, norm_ffn_pre, norm_ffn_post, w_in,
           hy_conv_w, hy_conv_b, hy_w1, hy_b1, hy_freq1, hy_w2, hy_b2, hy_freq2, hy_w3,
           hy_decay, hy_bias, ssm_conv_w, ssm_conv_b, ssm_a_log, ssm_dt_bias, ssm_d, ssm_norm,
           attn_sink, hy_proj, ssm_proj, attn_proj, w_out, ffn_w_gate, ffn_w_up, ffn_w_down):
    nb_c, sl_c, _ = x_prompt.shape
    nb_l, sl_l, _ = x_sample.shape
    past = cache_k.shape[2]
    n_st = 2 * SSM_HEADS

    cond = jnp.concatenate([c_ctx[None], c, jnp.zeros((8 - 1 - nb_l, D_MODEL), F32)], axis=0)
    mod = _ada_call(cond, ada_w, ada_b).reshape(DEPTH, 8, 6, D_MODEL)

    streams = {
        "ctx": dict(nb=nb_c, sl=sl_c, x=x_prompt.reshape(nb_c * sl_c, D_MODEL), mats=_dft_mats(sl_c),
                    feat=_filter_features(sl_c)),
        "lat": dict(nb=nb_l, sl=sl_l, x=x_sample.reshape(nb_l * sl_l, D_MODEL), mats=_dft_mats(sl_l),
                    feat=_filter_features(sl_l)),
    }
    rope_cos, rope_sin = _rope_tables(sl_l)
    zero_state = jnp.zeros((1, n_st, SSM_HEAD_DIM, SSM_STATE), F32)
    row = lambda v: v.reshape(1, -1)

    ks, vs, ss = [], [], []
    for l in range(DEPTH):
        w_in_p = _pack_w_in(w_in[l])
        w1p = jnp.pad(hy_w1[l], ((0, LANES - HY_FEAT), (0, 0)))
        dtb = _pad_lanes(ssm_dt_bias[l])
        alog = _pad_lanes(ssm_a_log[l])
        dsk = jnp.repeat(ssm_d[l, 0] + ssm_d[l, 1], SSM_HEAD_DIM).reshape(1, SSM_W)
        sink = jnp.broadcast_to(attn_sink[l][:, None], (N_HEADS, LANES))
        wh, ws, wa, wo = (w[l].astype(BF16) for w in (hy_proj, ssm_proj, attn_proj, w_out))
        wg, wu, wd = (w[l].astype(BF16) for w in (ffn_w_gate, ffn_w_up, ffn_w_down))
        for name in ("ctx", "lat"):
            st = streams[name]
            nb, sl, is_ctx = st["nb"], st["sl"], name == "ctx"
            m = mod[l, 0:1] if is_ctx else mod[l, 1:1 + nb]
            kre, kim = _filter_call(st["feat"], w1p, row(hy_b1[l]), row(hy_freq1[l]), hy_w2[l], row(hy_b2[l]),
                                    row(hy_freq2[l]), hy_w3[l], row(hy_decay[l]), st["mats"], sl)
            hy_in, s_z, s_xbc, s_dt, a_q, a_k, a_v, gate = _in_call(
                st["x"], m, row(norm_mix_pre[l]), w_in_p, nb, sl, is_ctx)
            hy_y = _hyena_call(hy_in, hy_conv_w[l], row(hy_conv_b[l]), hy_bias[l], kre, kim, st["mats"], nb, sl)
            h0 = zero_state if is_ctx else state_ssm[:, l].reshape(nb, n_st, SSM_HEAD_DIM, SSM_STATE)
            ssm_y, h_fin = _ssd_call(s_xbc, s_dt, h0, ssm_conv_w[l], row(ssm_conv_b[l]), dtb, alog, dsk, nb, sl, is_ctx)
            if is_ctx:
                att = _ctx_attn_call(a_q, a_k, a_v, sink, nb, sl)
                ks.append(a_k.reshape(nb, sl, KV_HEADS, HEAD_DIM))
                vs.append(a_v.reshape(nb, sl, KV_HEADS, HEAD_DIM))
                ss.append(h_fin.reshape(nb, 2, SSM_HEADS, SSM_HEAD_DIM, SSM_STATE))
            else:
                att = _lat_attn_call(a_q, a_k, a_v, cache_k[:, l].reshape(nb, past, KV_W),
                                     cache_v[:, l].reshape(nb, past, KV_W), rope_cos, rope_sin, sink, nb, sl)
            st["x"] = _post_call(st["x"], hy_y, ssm_y, s_z, att, gate, m, row(ssm_norm[l]), row(norm_mix_post[l]),
                                 row(norm_ffn_pre[l]), row(norm_ffn_post[l]), wh, ws, wa, wo, wg, wu, wd, nb, sl, is_ctx)

    y_prompt = streams["ctx"]["x"].reshape(nb_c, sl_c, D_MODEL)
    y_sample = streams["lat"]["x"].reshape(nb_l, sl_l, D_MODEL)
    return (y_prompt, y_sample, jnp.stack(ks, axis=1), jnp.stack(vs, axis=1), jnp.stack(ss, axis=1))
```

```python
import functools
import math

import jax
import jax.numpy as jnp
from jax import lax
from jax.experimental import pallas as pl
from jax.experimental.pallas import tpu as pltpu

F32 = jnp.float32
BF16 = jnp.bfloat16

D_MODEL = 1024
DEPTH = 2
GRID_W = 64
HY_W = 512
HY_BANDS = 16
HY_FEAT = 1 + 2 * HY_BANDS
HY_HIDDEN = 64
SSM_W = 512
SSM_HEAD_DIM = 64
SSM_HEADS = 8
SSM_GROUPS = 2
SSM_STATE = 64
SSM_CHUNK = 128
XBC_W = SSM_W + 2 * SSM_GROUPS * SSM_STATE
N_HEADS = 8
KV_HEADS = 2
HEAD_DIM = 64
Q_PER_KV = N_HEADS // KV_HEADS
ATT_W = N_HEADS * HEAD_DIM
KV_W = KV_HEADS * HEAD_DIM
WINDOW = 128
ATT_BLOCK = 128
ROPE_BASE = 10000.0
FFN_HIDDEN = ((8 * D_MODEL + 3 * 256 - 1) // (3 * 256)) * 256
EPS = 1e-6

LANES = 128
ROW_TILE = 256
DFT_TILE = 256
IN_SPLIT = (3 * HY_W, SSM_W, XBC_W, 2 * SSM_HEADS, ATT_W, KV_W, KV_W, 3 * D_MODEL)
IN_WIDTHS = (3 * HY_W, SSM_W, XBC_W, LANES, ATT_W, KV_W, KV_W, 3 * D_MODEL)
VMEM_LIMIT = 56 * 1024 * 1024


def _params(n_grid):
    return pltpu.CompilerParams(dimension_semantics=("arbitrary",) * n_grid, vmem_limit_bytes=VMEM_LIMIT)


def _resident(shape, layer=None):
    nd = len(shape)
    if layer is None:
        return pl.BlockSpec(shape, lambda *_: (0,) * nd, pipeline_mode=pl.Buffered(1))
    return pl.BlockSpec((None, *shape), lambda *_: (layer,) + (0,) * nd, pipeline_mode=pl.Buffered(1))


def _bdot(a, b):
    return jnp.dot(a.astype(BF16), b.astype(BF16), preferred_element_type=F32)


def _dot_nt(a, b):
    return lax.dot_general(a.astype(BF16), b.astype(BF16), (((1,), (1,)), ((), ())), preferred_element_type=F32)


def _dot_tn(a, b):
    return lax.dot_general(a.astype(BF16), b.astype(BF16), (((0,), (0,)), ((), ())), preferred_element_type=F32)


def _split2(x):
    hi = x.astype(BF16)
    lo = (x - hi.astype(F32)).astype(BF16)
    return hi, lo


def _split3(x):
    h1 = x.astype(BF16)
    r1 = x - h1.astype(F32)
    h2 = r1.astype(BF16)
    h3 = (r1 - h2.astype(F32)).astype(BF16)
    return h1, h2, h3


def _dot3(a, b):
    ah, al = _split2(a)
    bh, bl = _split2(b)
    return (jnp.dot(ah, bh, preferred_element_type=F32) + jnp.dot(ah, bl, preferred_element_type=F32)
            + jnp.dot(al, bh, preferred_element_type=F32))


def _rms(x, g):
    return x * lax.rsqrt(jnp.mean(x * x, axis=-1, keepdims=True) + EPS) * g


def _silu(x):
    return x * jax.nn.sigmoid(x)


def _dwconv(x, w, b):
    n = x.shape[0]
    row = lax.broadcasted_iota(jnp.int32, (n, 1), 0)
    prev = jnp.where(row == 0, 0.0, pltpu.roll(x, 1, 0))
    nxt = jnp.where(row == n - 1, 0.0, pltpu.roll(x, n - 1, 0))
    return prev * w[0:1, :] + x * w[1:2, :] + nxt * w[2:3, :] + b


def _ada_kernel(cond_ref, w_ref, b_ref, o_ref):
    o_ref[0] = _bdot(_silu(cond_ref[...]), w_ref[0]) + b_ref[0]


def _ada_call(cond, ada_w, ada_b):
    n = 6 * D_MODEL
    tn = n // 4
    rows = cond.shape[0]
    return pl.pallas_call(
        _ada_kernel,
        grid=(DEPTH, n // tn),
        in_specs=[
            pl.BlockSpec((rows, D_MODEL), lambda l, j: (0, 0)),
            pl.BlockSpec((1, D_MODEL, tn), lambda l, j: (l, 0, j)),
            pl.BlockSpec((1, 1, tn), lambda l, j: (l, 0, j)),
        ],
        out_specs=pl.BlockSpec((1, rows, tn), lambda l, j: (l, 0, j)),
        out_shape=jax.ShapeDtypeStruct((DEPTH, rows, n), F32),
        compiler_params=_params(2),
        name="adaln",
    )(cond, ada_w, ada_b.reshape(DEPTH, 1, n))


def _in_kernel(x_ref, mod_ref, g_ref, w_ref, *o_refs):
    h = _rms(x_ref[...], g_ref[...]) * (1.0 + mod_ref[0, 1:2, :]) + mod_ref[0, 0:1, :]
    hb = h.astype(BF16)
    off = 0
    for o_ref, wd in zip(o_refs, IN_WIDTHS):
        o_ref[...] = jnp.dot(hb, w_ref[:, off:off + wd], preferred_element_type=F32)
        off += wd


def _mod_spec(shared):
    if shared:
        return pl.BlockSpec((1, 6, D_MODEL), lambda b, i: (0, 0, 0))
    return pl.BlockSpec((1, 6, D_MODEL), lambda b, i: (b, 0, 0))


def _in_call(x, mod, g, w, layer, nb, sl, shared_mod):
    nt = sl // ROW_TILE
    n_tok = nb * sl
    row = lambda b, i: (b * nt + i, 0)
    return pl.pallas_call(
        _in_kernel,
        grid=(nb, nt),
        in_specs=[
            pl.BlockSpec((ROW_TILE, D_MODEL), row),
            _mod_spec(shared_mod),
            _resident((1, D_MODEL)),
            _resident((D_MODEL, sum(IN_WIDTHS)), layer),
        ],
        out_specs=[pl.BlockSpec((ROW_TILE, wd), row) for wd in IN_WIDTHS],
        out_shape=[jax.ShapeDtypeStruct((n_tok, wd), F32) for wd in IN_WIDTHS],
        compiler_params=_params(2),
        name="in_proj",
    )(x, mod, g, w)


def _filter_kernel(feat_ref, w1, b1, f1, w2, b2, f2, w3, dec, cos_ref, sin_ref, kre_ref, kim_ref, *, sl_len):
    feat = feat_ref[...]
    h = jnp.sin(f1[...] * (_dot3(feat, w1[...]) + b1[...]))
    h = jnp.sin(f2[...] * (_dot3(h, w2[...]) + b2[...]))
    t = feat[:, 0:1]
    filt = _dot3(h, w3[...]) * jnp.exp(-t * dec[...])
    row = lax.broadcasted_iota(jnp.int32, (sl_len, 1), 0)
    hf = filt[:, :HY_W]
    hb = jnp.where(row == 0, 0.0, filt[:, HY_W:])
    ksum = hf + hb
    kdiff = hb - hf
    kre = _dot3(cos_ref[...], ksum)
    kim = _dot3(sin_ref[...], kdiff)
    nyq = jnp.sum(jnp.where(row % 2 == 1, -ksum, ksum), axis=0, keepdims=True)
    kim = jnp.where(row == 0, nyq, kim)
    scale = jnp.where(row == 0, 0.5 / sl_len, 1.0 / sl_len)
    kre_ref[0] = kre * scale
    kim_ref[0] = kim * scale


def _filter_call(feat, w1, b1, f1, w2, b2, f2, w3, dec, cos, sinp, sl):
    hid = HY_HIDDEN
    return pl.pallas_call(
        functools.partial(_filter_kernel, sl_len=sl),
        grid=(2,),
        in_specs=[
            _resident((sl, LANES)), _resident((LANES, hid)), _resident((1, hid)), _resident((1, hid)),
            _resident((hid, hid)), _resident((1, hid)), _resident((1, hid)),
            pl.BlockSpec((hid, 2 * HY_W), lambda o: (0, o)),
            pl.BlockSpec((1, 2 * HY_W), lambda o: (0, o)),
            _resident((sl, sl)), _resident((sl, sl)),
        ],
        out_specs=[pl.BlockSpec((1, sl, HY_W), lambda o: (o, 0, 0))] * 2,
        out_shape=[jax.ShapeDtypeStruct((2, sl, HY_W), F32)] * 2,
        compiler_params=_params(1),
        name="hyena_filter",
    )(feat, w1, b1, f1, w2, b2, f2, w3, dec, cos, sinp)


def _hyena_kernel(x_ref, cw, cb, bias, kre, kim, cos, sin, sin_t, o_ref, hc, u, pre_s, pim_s, *, sl_len):
    tile = min(DFT_TILE, sl_len)
    nt = sl_len // tile
    hc[...] = _dwconv(x_ref[...], cw[...], cb[...])
    u[...] = hc[:, 0:HY_W].astype(BF16)

    def conv(order, epilogue):
        def fwd(i, carry):
            r = pl.multiple_of(i * tile, tile)
            rows = pl.ds(r, tile)
            a = jnp.dot(cos[rows, :], u[...], preferred_element_type=F32)
            b = jnp.dot(sin[rows, :], u[...], preferred_element_type=F32)
            kr = kre[order, rows, :]
            ki = kim[order, rows, :]
            first = (r + lax.broadcasted_iota(jnp.int32, (tile, 1), 0)) == 0
            pre = jnp.where(first, a * kr, a * kr + b * ki)
            pim = jnp.where(first, b * ki, b * kr - a * ki)
            pre_s[rows, :] = pre.astype(BF16)
            pim_s[rows, :] = pim.astype(BF16)
            return carry

        lax.fori_loop(0, nt, fwd, 0)

        def inv(i, carry):
            r = pl.multiple_of(i * tile, tile)
            rows = pl.ds(r, tile)
            y = (jnp.dot(cos[rows, :], pre_s[...], preferred_element_type=F32)
                 + jnp.dot(sin_t[rows, :], pim_s[...], preferred_element_type=F32))
            epilogue(rows, y)
            return carry

        lax.fori_loop(0, nt, inv, 0)

    def after_first(rows, y):
        z = hc[rows, HY_W:2 * HY_W] * (y + hc[rows, 0:HY_W] * bias[0:1, :])
        hc[rows, 0:HY_W] = z
        u[rows, :] = z.astype(BF16)

    def after_second(rows, y):
        o_ref[rows, :] = hc[rows, 2 * HY_W:3 * HY_W] * (y + hc[rows, 0:HY_W] * bias[1:2, :])

    conv(0, after_first)
    conv(1, after_second)


def _hyena_call(hy_in, cw, cb, bias, kre, kim, mats, nb, sl):
    seq = lambda b: (b, 0)
    mat = _resident((sl, sl))
    return pl.pallas_call(
        functools.partial(_hyena_kernel, sl_len=sl),
        grid=(nb,),
        in_specs=[
            pl.BlockSpec((sl, 3 * HY_W), seq, pipeline_mode=pl.Buffered(1)),
            _resident((3, 3 * HY_W)), _resident((1, 3 * HY_W)), _resident((2, HY_W)),
            _resident((2, sl, HY_W)), _resident((2, sl, HY_W)),
            mat, mat, mat,
        ],
        out_specs=pl.BlockSpec((sl, HY_W), seq),
        out_shape=jax.ShapeDtypeStruct((nb * sl, HY_W), F32),
        scratch_shapes=[pltpu.VMEM((sl, 3 * HY_W), F32)] + [pltpu.VMEM((sl, HY_W), BF16)] * 3,
        compiler_params=_params(1),
        name="hyena_conv",
    )(hy_in, cw, cb, bias, kre, kim, *mats)


SSM_PAIRS = SSM_HEADS // 2
PAIR_W = 2 * SSM_HEAD_DIM


def _ssd_kernel(xbc_ref, dt_ref, h0_ref, cw, cb, dtb, alog, dsk, expand, o_ref, hfin_ref, xbc, dt, dta, hst, *, sl_len):
    nc = sl_len // SSM_CHUNK
    q = SSM_CHUNK
    xbc[...] = _silu(_dwconv(xbc_ref[...], cw[...], cb[...]))
    dtv = jax.nn.softplus(dt_ref[...] + dtb[...])
    dt[...] = dtv
    dta[...] = dtv * (-jnp.exp(alog[...]))
    hst[...] = h0_ref[0]
    ii = lax.broadcasted_iota(jnp.int32, (q, q), 0)
    jj = lax.broadcasted_iota(jnp.int32, (q, q), 1)
    low_lanes = lax.broadcasted_iota(jnp.int32, (1, PAIR_W), 1) < SSM_HEAD_DIM
    low_rows = lax.broadcasted_iota(jnp.int32, (2 * SSM_STATE, 1), 0) < SSM_STATE

    def chunk(c, d):
        r = pl.multiple_of(c * q, q)
        rows = pl.ds(r, q)
        xs = xbc[rows, 0:SSM_W]
        bm = xbc[rows, SSM_W:SSM_W + SSM_GROUPS * SSM_STATE]
        cm = xbc[rows, SSM_W + SSM_GROUPS * SSM_STATE:XBC_W]
        tri = (jj <= ii) if d == 0 else (jj >= ii)
        trib = tri.astype(BF16)
        a1, a2, a3 = _split3(dta[rows, :])
        acum = (jnp.dot(trib, a1, preferred_element_type=F32) + jnp.dot(trib, a2, preferred_element_type=F32)
                + jnp.dot(trib, a3, preferred_element_type=F32))
        acum_t = acum.T
        ex = expand[d]
        c1, c2, c3 = _split3(acum)
        a_wide = (jnp.dot(c1, ex, preferred_element_type=F32) + jnp.dot(c2, ex, preferred_element_type=F32)
                  + jnp.dot(c3, ex, preferred_element_type=F32))
        d1, d2 = _split2(dt[rows, :])
        dt_wide = jnp.dot(d1, ex, preferred_element_type=F32) + jnp.dot(d2, ex, preferred_element_type=F32)
        last = a_wide[q - 1:q, :] if d == 0 else a_wide[0:1, :]
        xdt = xs * dt_wide
        xdd = xdt * jnp.exp(last - a_wide)
        ea = jnp.exp(a_wide)
        cdec = jnp.exp(last)
        cb_t = [_dot_nt(jnp.where(low_lanes, cm, 0.0), bm), _dot_nt(jnp.where(low_lanes, 0.0, cm), bm)]
        ys = []
        for p in range(SSM_PAIRS):
            g = p // (SSM_PAIRS // SSM_GROUPS)
            lanes = slice(p * PAIR_W, (p + 1) * PAIR_W)
            scores = []
            for hh in range(2):
                col = d * SSM_HEADS + 2 * p + hh
                lmat = jnp.exp(jnp.where(tri, acum[:, col:col + 1] - acum_t[col:col + 1, :], -jnp.inf))
                scores.append((cb_t[g] * lmat).astype(BF16))
            xp = xdt[:, lanes]
            x_blocks = jnp.concatenate([jnp.where(low_lanes, xp, 0.0), jnp.where(low_lanes, 0.0, xp)], axis=0)
            y_diag = jnp.dot(jnp.concatenate(scores, axis=1), x_blocks.astype(BF16), preferred_element_type=F32)
            slab = d * SSM_PAIRS + p
            h_prev = hst[slab]
            y_off = _bdot(cm, h_prev) * ea[:, lanes]
            states = _dot_tn(bm, xdd[:, lanes])
            own_rows = low_rows if g == 0 else jnp.logical_not(low_rows)
            hst[slab] = h_prev * cdec[:, lanes] + jnp.where(own_rows, states, 0.0)
            ys.append(y_diag + y_off)
        y = jnp.concatenate(ys, axis=1)
        if d == 0:
            o_ref[rows, :] = y + dsk[...] * xs
        else:
            o_ref[rows, :] += y

    def fwd(c, carry):
        chunk(c, 0)
        return carry

    def bwd(c, carry):
        chunk(nc - 1 - c, 1)
        return carry

    lax.fori_loop(0, nc, fwd, 0)
    lax.fori_loop(0, nc, bwd, 0)
    hfin_ref[0] = hst[...]


def _ssd_call(s_xbc, s_dt, h0, cw, cb, dtb, alog, dsk, expand, nb, sl, shared_h0):
    seq = lambda b: (b, 0)
    st_shape = (1, 2 * SSM_PAIRS, 2 * SSM_STATE, PAIR_W)
    h0_map = (lambda b: (0, 0, 0, 0)) if shared_h0 else (lambda b: (b, 0, 0, 0))
    return pl.pallas_call(
        functools.partial(_ssd_kernel, sl_len=sl),
        grid=(nb,),
        in_specs=[
            pl.BlockSpec((sl, XBC_W), seq), pl.BlockSpec((sl, LANES), seq), pl.BlockSpec(st_shape, h0_map),
            _resident((3, XBC_W)), _resident((1, XBC_W)), _resident((1, LANES)), _resident((1, LANES)),
            _resident((1, SSM_W)), _resident((2, LANES, SSM_W)),
        ],
        out_specs=[pl.BlockSpec((sl, SSM_W), seq), pl.BlockSpec(st_shape, lambda b: (b, 0, 0, 0))],
        out_shape=[jax.ShapeDtypeStruct((nb * sl, SSM_W), F32), jax.ShapeDtypeStruct((nb, *st_shape[1:]), F32)],
        scratch_shapes=[pltpu.VMEM((sl, XBC_W), F32), pltpu.VMEM((sl, LANES), F32), pltpu.VMEM((sl, LANES), F32),
                        pltpu.VMEM(st_shape[1:], F32)],
        compiler_params=_params(1),
        name="ssd_scan",
    )(s_xbc, s_dt, h0, cw, cb, dtb, alog, dsk, expand)


def _pack_states(h):
    nb = h.shape[0]
    t = h.reshape(nb, 2, SSM_PAIRS, 2, SSM_HEAD_DIM, SSM_STATE).transpose(0, 1, 2, 5, 3, 4)
    t = t.reshape(nb, 2, SSM_PAIRS, SSM_STATE, PAIR_W)
    z = jnp.zeros_like(t)
    first = (jnp.arange(SSM_PAIRS) < SSM_PAIRS // SSM_GROUPS)[None, None, :, None, None]
    slabs = jnp.where(first, jnp.concatenate([t, z], axis=3), jnp.concatenate([z, t], axis=3))
    return slabs.reshape(nb, 2 * SSM_PAIRS, 2 * SSM_STATE, PAIR_W)


def _unpack_states(s):
    nb = s.shape[0]
    half = SSM_PAIRS // SSM_GROUPS
    t = s.reshape(nb, 2, SSM_PAIRS, SSM_GROUPS, SSM_STATE, PAIR_W)
    t = jnp.concatenate([t[:, :, :half, 0], t[:, :, half:, 1]], axis=2)
    t = t.reshape(nb, 2, SSM_PAIRS, SSM_STATE, 2, SSM_HEAD_DIM).transpose(0, 1, 2, 4, 5, 3)
    return t.reshape(nb, 2, SSM_HEADS, SSM_HEAD_DIM, SSM_STATE)


def _expand_table():
    r = jnp.arange(LANES)[:, None]
    head = jnp.arange(SSM_W)[None, :] // SSM_HEAD_DIM
    return jnp.stack([r == head, r == SSM_HEADS + head]).astype(BF16)


def _head_rhs(kt_g):
    z = jnp.zeros_like(kt_g)
    return jnp.concatenate([kt_g, z], axis=0), jnp.concatenate([z, kt_g], axis=0)


def _dup_halves(v):
    low = lax.broadcasted_iota(jnp.int32, (1, KV_W), 1) < HEAD_DIM
    swapped = pltpu.roll(v, HEAD_DIM, 1)
    return jnp.where(low, v, swapped), jnp.where(low, swapped, v)


def _attend_pair(qp, rhs_pair, vals, mask, sinks):
    outs = []
    for hh in range(2):
        s = jnp.dot(qp, rhs_pair[hh], preferred_element_type=F32)
        if mask is not None:
            s = jnp.where(mask, s, -jnp.inf)
        m = jnp.maximum(jnp.max(s, axis=-1, keepdims=True), sinks[hh])
        e = jnp.exp(s - m)
        denom = jnp.sum(e, axis=-1, keepdims=True) + jnp.exp(sinks[hh] - m)
        outs.append(jnp.dot(e.astype(BF16), vals, preferred_element_type=F32) / denom)
    low = lax.broadcasted_iota(jnp.int32, (1, KV_W), 1) < HEAD_DIM
    return jnp.where(low, outs[0], outs[1])


def _ctx_attn_kernel(q_ref, k_ref, v_ref, sink_ref, o_ref, *, sl_len):
    scale = HEAD_DIM ** -0.5
    kt = k_ref[...].T.astype(BF16)
    vdup = [v.astype(BF16) for v in _dup_halves(v_ref[...])]
    for g in range(KV_HEADS):
        rhs = _head_rhs(kt[g * HEAD_DIM:(g + 1) * HEAD_DIM, :])
        for pp in range(Q_PER_KV // 2):
            pair = g * (Q_PER_KV // 2) + pp
            lanes = slice(pair * KV_W, (pair + 1) * KV_W)
            qp = (q_ref[:, lanes] * scale).astype(BF16)
            sinks = [sink_ref[2 * pair + hh:2 * pair + hh + 1, 0:1] for hh in range(2)]
            o_ref[:, lanes] = _attend_pair(qp, rhs, vdup[g], None, sinks)


def _ctx_attn_call(q, k, v, sink, nb, sl):
    seq = lambda b: (b, 0)
    return pl.pallas_call(
        functools.partial(_ctx_attn_kernel, sl_len=sl),
        grid=(nb,),
        in_specs=[pl.BlockSpec((sl, ATT_W), seq), pl.BlockSpec((sl, KV_W), seq), pl.BlockSpec((sl, KV_W), seq),
                  _resident((N_HEADS, LANES))],
        out_specs=pl.BlockSpec((sl, ATT_W), seq),
        out_shape=jax.ShapeDtypeStruct((nb * sl, ATT_W), F32),
        compiler_params=_params(1),
        name="ctx_attention",
    )(q, k, v, sink)


def _rope(x, cos, sin_signed):
    w = x.shape[1]
    lane = lax.broadcasted_iota(jnp.int32, (1, w), 1)
    partner = jnp.where(lane % 32 < 16, pltpu.roll(x, w - 16, 1), pltpu.roll(x, 16, 1))
    return x * cos + partner * sin_signed


def _lat_attn_kernel(q_ref, k_ref, v_ref, ck_ref, cv_ref, cos_ref, sin_ref, sink_ref, o_ref, qs, kts, vds, *,
                     sl_len, past):
    blk = ATT_BLOCK
    nblk = sl_len // blk
    scale = HEAD_DIM ** -0.5
    qs[...] = (_rope(q_ref[...], cos_ref[...], sin_ref[...]) * scale).astype(BF16)
    kts[:, 0:past] = ck_ref[0].T.astype(BF16)
    kts[:, past:] = _rope(k_ref[...], cos_ref[:, 0:KV_W], sin_ref[:, 0:KV_W]).T.astype(BF16)
    for g, dup in enumerate(_dup_halves(jnp.concatenate([cv_ref[0], v_ref[...]], axis=0))):
        vds[g] = dup.astype(BF16)
    qi = lax.broadcasted_iota(jnp.int32, (blk, 1), 0)
    for i in range(nblk):
        rows = slice(i * blk, (i + 1) * blk)
        lo, hi = max(i - 1, 0), min(i + 2, nblk)
        band = slice(past + lo * blk, past + hi * blk)
        n_keys = past + (hi - lo) * blk
        cc = lax.broadcasted_iota(jnp.int32, (1, n_keys), 1)
        rel = cc - past + (lo - i) * blk
        mask = (cc < past) | (jnp.abs(qi - rel) <= WINDOW)
        for g in range(KV_HEADS):
            krows = slice(g * HEAD_DIM, (g + 1) * HEAD_DIM)
            rhs = _head_rhs(jnp.concatenate([kts[krows, 0:past], kts[krows, band]], axis=1))
            vals = jnp.concatenate([vds[g, 0:past, :], vds[g, band, :]], axis=0)
            for pp in range(Q_PER_KV // 2):
                pair = g * (Q_PER_KV // 2) + pp
                lanes = slice(pair * KV_W, (pair + 1) * KV_W)
                sinks = [sink_ref[2 * pair + hh:2 * pair + hh + 1, 0:1] for hh in range(2)]
                o_ref[rows, lanes] = _attend_pair(qs[rows, lanes], rhs, vals, mask, sinks)


def _lat_attn_call(q, k, v, ck, cv, cos, sin, sink, nb, sl):
    seq = lambda b: (b, 0)
    past = ck.shape[1]
    cache = pl.BlockSpec((1, past, KV_W), lambda b: (b, 0, 0))
    return pl.pallas_call(
        functools.partial(_lat_attn_kernel, sl_len=sl, past=past),
        grid=(nb,),
        in_specs=[pl.BlockSpec((sl, ATT_W), seq), pl.BlockSpec((sl, KV_W), seq), pl.BlockSpec((sl, KV_W), seq),
                  cache, cache, _resident((sl, ATT_W)), _resident((sl, ATT_W)), _resident((N_HEADS, LANES))],
        out_specs=pl.BlockSpec((sl, ATT_W), seq),
        out_shape=jax.ShapeDtypeStruct((nb * sl, ATT_W), F32),
        scratch_shapes=[pltpu.VMEM((sl, ATT_W), BF16), pltpu.VMEM((KV_W, past + sl), BF16),
                        pltpu.VMEM((KV_HEADS, past + sl, KV_W), BF16)],
        compiler_params=_params(1),
        name="latent_attention",
    )(q, k, v, ck, cv, cos, sin, sink)


def _post_kernel(x_ref, hy_ref, sy_ref, sz_ref, at_ref, g_ref, mod_ref, snorm, npost, nfpre, nfpost,
                 wh, ws, wa, wo, wg, wu, wd, o_ref):
    m = mod_ref[0]
    ssm_y = _rms(sy_ref[...] * _silu(sz_ref[...]), snorm[...])
    merged = (jax.nn.sigmoid(g_ref[:, 0:D_MODEL]) * _bdot(hy_ref[...], wh[...])
              + jax.nn.sigmoid(g_ref[:, D_MODEL:2 * D_MODEL]) * _bdot(ssm_y, ws[...])
              + jax.nn.sigmoid(g_ref[:, 2 * D_MODEL:3 * D_MODEL]) * _bdot(at_ref[...], wa[...]))
    x1 = x_ref[...] + m[2:3, :] * _rms(_bdot(merged, wo[...]), npost[...])
    hb = (_rms(x1, nfpre[...]) * (1.0 + m[4:5, :]) + m[3:4, :]).astype(BF16)
    gate = jnp.dot(hb, wg[...], preferred_element_type=F32)
    up = jnp.dot(hb, wu[...], preferred_element_type=F32)
    f = _bdot(_silu(gate) * up, wd[...])
    o_ref[...] = x1 + m[5:6, :] * _rms(f, nfpost[...])


def _post_call(x, hy, sy, sz, at, g, mod, snorm, npost, nfpre, nfpost, wh, ws, wa, wo, wg, wu, wd, layer, nb, sl,
               shared_mod):
    nt = sl // ROW_TILE
    row = lambda b, i: (b * nt + i, 0)
    tok = lambda wdt: pl.BlockSpec((ROW_TILE, wdt), row)
    vec = lambda wdt: _resident((1, wdt))
    return pl.pallas_call(
        _post_kernel,
        grid=(nb, nt),
        in_specs=[tok(D_MODEL), tok(HY_W), tok(SSM_W), tok(SSM_W), tok(ATT_W), tok(3 * D_MODEL), _mod_spec(shared_mod),
                  vec(SSM_W), vec(D_MODEL), vec(D_MODEL), vec(D_MODEL),
                  _resident((HY_W, D_MODEL), layer), _resident((SSM_W, D_MODEL), layer),
                  _resident((ATT_W, D_MODEL), layer), _resident((D_MODEL, D_MODEL), layer),
                  _resident((D_MODEL, FFN_HIDDEN), layer), _resident((D_MODEL, FFN_HIDDEN), layer),
                  _resident((FFN_HIDDEN, D_MODEL), layer)],
        out_specs=tok(D_MODEL),
        out_shape=jax.ShapeDtypeStruct((nb * sl, D_MODEL), F32),
        compiler_params=_params(2),
        name="merge_ffn",
    )(x, hy, sy, sz, at, g, mod, snorm, npost, nfpre, nfpost, wh, ws, wa, wo, wg, wu, wd)


def _dft_mats(sl):
    f = jnp.arange(sl, dtype=jnp.int32)
    ang = ((f[:, None] * f[None, :]) % (2 * sl)).astype(F32) * (math.pi / sl)
    cos = jnp.cos(ang)
    sign = jnp.where(f % 2 == 1, -1.0, 1.0).astype(F32)
    sinp = jnp.where(f[:, None] == 0, sign[None, :], jnp.sin(ang))
    sinp_b = sinp.astype(BF16)
    return cos, sinp, (cos.astype(BF16), sinp_b, sinp_b.T)


def _filter_features(sl):
    t = jnp.arange(sl, dtype=F32) / sl
    ang = 2.0 * jnp.pi * t[:, None] * jnp.arange(1, HY_BANDS + 1, dtype=F32)[None]
    feat = jnp.concatenate([t[:, None], jnp.cos(ang), jnp.sin(ang)], axis=-1)
    return jnp.pad(feat, ((0, 0), (0, LANES - HY_FEAT)))


def _rope_tables(sl):
    rows = sl // GRID_W
    row = jnp.repeat(jnp.arange(rows), GRID_W).astype(F32)
    col = jnp.tile(jnp.arange(GRID_W), rows).astype(F32)
    nq = HEAD_DIM // 4
    inv = ROPE_BASE ** (-jnp.arange(nq, dtype=F32) / nq)
    ar = row[:, None] * inv
    ac = col[:, None] * inv
    cos = jnp.concatenate([jnp.cos(ar), jnp.cos(ar), jnp.cos(ac), jnp.cos(ac)], axis=-1)
    sin = jnp.concatenate([-jnp.sin(ar), jnp.sin(ar), -jnp.sin(ac), jnp.sin(ac)], axis=-1)
    return jnp.tile(cos, (1, N_HEADS)), jnp.tile(sin, (1, N_HEADS))


def _pack_w_in(w):
    pieces = []
    off = 0
    for wd, padded in zip(IN_SPLIT, IN_WIDTHS):
        pieces.append(jnp.pad(w[:, :, off:off + wd].astype(BF16), ((0, 0), (0, 0), (0, padded - wd))))
        off += wd
    return jnp.concatenate(pieces, axis=2)


def _pad_lanes(v):
    return jnp.pad(v.reshape(1, -1), ((0, 0), (0, LANES - v.size)))


def kernel(x_prompt, x_sample, c, cache_k, cache_v, state_ssm, c_ctx, ada_w, ada_b,
           norm_mix_pre, norm_mix_post, norm_ffn_pre, norm_ffn_post, w_in,
           hy_conv_w, hy_conv_b, hy_w1, hy_b1, hy_freq1, hy_w2, hy_b2, hy_freq2, hy_w3,
           hy_decay, hy_bias, ssm_conv_w, ssm_conv_b, ssm_a_log, ssm_dt_bias, ssm_d, ssm_norm,
           attn_sink, hy_proj, ssm_proj, attn_proj, w_out, ffn_w_gate, ffn_w_up, ffn_w_down):
    nb_c, sl_c, _ = x_prompt.shape
    nb_l, sl_l, _ = x_sample.shape
    past = cache_k.shape[2]

    cond = jnp.concatenate([c_ctx[None], c, jnp.zeros((8 - 1 - nb_l, D_MODEL), F32)], axis=0)
    mod = _ada_call(cond, ada_w, ada_b).reshape(DEPTH, 8, 6, D_MODEL)

    streams = {}
    for name, x in (("ctx", x_prompt), ("lat", x_sample)):
        nb, sl, _ = x.shape
        cos, sinp, mats = _dft_mats(sl)
        streams[name] = dict(nb=nb, sl=sl, x=x.reshape(nb * sl, D_MODEL), cos=cos, sinp=sinp, mats=mats,
                             feat=_filter_features(sl))
    rope_cos, rope_sin = _rope_tables(sl_l)
    zero_state = jnp.zeros((1, 2 * SSM_PAIRS, 2 * SSM_STATE, PAIR_W), F32)
    expand = _expand_table()
    row = lambda v: v.reshape(1, -1)

    w_in_p = _pack_w_in(w_in)
    wh, ws, wa, wo = (w.astype(BF16) for w in (hy_proj, ssm_proj, attn_proj, w_out))
    wg, wu, wd = (w.astype(BF16) for w in (ffn_w_gate, ffn_w_up, ffn_w_down))

    ks, vs, ss = [], [], []
    for l in range(DEPTH):
        w1p = jnp.pad(hy_w1[l], ((0, LANES - HY_FEAT), (0, 0)))
        dtb = _pad_lanes(ssm_dt_bias[l])
        alog = _pad_lanes(ssm_a_log[l])
        dsk = jnp.repeat(ssm_d[l, 0] + ssm_d[l, 1], SSM_HEAD_DIM).reshape(1, SSM_W)
        sink = jnp.broadcast_to(attn_sink[l][:, None], (N_HEADS, LANES))
        for name in ("ctx", "lat"):
            st = streams[name]
            nb, sl, is_ctx = st["nb"], st["sl"], name == "ctx"
            m = mod[l, 0:1] if is_ctx else mod[l, 1:1 + nb]
            kre, kim = _filter_call(st["feat"], w1p, row(hy_b1[l]), row(hy_freq1[l]), hy_w2[l], row(hy_b2[l]),
                                    row(hy_freq2[l]), hy_w3[l], row(hy_decay[l]), st["cos"], st["sinp"], sl)
            hy_in, s_z, s_xbc, s_dt, a_q, a_k, a_v, gate = _in_call(
                st["x"], m, row(norm_mix_pre[l]), w_in_p, l, nb, sl, is_ctx)
            hy_y = _hyena_call(hy_in, hy_conv_w[l], row(hy_conv_b[l]), hy_bias[l], kre, kim, st["mats"], nb, sl)
            h0 = zero_state if is_ctx else _pack_states(state_ssm[:, l])
            ssm_y, h_fin = _ssd_call(s_xbc, s_dt, h0, ssm_conv_w[l], row(ssm_conv_b[l]), dtb, alog, dsk, expand,
                                     nb, sl, is_ctx)
            if is_ctx:
                att = _ctx_attn_call(a_q, a_k, a_v, sink, nb, sl)
                ks.append(a_k.reshape(nb, sl, KV_HEADS, HEAD_DIM))
                vs.append(a_v.reshape(nb, sl, KV_HEADS, HEAD_DIM))
                ss.append(_unpack_states(h_fin))
            else:
                att = _lat_attn_call(a_q, a_k, a_v, cache_k[:, l].reshape(nb, past, KV_W),
                                     cache_v[:, l].reshape(nb, past, KV_W), rope_cos, rope_sin, sink, nb, sl)
            st["x"] = _post_call(st["x"], hy_y, ssm_y, s_z, att, gate, m, row(ssm_norm[l]), row(norm_mix_post[l]),
                                 row(norm_ffn_pre[l]), row(norm_ffn_post[l]), wh, ws, wa, wo, wg, wu, wd, l, nb, sl,
                                 is_ctx)

    y_prompt = streams["ctx"]["x"].reshape(nb_c, sl_c, D_MODEL)
    y_sample = streams["lat"]["x"].reshape(nb_l, sl_l, D_MODEL)
    return (y_prompt, y_sample, jnp.stack(ks, axis=1), jnp.stack(vs, axis=1), jnp.stack(ss, axis=1))
```

```python
import functools
import math

import jax
import jax.numpy as jnp
from jax import lax
from jax.experimental import pallas as pl
from jax.experimental.pallas import tpu as pltpu

F32 = jnp.float32
BF16 = jnp.bfloat16

D_MODEL = 1024
DEPTH = 2
GRID_W = 64
HY_W = 512
HY_BANDS = 16
HY_FEAT = 1 + 2 * HY_BANDS
HY_HIDDEN = 64
SSM_W = 512
SSM_HEAD_DIM = 64
SSM_HEADS = 8
SSM_GROUPS = 2
SSM_STATE = 64
SSM_CHUNK = 128
XBC_W = SSM_W + 2 * SSM_GROUPS * SSM_STATE
N_HEADS = 8
KV_HEADS = 2
HEAD_DIM = 64
Q_PER_KV = N_HEADS // KV_HEADS
ATT_W = N_HEADS * HEAD_DIM
KV_W = KV_HEADS * HEAD_DIM
WINDOW = 128
ATT_BLOCK = 128
ROPE_BASE = 10000.0
FFN_HIDDEN = ((8 * D_MODEL + 3 * 256 - 1) // (3 * 256)) * 256
EPS = 1e-6

LANES = 128
ROW_TILE = 512
DFT_TILE = 256
MXU_TILE = 256
FFN_SPLIT = (FFN_HIDDEN // MXU_TILE + 1) // 2 * MXU_TILE
FFN_CHUNKS = ((0, FFN_SPLIT), (FFN_SPLIT, FFN_HIDDEN))
IN_SPLIT = (3 * HY_W, SSM_W, XBC_W, 2 * SSM_HEADS, ATT_W, KV_W, KV_W, 3 * D_MODEL)
IN_WIDTHS = (3 * HY_W, SSM_W, XBC_W, LANES, ATT_W, KV_W, KV_W, 3 * D_MODEL)
IN_ALIGNED = sum(IN_SPLIT[:3])
IN_TAIL = sum(IN_WIDTHS) - IN_ALIGNED
VMEM_LIMIT = 56 * 1024 * 1024


def _params(n_grid):
    return pltpu.CompilerParams(dimension_semantics=("arbitrary",) * n_grid, vmem_limit_bytes=VMEM_LIMIT)


def _resident(shape, layer=None):
    nd = len(shape)
    if layer is None:
        return pl.BlockSpec(shape, lambda *_: (0,) * nd, pipeline_mode=pl.Buffered(1))
    return pl.BlockSpec((None, *shape), lambda *_: (layer,) + (0,) * nd, pipeline_mode=pl.Buffered(1))


def _bdot(a, b):
    return jnp.dot(a.astype(BF16), b.astype(BF16), preferred_element_type=F32)


def _dot_nt(a, b):
    return lax.dot_general(a.astype(BF16), b.astype(BF16), (((1,), (1,)), ((), ())), preferred_element_type=F32)


def _dot_tn(a, b):
    return lax.dot_general(a.astype(BF16), b.astype(BF16), (((0,), (0,)), ((), ())), preferred_element_type=F32)


def _split2(x):
    hi = x.astype(BF16)
    lo = (x - hi.astype(F32)).astype(BF16)
    return hi, lo


def _split3(x):
    h1 = x.astype(BF16)
    r1 = x - h1.astype(F32)
    h2 = r1.astype(BF16)
    h3 = (r1 - h2.astype(F32)).astype(BF16)
    return h1, h2, h3


def _dot3(a, b):
    ah, al = _split2(a)
    bh, bl = _split2(b)
    return (jnp.dot(ah, bh, preferred_element_type=F32) + jnp.dot(ah, bl, preferred_element_type=F32)
            + jnp.dot(al, bh, preferred_element_type=F32))


def _rms(x, g):
    return x * lax.rsqrt(jnp.mean(x * x, axis=-1, keepdims=True) + EPS) * g


def _silu(x):
    return x * jax.nn.sigmoid(x)


def _dwconv(x, w, b):
    n = x.shape[0]
    row = lax.broadcasted_iota(jnp.int32, (n, 1), 0)
    prev = jnp.where(row == 0, 0.0, pltpu.roll(x, 1, 0))
    nxt = jnp.where(row == n - 1, 0.0, pltpu.roll(x, n - 1, 0))
    return prev * w[0:1, :] + x * w[1:2, :] + nxt * w[2:3, :] + b


def _ada_kernel(cond_ref, w_ref, b_ref, o_ref):
    o_ref[0] = _bdot(_silu(cond_ref[...]), w_ref[0]) + b_ref[0]


def _ada_call(cond, ada_w, ada_b):
    n = 6 * D_MODEL
    tn = n // 4
    rows = cond.shape[0]
    return pl.pallas_call(
        _ada_kernel,
        grid=(DEPTH, n // tn),
        in_specs=[
            pl.BlockSpec((rows, D_MODEL), lambda l, j: (0, 0)),
            pl.BlockSpec((1, D_MODEL, tn), lambda l, j: (l, 0, j)),
            pl.BlockSpec((1, 1, tn), lambda l, j: (l, 0, j)),
        ],
        out_specs=pl.BlockSpec((1, rows, tn), lambda l, j: (l, 0, j)),
        out_shape=jax.ShapeDtypeStruct((DEPTH, rows, n), F32),
        compiler_params=_params(2),
        name="adaln",
    )(cond, ada_w, ada_b.reshape(DEPTH, 1, n))


def _in_kernel(x_ref, mod_ref, g_ref, w_ref, *refs):
    o_refs, tail = refs[:-1], refs[-1]

    @pl.when(pl.program_id(0) == 0)
    def _():
        rest = w_ref[:, IN_ALIGNED:]
        lane = lax.broadcasted_iota(jnp.int32, (1, LANES), 1)
        tail[:, 0:LANES] = jnp.where(lane < IN_SPLIT[3], rest[:, 0:LANES], 0)
        shifted = pltpu.roll(rest, IN_TAIL - IN_SPLIT[3], 1)
        tail[:, LANES:] = shifted[:, 0:IN_TAIL - LANES]

    h = _rms(x_ref[...], g_ref[...]) * (1.0 + mod_ref[0, 1:2, :]) + mod_ref[0, 0:1, :]
    hb = h.astype(BF16)
    off = 0
    for o_ref, wd in zip(o_refs, IN_WIDTHS):
        w = w_ref[:, off:off + wd] if off < IN_ALIGNED else tail[:, off - IN_ALIGNED:off - IN_ALIGNED + wd]
        o_ref[...] = jnp.dot(hb, w, preferred_element_type=F32)
        off += wd


def _mod_spec(shared, sl):
    if shared:
        return pl.BlockSpec((1, 6, D_MODEL), lambda i: (0, 0, 0))
    assert sl % ROW_TILE == 0
    return pl.BlockSpec((1, 6, D_MODEL), lambda i: (i // (sl // ROW_TILE), 0, 0))


def _in_call(x, mod, g, w, layer, nb, sl, shared_mod):
    n_tok = nb * sl
    assert n_tok % ROW_TILE == 0
    row = lambda i: (i, 0)
    return pl.pallas_call(
        _in_kernel,
        grid=(n_tok // ROW_TILE,),
        in_specs=[
            pl.BlockSpec((ROW_TILE, D_MODEL), row),
            _mod_spec(shared_mod, sl),
            _resident((1, D_MODEL)),
            _resident((D_MODEL, sum(IN_WIDTHS)), layer),
        ],
        out_specs=[pl.BlockSpec((ROW_TILE, wd), row) for wd in IN_WIDTHS],
        out_shape=[jax.ShapeDtypeStruct((n_tok, wd), F32) for wd in IN_WIDTHS],
        scratch_shapes=[pltpu.VMEM((D_MODEL, IN_TAIL), BF16)],
        compiler_params=_params(1),
        name="in_proj",
    )(x, mod, g, w)


def _filter_kernel(feat_ref, w1, b1, f1, w2, b2, f2, w3, dec, cos_ref, sin_ref, kre_ref, kim_ref, *, sl_len):
    feat = feat_ref[...]
    h = jnp.sin(f1[...] * (_dot3(feat, w1[...]) + b1[...]))
    h = jnp.sin(f2[...] * (_dot3(h, w2[...]) + b2[...]))
    t = feat[:, 0:1]
    filt = _dot3(h, w3[...]) * jnp.exp(-t * dec[...])
    row = lax.broadcasted_iota(jnp.int32, (sl_len, 1), 0)
    hf = filt[:, :HY_W]
    hb = jnp.where(row == 0, 0.0, filt[:, HY_W:])
    ksum = hf + hb
    kdiff = hb - hf
    kre = _bdot(cos_ref[...], ksum)
    kim = _bdot(sin_ref[...], kdiff)
    nyq = jnp.sum(jnp.where(row % 2 == 1, -ksum, ksum), axis=0, keepdims=True)
    kim = jnp.where(row == 0, nyq, kim)
    scale = jnp.where(row == 0, 0.5 / sl_len, 1.0 / sl_len)
    kre_ref[0] = kre * scale
    kim_ref[0] = kim * scale


def _filter_call(feat, w1, b1, f1, w2, b2, f2, w3, dec, cos, sinp, sl):
    hid = HY_HIDDEN
    return pl.pallas_call(
        functools.partial(_filter_kernel, sl_len=sl),
        grid=(2,),
        in_specs=[
            _resident((sl, LANES)), _resident((LANES, hid)), _resident((1, hid)), _resident((1, hid)),
            _resident((hid, hid)), _resident((1, hid)), _resident((1, hid)),
            pl.BlockSpec((hid, 2 * HY_W), lambda o: (0, o)),
            pl.BlockSpec((1, 2 * HY_W), lambda o: (0, o)),
            _resident((sl, sl)), _resident((sl, sl)),
        ],
        out_specs=[pl.BlockSpec((1, sl, HY_W), lambda o: (o, 0, 0))] * 2,
        out_shape=[jax.ShapeDtypeStruct((2, sl, HY_W), F32)] * 2,
        compiler_params=_params(1),
        name="hyena_filter",
    )(feat, w1, b1, f1, w2, b2, f2, w3, dec, cos, sinp)


def _hyena_kernel(x_ref, cw, cb, bias, kre, kim, cos, sin, sin_t, o_ref, hc, u, pre_s, pim_s, *, sl_len):
    tile = min(DFT_TILE, sl_len)
    nt = sl_len // tile
    hc[...] = _dwconv(x_ref[...], cw[...], cb[...])
    u[...] = hc[:, 0:HY_W].astype(BF16)

    def conv(order, epilogue):
        def fwd(i, carry):
            r = pl.multiple_of(i * tile, tile)
            rows = pl.ds(r, tile)
            a = jnp.dot(cos[rows, :], u[...], preferred_element_type=F32)
            b = jnp.dot(sin[rows, :], u[...], preferred_element_type=F32)
            kr = kre[order, rows, :]
            ki = kim[order, rows, :]
            first = (r + lax.broadcasted_iota(jnp.int32, (tile, 1), 0)) == 0
            pre = jnp.where(first, a * kr, a * kr + b * ki)
            pim = jnp.where(first, b * ki, b * kr - a * ki)
            pre_s[rows, :] = pre.astype(BF16)
            pim_s[rows, :] = pim.astype(BF16)
            return carry

        lax.fori_loop(0, nt, fwd, 0)

        def inv(i, carry):
            r = pl.multiple_of(i * tile, tile)
            rows = pl.ds(r, tile)
            y = (jnp.dot(cos[rows, :], pre_s[...], preferred_element_type=F32)
                 + jnp.dot(sin_t[rows, :], pim_s[...], preferred_element_type=F32))
            epilogue(rows, y)
            return carry

        lax.fori_loop(0, nt, inv, 0)

    def after_first(rows, y):
        z = hc[rows, HY_W:2 * HY_W] * (y + hc[rows, 0:HY_W] * bias[0:1, :])
        hc[rows, 0:HY_W] = z
        u[rows, :] = z.astype(BF16)

    def after_second(rows, y):
        o_ref[rows, :] = (hc[rows, 2 * HY_W:3 * HY_W] * (y + hc[rows, 0:HY_W] * bias[1:2, :])).astype(BF16)

    conv(0, after_first)
    conv(1, after_second)


def _hyena_call(hy_in, cw, cb, bias, kre, kim, mats, nb, sl):
    seq = lambda b: (b, 0)
    mat = _resident((sl, sl))
    return pl.pallas_call(
        functools.partial(_hyena_kernel, sl_len=sl),
        grid=(nb,),
        in_specs=[
            pl.BlockSpec((sl, 3 * HY_W), seq),
            _resident((3, 3 * HY_W)), _resident((1, 3 * HY_W)), _resident((2, HY_W)),
            _resident((2, sl, HY_W)), _resident((2, sl, HY_W)),
            mat, mat, mat,
        ],
        out_specs=pl.BlockSpec((sl, HY_W), seq),
        out_shape=jax.ShapeDtypeStruct((nb * sl, HY_W), BF16),
        scratch_shapes=[pltpu.VMEM((sl, 3 * HY_W), F32)] + [pltpu.VMEM((sl, HY_W), BF16)] * 3,
        compiler_params=_params(1),
        name="hyena_conv",
    )(hy_in, cw, cb, bias, kre, kim, *mats)


SSM_PAIRS = SSM_HEADS // 2
PAIR_W = 2 * SSM_HEAD_DIM


def _ssd_kernel(xbc_ref, dt_ref, h0_ref, cw, cb, dtb, alog, dsk, expand, o_ref, hfin_ref, xbc, dt, dta, hst, y_bwd, *,
                sl_len):
    nc = sl_len // SSM_CHUNK
    q = SSM_CHUNK
    xbc[...] = _silu(_dwconv(xbc_ref[...], cw[...], cb[...]))
    dtv = jax.nn.softplus(dt_ref[...] + dtb[...])
    dt[...] = dtv
    dta[...] = dtv * (-jnp.exp(alog[...]))
    hst[...] = h0_ref[0]
    ii = lax.broadcasted_iota(jnp.int32, (q, q), 0)
    jj = lax.broadcasted_iota(jnp.int32, (q, q), 1)
    low_lanes = lax.broadcasted_iota(jnp.int32, (1, PAIR_W), 1) < SSM_HEAD_DIM
    low_rows = lax.broadcasted_iota(jnp.int32, (2 * SSM_STATE, 1), 0) < SSM_STATE

    def chunk(c, d):
        r = pl.multiple_of(c * q, q)
        rows = pl.ds(r, q)
        xs = xbc[rows, 0:SSM_W]
        bm = xbc[rows, SSM_W:SSM_W + SSM_GROUPS * SSM_STATE]
        cm = xbc[rows, SSM_W + SSM_GROUPS * SSM_STATE:XBC_W]
        tri = (jj <= ii) if d == 0 else (jj >= ii)
        trib = tri.astype(BF16)
        a1, a2, a3 = _split3(dta[rows, :])
        acum = (jnp.dot(trib, a1, preferred_element_type=F32) + jnp.dot(trib, a2, preferred_element_type=F32)
                + jnp.dot(trib, a3, preferred_element_type=F32))
        acum_t = acum.T
        ex = expand[d]
        c1, c2, c3 = _split3(acum)
        a_wide = (jnp.dot(c1, ex, preferred_element_type=F32) + jnp.dot(c2, ex, preferred_element_type=F32)
                  + jnp.dot(c3, ex, preferred_element_type=F32))
        d1, d2 = _split2(dt[rows, :])
        dt_wide = jnp.dot(d1, ex, preferred_element_type=F32) + jnp.dot(d2, ex, preferred_element_type=F32)
        last = a_wide[q - 1:q, :] if d == 0 else a_wide[0:1, :]
        xdt = xs * dt_wide
        xdd = xdt * jnp.exp(last - a_wide)
        ea = jnp.exp(a_wide)
        cdec = jnp.exp(last)
        cb_t = [_dot_nt(jnp.where(low_lanes, cm, 0.0), bm), _dot_nt(jnp.where(low_lanes, 0.0, cm), bm)]
        ys = []
        for p in range(SSM_PAIRS):
            g = p // (SSM_PAIRS // SSM_GROUPS)
            lanes = slice(p * PAIR_W, (p + 1) * PAIR_W)
            scores = []
            for hh in range(2):
                col = d * SSM_HEADS + 2 * p + hh
                lmat = jnp.exp(jnp.where(tri, acum[:, col:col + 1] - acum_t[col:col + 1, :], -jnp.inf))
                scores.append((cb_t[g] * lmat).astype(BF16))
            xp = xdt[:, lanes]
            x_blocks = jnp.concatenate([jnp.where(low_lanes, xp, 0.0), jnp.where(low_lanes, 0.0, xp)], axis=0)
            y_diag = jnp.dot(jnp.concatenate(scores, axis=1), x_blocks.astype(BF16), preferred_element_type=F32)
            slab = d * SSM_PAIRS + p
            h_prev = hst[slab]
            y_off = _bdot(cm, h_prev) * ea[:, lanes]
            states = _dot_tn(bm, xdd[:, lanes])
            own_rows = low_rows if g == 0 else jnp.logical_not(low_rows)
            hst[slab] = h_prev * cdec[:, lanes] + jnp.where(own_rows, states, 0.0)
            ys.append(y_diag + y_off)
        y = jnp.concatenate(ys, axis=1)
        if d == 0:
            o_ref[rows, :] = y + dsk[...] * xs
        else:
            y_bwd[rows, :] = y

    def step(c, carry):
        chunk(c, 0)
        chunk(nc - 1 - c, 1)
        return carry

    lax.fori_loop(0, nc, step, 0)
    o_ref[...] += y_bwd[...]
    hfin_ref[0] = hst[...]


def _ssd_call(s_xbc, s_dt, h0, cw, cb, dtb, alog, dsk, expand, nb, sl, shared_h0):
    seq = lambda b: (b, 0)
    st_shape = (1, 2 * SSM_PAIRS, 2 * SSM_STATE, PAIR_W)
    h0_map = (lambda b: (0, 0, 0, 0)) if shared_h0 else (lambda b: (b, 0, 0, 0))
    return pl.pallas_call(
        functools.partial(_ssd_kernel, sl_len=sl),
        grid=(nb,),
        in_specs=[
            pl.BlockSpec((sl, XBC_W), seq), pl.BlockSpec((sl, LANES), seq), pl.BlockSpec(st_shape, h0_map),
            _resident((3, XBC_W)), _resident((1, XBC_W)), _resident((1, LANES)), _resident((1, LANES)),
            _resident((1, SSM_W)), _resident((2, LANES, SSM_W)),
        ],
        out_specs=[pl.BlockSpec((sl, SSM_W), seq), pl.BlockSpec(st_shape, lambda b: (b, 0, 0, 0))],
        out_shape=[jax.ShapeDtypeStruct((nb * sl, SSM_W), F32), jax.ShapeDtypeStruct((nb, *st_shape[1:]), F32)],
        scratch_shapes=[pltpu.VMEM((sl, XBC_W), F32), pltpu.VMEM((sl, LANES), F32), pltpu.VMEM((sl, LANES), F32),
                        pltpu.VMEM(st_shape[1:], F32), pltpu.VMEM((sl, SSM_W), F32)],
        compiler_params=_params(1),
        name="ssd_scan",
    )(s_xbc, s_dt, h0, cw, cb, dtb, alog, dsk, expand)


def _pack_states(h):
    nb = h.shape[0]
    t = h.reshape(nb, 2, SSM_PAIRS, 2, SSM_HEAD_DIM, SSM_STATE).transpose(0, 1, 2, 5, 3, 4)
    t = t.reshape(nb, 2, SSM_PAIRS, SSM_STATE, PAIR_W)
    z = jnp.zeros_like(t)
    first = (jnp.arange(SSM_PAIRS) < SSM_PAIRS // SSM_GROUPS)[None, None, :, None, None]
    slabs = jnp.where(first, jnp.concatenate([t, z], axis=3), jnp.concatenate([z, t], axis=3))
    return slabs.reshape(nb, 2 * SSM_PAIRS, 2 * SSM_STATE, PAIR_W)


def _unpack_states(s):
    nb = s.shape[0]
    half = SSM_PAIRS // SSM_GROUPS
    t = s.reshape(nb, 2, SSM_PAIRS, SSM_GROUPS, SSM_STATE, PAIR_W)
    t = jnp.concatenate([t[:, :, :half, 0], t[:, :, half:, 1]], axis=2)
    t = t.reshape(nb, 2, SSM_PAIRS, SSM_STATE, 2, SSM_HEAD_DIM).transpose(0, 1, 2, 4, 5, 3)
    return t.reshape(nb, 2, SSM_HEADS, SSM_HEAD_DIM, SSM_STATE)


def _expand_table():
    r = jnp.arange(LANES)[:, None]
    head = jnp.arange(SSM_W)[None, :] // SSM_HEAD_DIM
    return jnp.stack([r == head, r == SSM_HEADS + head]).astype(BF16)


def _attend_group(keys, q_t, vals_t, mask, sink_row, g):
    z = jnp.zeros_like(q_t)
    rhs = jnp.concatenate([q_t, z] if g == 0 else [z, q_t], axis=0)
    s = jnp.dot(keys, rhs, preferred_element_type=F32)
    if mask is not None:
        s = jnp.where(mask, s, -jnp.inf)
    m = jnp.maximum(jnp.max(s, axis=0, keepdims=True), sink_row)
    e = jnp.exp(s - m)
    denom = jnp.sum(e, axis=0, keepdims=True) + jnp.exp(sink_row - m)
    return jnp.dot(vals_t, e.astype(BF16), preferred_element_type=F32) / denom


def _ctx_attn_kernel(q_ref, k_ref, v_ref, sink_ref, o_ref, *, sl_len):
    scale = HEAD_DIM ** -0.5
    q_t = (q_ref[...] * scale).T.astype(BF16)
    keys = k_ref[...].astype(BF16)
    v_t = v_ref[...].T.astype(BF16)
    outs = []
    for g in range(KV_HEADS):
        q_g = jnp.concatenate(
            [q_t[(g * Q_PER_KV + hq) * HEAD_DIM:(g * Q_PER_KV + hq + 1) * HEAD_DIM, :] for hq in range(Q_PER_KV)], axis=1)
        o_t = _attend_group(keys, q_g, v_t[g * HEAD_DIM:(g + 1) * HEAD_DIM, :], None, sink_ref[g], g)
        outs += [o_t[:, hq * sl_len:(hq + 1) * sl_len] for hq in range(Q_PER_KV)]
    o_ref[...] = jnp.concatenate(outs, axis=0).T.astype(BF16)


def _ctx_attn_call(q, k, v, sink_rows, nb, sl):
    seq = lambda b: (b, 0)
    return pl.pallas_call(
        functools.partial(_ctx_attn_kernel, sl_len=sl),
        grid=(nb,),
        in_specs=[pl.BlockSpec((sl, ATT_W), seq), pl.BlockSpec((sl, KV_W), seq), pl.BlockSpec((sl, KV_W), seq),
                  _resident((KV_HEADS, 1, Q_PER_KV * sl))],
        out_specs=pl.BlockSpec((sl, ATT_W), seq),
        out_shape=jax.ShapeDtypeStruct((nb * sl, ATT_W), BF16),
        compiler_params=_params(1),
        name="ctx_attention",
    )(q, k, v, sink_rows)


def _rope(x, cos, sin_signed):
    w = x.shape[1]
    lane = lax.broadcasted_iota(jnp.int32, (1, w), 1)
    partner = jnp.where(lane % 32 < 16, pltpu.roll(x, w - 16, 1), pltpu.roll(x, 16, 1))
    return x * cos + partner * sin_signed


def _lat_attn_kernel(q_ref, k_ref, v_ref, ck_ref, cv_ref, cos_ref, sin_ref, sink_ref, o_ref, q_t, kcat, v_t, o_t, *,
                     sl_len, past):
    blk = ATT_BLOCK
    nblk = sl_len // blk
    scale = HEAD_DIM ** -0.5
    q_t[...] = (_rope(q_ref[...], cos_ref[...], sin_ref[...]) * scale).T.astype(BF16)
    kcat[0:past, :] = ck_ref[0].astype(BF16)
    kcat[past:, :] = _rope(k_ref[...], cos_ref[:, 0:KV_W], sin_ref[:, 0:KV_W]).astype(BF16)
    v_t[:, 0:past] = cv_ref[0].T.astype(BF16)
    v_t[:, past:] = v_ref[...].T.astype(BF16)
    for i in range(nblk):
        cols = slice(i * blk, (i + 1) * blk)
        lo, hi = max(i - 1, 0), min(i + 2, nblk)
        band = slice(past + lo * blk, past + hi * blk)
        n_keys = past + (hi - lo) * blk
        kk = lax.broadcasted_iota(jnp.int32, (n_keys, 1), 0)
        qi = lax.broadcasted_iota(jnp.int32, (1, Q_PER_KV * blk), 1) % blk
        rel = kk - past + (lo - i) * blk
        mask = (kk < past) | (jnp.abs(qi - rel) <= WINDOW)
        keys = jnp.concatenate([kcat[0:past, :], kcat[band, :]], axis=0)
        for g in range(KV_HEADS):
            drows = slice(g * HEAD_DIM, (g + 1) * HEAD_DIM)
            q_g = jnp.concatenate(
                [q_t[(g * Q_PER_KV + hq) * HEAD_DIM:(g * Q_PER_KV + hq + 1) * HEAD_DIM, cols] for hq in range(Q_PER_KV)],
                axis=1)
            vals_t = jnp.concatenate([v_t[drows, 0:past], v_t[drows, band]], axis=1)
            out = _attend_group(keys, q_g, vals_t, mask, sink_ref[g], g)
            for hq in range(Q_PER_KV):
                h = g * Q_PER_KV + hq
                o_t[h * HEAD_DIM:(h + 1) * HEAD_DIM, cols] = out[:, hq * blk:(hq + 1) * blk]
    o_ref[...] = o_t[...].T.astype(BF16)


def _lat_attn_call(q, k, v, ck, cv, cos, sin, sink_rows, nb, sl):
    seq = lambda b: (b, 0)
    past = ck.shape[1]
    cache = pl.BlockSpec((1, past, KV_W), lambda b: (b, 0, 0))
    return pl.pallas_call(
        functools.partial(_lat_attn_kernel, sl_len=sl, past=past),
        grid=(nb,),
        in_specs=[pl.BlockSpec((sl, ATT_W), seq), pl.BlockSpec((sl, KV_W), seq), pl.BlockSpec((sl, KV_W), seq),
                  cache, cache, _resident((sl, ATT_W)), _resident((sl, ATT_W)),
                  _resident((KV_HEADS, 1, Q_PER_KV * ATT_BLOCK))],
        out_specs=pl.BlockSpec((sl, ATT_W), seq),
        out_shape=jax.ShapeDtypeStruct((nb * sl, ATT_W), BF16),
        scratch_shapes=[pltpu.VMEM((ATT_W, sl), BF16), pltpu.VMEM((past + sl, KV_W), BF16),
                        pltpu.VMEM((KV_W, past + sl), BF16), pltpu.VMEM((ATT_W, sl), F32)],
        compiler_params=_params(1),
        name="latent_attention",
    )(q, k, v, ck, cv, cos, sin, sink_rows)


def _sink_rows(sink, n_q):
    return jnp.repeat(sink.reshape(KV_HEADS, Q_PER_KV), n_q, axis=1).reshape(KV_HEADS, 1, Q_PER_KV * n_q)


def _post_kernel(x_ref, hy_ref, sy_ref, sz_ref, at_ref, g_ref, mod_ref, snorm, npost, nfpre, nfpost,
                 wh, ws, wa, wo, wg, wu, wd, o_ref):
    m = mod_ref[0]
    ssm_y = _rms(sy_ref[...] * _silu(sz_ref[...]), snorm[...])
    merged = (jax.nn.sigmoid(g_ref[:, 0:D_MODEL]) * _bdot(hy_ref[...], wh[...])
              + jax.nn.sigmoid(g_ref[:, D_MODEL:2 * D_MODEL]) * _bdot(ssm_y, ws[...])
              + jax.nn.sigmoid(g_ref[:, 2 * D_MODEL:3 * D_MODEL]) * _bdot(at_ref[...], wa[...]))
    x1 = x_ref[...] + m[2:3, :] * _rms(_bdot(merged, wo[...]), npost[...])
    hb = (_rms(x1, nfpre[...]) * (1.0 + m[4:5, :]) + m[3:4, :]).astype(BF16)
    f = None
    for lo, hi in FFN_CHUNKS:
        gate = jnp.dot(hb, wg[:, lo:hi], preferred_element_type=F32)
        up = jnp.dot(hb, wu[:, lo:hi], preferred_element_type=F32)
        part = _bdot(_silu(gate) * up, wd[lo:hi, :])
        f = part if f is None else f + part
    o_ref[...] = x1 + m[5:6, :] * _rms(f, nfpost[...])


def _post_call(x, hy, sy, sz, at, g, mod, snorm, npost, nfpre, nfpost, wh, ws, wa, wo, wg, wu, wd, layer, nb, sl,
               shared_mod):
    n_tok = nb * sl
    assert n_tok % ROW_TILE == 0
    tok = lambda wdt: pl.BlockSpec((ROW_TILE, wdt), lambda i: (i, 0))
    vec = lambda wdt: _resident((1, wdt))
    return pl.pallas_call(
        _post_kernel,
        grid=(n_tok // ROW_TILE,),
        in_specs=[tok(D_MODEL), tok(HY_W), tok(SSM_W), tok(SSM_W), tok(ATT_W), tok(3 * D_MODEL),
                  _mod_spec(shared_mod, sl),
                  vec(SSM_W), vec(D_MODEL), vec(D_MODEL), vec(D_MODEL),
                  _resident((HY_W, D_MODEL), layer), _resident((SSM_W, D_MODEL), layer),
                  _resident((ATT_W, D_MODEL), layer), _resident((D_MODEL, D_MODEL), layer),
                  _resident((D_MODEL, FFN_HIDDEN), layer), _resident((D_MODEL, FFN_HIDDEN), layer),
                  _resident((FFN_HIDDEN, D_MODEL), layer)],
        out_specs=tok(D_MODEL),
        out_shape=jax.ShapeDtypeStruct((n_tok, D_MODEL), F32),
        compiler_params=_params(1),
        name="merge_ffn",
    )(x, hy, sy, sz, at, g, mod, snorm, npost, nfpre, nfpost, wh, ws, wa, wo, wg, wu, wd)


def _dft_mats(sl):
    radix = 32
    assert sl % radix == 0
    s = jnp.arange(sl, dtype=jnp.int32)

    def angles(f):
        return ((f[:, None] * s[None, :]) % (2 * sl)).astype(F32) * (math.pi / sl)

    a_hi = angles(jnp.arange(sl // radix, dtype=jnp.int32) * radix)[:, None, :]
    a_lo = angles(jnp.arange(radix, dtype=jnp.int32))[None, :, :]
    cos = (jnp.cos(a_hi) * jnp.cos(a_lo) - jnp.sin(a_hi) * jnp.sin(a_lo)).reshape(sl, sl)
    sin = (jnp.sin(a_hi) * jnp.cos(a_lo) + jnp.cos(a_hi) * jnp.sin(a_lo)).reshape(sl, sl)
    sign = jnp.where(s % 2 == 1, -1.0, 1.0).astype(F32)
    sinp = jnp.where(s[:, None] == 0, sign[None, :], sin).astype(BF16)
    return cos.astype(BF16), sinp, sinp.T


def _filter_features(sl):
    t = jnp.arange(sl, dtype=F32) / sl
    ang = 2.0 * jnp.pi * t[:, None] * jnp.arange(1, HY_BANDS + 1, dtype=F32)[None]
    feat = jnp.concatenate([t[:, None], jnp.cos(ang), jnp.sin(ang)], axis=-1)
    return jnp.pad(feat, ((0, 0), (0, LANES - HY_FEAT)))


def _rope_tables(sl):
    rows = sl // GRID_W
    row = jnp.repeat(jnp.arange(rows), GRID_W).astype(F32)
    col = jnp.tile(jnp.arange(GRID_W), rows).astype(F32)
    nq = HEAD_DIM // 4
    inv = ROPE_BASE ** (-jnp.arange(nq, dtype=F32) / nq)
    ar = row[:, None] * inv
    ac = col[:, None] * inv
    cos = jnp.concatenate([jnp.cos(ar), jnp.cos(ar), jnp.cos(ac), jnp.cos(ac)], axis=-1)
    sin = jnp.concatenate([-jnp.sin(ar), jnp.sin(ar), -jnp.sin(ac), jnp.sin(ac)], axis=-1)
    return jnp.tile(cos, (1, N_HEADS)), jnp.tile(sin, (1, N_HEADS))


def _pack_w_in(w):
    return jnp.pad(w.astype(BF16), ((0, 0), (0, 0), (0, sum(IN_WIDTHS) - sum(IN_SPLIT))))


def _pad_lanes(v):
    return jnp.pad(v.reshape(1, -1), ((0, 0), (0, LANES - v.size)))


def kernel(x_prompt, x_sample, c, cache_k, cache_v, state_ssm, c_ctx, ada_w, ada_b,
           norm_mix_pre, norm_mix_post, norm_ffn_pre, norm_ffn_post, w_in,
           hy_conv_w, hy_conv_b, hy_w1, hy_b1, hy_freq1, hy_w2, hy_b2, hy_freq2, hy_w3,
           hy_decay, hy_bias, ssm_conv_w, ssm_conv_b, ssm_a_log, ssm_dt_bias, ssm_d, ssm_norm,
           attn_sink, hy_proj, ssm_proj, attn_proj, w_out, ffn_w_gate, ffn_w_up, ffn_w_down):
    nb_c, sl_c, _ = x_prompt.shape
    nb_l, sl_l, _ = x_sample.shape
    past = cache_k.shape[2]

    cond = jnp.concatenate([c_ctx[None], c, jnp.zeros((8 - 1 - nb_l, D_MODEL), F32)], axis=0)
    mod = _ada_call(cond, ada_w, ada_b).reshape(DEPTH, 8, 6, D_MODEL)

    streams = {}
    for name, x in (("ctx", x_prompt), ("lat", x_sample)):
        nb, sl, _ = x.shape
        streams[name] = dict(nb=nb, sl=sl, x=x.reshape(nb * sl, D_MODEL), mats=_dft_mats(sl),
                             feat=_filter_features(sl))
    rope_cos, rope_sin = _rope_tables(sl_l)
    zero_state = jnp.zeros((1, 2 * SSM_PAIRS, 2 * SSM_STATE, PAIR_W), F32)
    expand = _expand_table()
    row = lambda v: v.reshape(1, -1)

    w_in_p = _pack_w_in(w_in)
    wh, ws, wa, wo = (w.astype(BF16) for w in (hy_proj, ssm_proj, attn_proj, w_out))
    wg, wu, wd = (w.astype(BF16) for w in (ffn_w_gate, ffn_w_up, ffn_w_down))

    ks, vs, ss = [], [], []
    for l in range(DEPTH):
        w1p = jnp.pad(hy_w1[l], ((0, LANES - HY_FEAT), (0, 0)))
        dtb = _pad_lanes(ssm_dt_bias[l])
        alog = _pad_lanes(ssm_a_log[l])
        dsk = jnp.repeat(ssm_d[l, 0] + ssm_d[l, 1], SSM_HEAD_DIM).reshape(1, SSM_W)
        sink = {"ctx": _sink_rows(attn_sink[l], sl_c), "lat": _sink_rows(attn_sink[l], ATT_BLOCK)}
        for name in ("ctx", "lat"):
            st = streams[name]
            nb, sl, is_ctx = st["nb"], st["sl"], name == "ctx"
            m = mod[l, 0:1] if is_ctx else mod[l, 1:1 + nb]
            kre, kim = _filter_call(st["feat"], w1p, row(hy_b1[l]), row(hy_freq1[l]), hy_w2[l], row(hy_b2[l]),
                                    row(hy_freq2[l]), hy_w3[l], row(hy_decay[l]), st["mats"][0], st["mats"][1], sl)
            hy_in, s_z, s_xbc, s_dt, a_q, a_k, a_v, gate = _in_call(
                st["x"], m, row(norm_mix_pre[l]), w_in_p, l, nb, sl, is_ctx)
            hy_y = _hyena_call(hy_in, hy_conv_w[l], row(hy_conv_b[l]), hy_bias[l], kre, kim, st["mats"], nb, sl)
            h0 = zero_state if is_ctx else _pack_states(state_ssm[:, l])
            ssm_y, h_fin = _ssd_call(s_xbc, s_dt, h0, ssm_conv_w[l], row(ssm_conv_b[l]), dtb, alog, dsk, expand,
                                     nb, sl, is_ctx)
            if is_ctx:
                att = _ctx_attn_call(a_q, a_k, a_v, sink[name], nb, sl)
                ks.append(a_k.reshape(nb, sl, KV_HEADS, HEAD_DIM))
                vs.append(a_v.reshape(nb, sl, KV_HEADS, HEAD_DIM))
                ss.append(_unpack_states(h_fin))
            else:
                att = _lat_attn_call(a_q, a_k, a_v, cache_k[:, l].reshape(nb, past, KV_W),
                                     cache_v[:, l].reshape(nb, past, KV_W), rope_cos, rope_sin, sink[name], nb, sl)
            st["x"] = _post_call(st["x"], hy_y, ssm_y, s_z, att, gate, m, row(ssm_norm[l]), row(norm_mix_post[l]),
                                 row(norm_ffn_pre[l]), row(norm_ffn_post[l]), wh, ws, wa, wo, wg, wu, wd, l, nb, sl,
                                 is_ctx)

    y_prompt = streams["ctx"]["x"].reshape(nb_c, sl_c, D_MODEL)
    y_sample = streams["lat"]["x"].reshape(nb_l, sl_l, D_MODEL)
    return (y_prompt, y_sample, jnp.stack(ks, axis=1), jnp.stack(vs, axis=1), jnp.stack(ss, axis=1))
```

```python
import functools
import math

import jax
import jax.numpy as jnp
from jax import lax
from jax.experimental import pallas as pl
from jax.experimental.pallas import tpu as pltpu

F32 = jnp.float32
BF16 = jnp.bfloat16

D_MODEL = 1024
DEPTH = 2
GRID_W = 64
HY_W = 512
HY_BANDS = 16
HY_FEAT = 1 + 2 * HY_BANDS
HY_HIDDEN = 64
SSM_W = 512
SSM_HEAD_DIM = 64
SSM_HEADS = 8
SSM_GROUPS = 2
SSM_STATE = 64
SSM_CHUNK = 128
XBC_W = SSM_W + 2 * SSM_GROUPS * SSM_STATE
N_HEADS = 8
KV_HEADS = 2
HEAD_DIM = 64
Q_PER_KV = N_HEADS // KV_HEADS
ATT_W = N_HEADS * HEAD_DIM
KV_W = KV_HEADS * HEAD_DIM
WINDOW = 128
ATT_BLOCK = 128
ROPE_BASE = 10000.0
FFN_HIDDEN = ((8 * D_MODEL + 3 * 256 - 1) // (3 * 256)) * 256
EPS = 1e-6

LANES = 128
ROW_TILE = 512
DFT_TILE = 256
MXU_TILE = 256
FFN_SPLIT = (FFN_HIDDEN // MXU_TILE + 1) // 2 * MXU_TILE
FFN_CHUNKS = ((0, FFN_SPLIT), (FFN_SPLIT, FFN_HIDDEN))
IN_SPLIT = (3 * HY_W, SSM_W, XBC_W, 2 * SSM_HEADS, ATT_W, KV_W, KV_W, 3 * D_MODEL)
IN_WIDTHS = (3 * HY_W, SSM_W, XBC_W, LANES, ATT_W, KV_W, KV_W, 3 * D_MODEL)
IN_ALIGNED = sum(IN_SPLIT[:3])
IN_TAIL = sum(IN_WIDTHS) - IN_ALIGNED
VMEM_LIMIT = 56 * 1024 * 1024


def _params(n_grid):
    return pltpu.CompilerParams(dimension_semantics=("arbitrary",) * n_grid, vmem_limit_bytes=VMEM_LIMIT)


def _resident(shape, layer=None):
    nd = len(shape)
    if layer is None:
        return pl.BlockSpec(shape, lambda *_: (0,) * nd, pipeline_mode=pl.Buffered(1))
    return pl.BlockSpec((None, *shape), lambda *_: (layer,) + (0,) * nd, pipeline_mode=pl.Buffered(1))


def _bdot(a, b):
    return jnp.dot(a.astype(BF16), b.astype(BF16), preferred_element_type=F32)


def _dot_nt(a, b):
    return lax.dot_general(a.astype(BF16), b.astype(BF16), (((1,), (1,)), ((), ())), preferred_element_type=F32)


def _dot_tn(a, b):
    return lax.dot_general(a.astype(BF16), b.astype(BF16), (((0,), (0,)), ((), ())), preferred_element_type=F32)


def _split2(x):
    hi = x.astype(BF16)
    lo = (x - hi.astype(F32)).astype(BF16)
    return hi, lo


def _split3(x):
    h1 = x.astype(BF16)
    r1 = x - h1.astype(F32)
    h2 = r1.astype(BF16)
    h3 = (r1 - h2.astype(F32)).astype(BF16)
    return h1, h2, h3


def _dot3(a, b):
    ah, al = _split2(a)
    bh, bl = _split2(b)
    return (jnp.dot(ah, bh, preferred_element_type=F32) + jnp.dot(ah, bl, preferred_element_type=F32)
            + jnp.dot(al, bh, preferred_element_type=F32))


def _rms(x, g):
    return x * lax.rsqrt(jnp.mean(x * x, axis=-1, keepdims=True) + EPS) * g


def _silu(x):
    return x * jax.nn.sigmoid(x)


def _dwconv(x, w, b):
    n = x.shape[0]
    row = lax.broadcasted_iota(jnp.int32, (n, 1), 0)
    prev = jnp.where(row == 0, 0.0, pltpu.roll(x, 1, 0))
    nxt = jnp.where(row == n - 1, 0.0, pltpu.roll(x, n - 1, 0))
    return prev * w[0:1, :] + x * w[1:2, :] + nxt * w[2:3, :] + b


def _ada_kernel(cond_ref, w_ref, b_ref, o_ref):
    o_ref[0] = _bdot(_silu(cond_ref[...]), w_ref[0]) + b_ref[0]


def _ada_call(cond, ada_w, ada_b):
    n = 6 * D_MODEL
    tn = n // 4
    rows = cond.shape[0]
    return pl.pallas_call(
        _ada_kernel,
        grid=(DEPTH, n // tn),
        in_specs=[
            pl.BlockSpec((rows, D_MODEL), lambda l, j: (0, 0)),
            pl.BlockSpec((1, D_MODEL, tn), lambda l, j: (l, 0, j)),
            pl.BlockSpec((1, 1, tn), lambda l, j: (l, 0, j)),
        ],
        out_specs=pl.BlockSpec((1, rows, tn), lambda l, j: (l, 0, j)),
        out_shape=jax.ShapeDtypeStruct((DEPTH, rows, n), F32),
        compiler_params=_params(2),
        name="adaln",
    )(cond, ada_w, ada_b.reshape(DEPTH, 1, n))


def _in_kernel(x_ref, mod_ref, g_ref, w_ref, *refs):
    o_refs, tail = refs[:-1], refs[-1]

    @pl.when(pl.program_id(0) == 0)
    def _():
        dt_w = IN_SPLIT[3]
        body = IN_TAIL - LANES
        lane = lax.broadcasted_iota(jnp.int32, (1, LANES), 1)
        tail[:, 0:LANES] = jnp.where(lane < dt_w, w_ref[:, IN_ALIGNED:IN_ALIGNED + LANES], 0)
        tail[:, LANES:] = pltpu.roll(w_ref[:, IN_ALIGNED:IN_ALIGNED + body], body - dt_w, 1)
        tail[:, IN_TAIL - dt_w:] = w_ref[:, IN_ALIGNED + body:]

    h = _rms(x_ref[...], g_ref[...]) * (1.0 + mod_ref[0, 1:2, :]) + mod_ref[0, 0:1, :]
    hb = h.astype(BF16)
    off = 0
    for o_ref, wd in zip(o_refs, IN_WIDTHS):
        w = w_ref[:, off:off + wd] if off < IN_ALIGNED else tail[:, off - IN_ALIGNED:off - IN_ALIGNED + wd]
        o_ref[...] = jnp.dot(hb, w, preferred_element_type=F32)
        off += wd


def _mod_spec(shared, sl):
    if shared:
        return pl.BlockSpec((1, 6, D_MODEL), lambda i: (0, 0, 0))
    assert sl % ROW_TILE == 0
    return pl.BlockSpec((1, 6, D_MODEL), lambda i: (i // (sl // ROW_TILE), 0, 0))


def _in_call(x, mod, g, w, layer, nb, sl, shared_mod):
    n_tok = nb * sl
    assert n_tok % ROW_TILE == 0
    row = lambda i: (i, 0)
    return pl.pallas_call(
        _in_kernel,
        grid=(n_tok // ROW_TILE,),
        in_specs=[
            pl.BlockSpec((ROW_TILE, D_MODEL), row),
            _mod_spec(shared_mod, sl),
            _resident((1, D_MODEL)),
            _resident((D_MODEL, sum(IN_SPLIT)), layer),
        ],
        out_specs=[pl.BlockSpec((ROW_TILE, wd), row) for wd in IN_WIDTHS],
        out_shape=[jax.ShapeDtypeStruct((n_tok, wd), F32) for wd in IN_WIDTHS],
        scratch_shapes=[pltpu.VMEM((D_MODEL, IN_TAIL), BF16)],
        compiler_params=_params(1),
        name="in_proj",
    )(x, mod, g, w)


def _filter_kernel(feat_ref, w1, b1, f1, w2, b2, f2, w3, dec, cos_ref, sin_ref, kre_ref, kim_ref, *, sl_len):
    feat = feat_ref[...]
    h = jnp.sin(f1[...] * (_dot3(feat, w1[...]) + b1[...]))
    h = jnp.sin(f2[...] * (_dot3(h, w2[...]) + b2[...]))
    t = feat[:, 0:1]
    filt = _dot3(h, w3[...]) * jnp.exp(-t * dec[...])
    row = lax.broadcasted_iota(jnp.int32, (sl_len, 1), 0)
    hf = filt[:, :HY_W]
    hb = jnp.where(row == 0, 0.0, filt[:, HY_W:])
    ksum = hf + hb
    kdiff = hb - hf
    kre = _bdot(cos_ref[...], ksum)
    kim = _bdot(sin_ref[...], kdiff)
    nyq = jnp.sum(jnp.where(row % 2 == 1, -ksum, ksum), axis=0, keepdims=True)
    kim = jnp.where(row == 0, nyq, kim)
    scale = jnp.where(row == 0, 0.5 / sl_len, 1.0 / sl_len)
    kre_ref[0] = kre * scale
    kim_ref[0] = kim * scale


def _filter_call(feat, w1, b1, f1, w2, b2, f2, w3, dec, cos, sinp, sl):
    hid = HY_HIDDEN
    return pl.pallas_call(
        functools.partial(_filter_kernel, sl_len=sl),
        grid=(2,),
        in_specs=[
            _resident((sl, LANES)), _resident((LANES, hid)), _resident((1, hid)), _resident((1, hid)),
            _resident((hid, hid)), _resident((1, hid)), _resident((1, hid)),
            pl.BlockSpec((hid, 2 * HY_W), lambda o: (0, o)),
            pl.BlockSpec((1, 2 * HY_W), lambda o: (0, o)),
            _resident((sl, sl)), _resident((sl, sl)),
        ],
        out_specs=[pl.BlockSpec((1, sl, HY_W), lambda o: (o, 0, 0))] * 2,
        out_shape=[jax.ShapeDtypeStruct((2, sl, HY_W), F32)] * 2,
        compiler_params=_params(1),
        name="hyena_filter",
    )(feat, w1, b1, f1, w2, b2, f2, w3, dec, cos, sinp)


def _hyena_kernel(x_ref, cw, cb, bias, kre, kim, cos, sin, sin_t, o_ref, hc, u, pre_s, pim_s, *, sl_len):
    tile = min(DFT_TILE, sl_len)
    nt = sl_len // tile
    hc[...] = _dwconv(x_ref[...], cw[...], cb[...])
    u[...] = hc[:, 0:HY_W].astype(BF16)

    def conv(order, epilogue):
        def fwd(i, carry):
            r = pl.multiple_of(i * tile, tile)
            rows = pl.ds(r, tile)
            a = jnp.dot(cos[rows, :], u[...], preferred_element_type=F32)
            b = jnp.dot(sin[rows, :], u[...], preferred_element_type=F32)
            kr = kre[order, rows, :]
            ki = kim[order, rows, :]
            first = (r + lax.broadcasted_iota(jnp.int32, (tile, 1), 0)) == 0
            pre = jnp.where(first, a * kr, a * kr + b * ki)
            pim = jnp.where(first, b * ki, b * kr - a * ki)
            pre_s[rows, :] = pre.astype(BF16)
            pim_s[rows, :] = pim.astype(BF16)
            return carry

        lax.fori_loop(0, nt, fwd, 0, unroll=True)

        def inv(i, carry):
            r = pl.multiple_of(i * tile, tile)
            rows = pl.ds(r, tile)
            y = (jnp.dot(cos[rows, :], pre_s[...], preferred_element_type=F32)
                 + jnp.dot(sin_t[rows, :], pim_s[...], preferred_element_type=F32))
            epilogue(rows, y)
            return carry

        lax.fori_loop(0, nt, inv, 0, unroll=True)

    def after_first(rows, y):
        z = hc[rows, HY_W:2 * HY_W] * (y + hc[rows, 0:HY_W] * bias[0:1, :])
        hc[rows, 0:HY_W] = z
        u[rows, :] = z.astype(BF16)

    def after_second(rows, y):
        o_ref[rows, :] = (hc[rows, 2 * HY_W:3 * HY_W] * (y + hc[rows, 0:HY_W] * bias[1:2, :])).astype(BF16)

    conv(0, after_first)
    conv(1, after_second)


def _hyena_call(hy_in, cw, cb, bias, kre, kim, mats, nb, sl):
    seq = lambda b: (b, 0)
    mat = _resident((sl, sl))
    return pl.pallas_call(
        functools.partial(_hyena_kernel, sl_len=sl),
        grid=(nb,),
        in_specs=[
            pl.BlockSpec((sl, 3 * HY_W), seq),
            _resident((3, 3 * HY_W)), _resident((1, 3 * HY_W)), _resident((2, HY_W)),
            _resident((2, sl, HY_W)), _resident((2, sl, HY_W)),
            mat, mat, mat,
        ],
        out_specs=pl.BlockSpec((sl, HY_W), seq),
        out_shape=jax.ShapeDtypeStruct((nb * sl, HY_W), BF16),
        scratch_shapes=[pltpu.VMEM((sl, 3 * HY_W), F32)] + [pltpu.VMEM((sl, HY_W), BF16)] * 3,
        compiler_params=_params(1),
        name="hyena_conv",
    )(hy_in, cw, cb, bias, kre, kim, *mats)


SSM_PAIRS = SSM_HEADS // 2
PAIR_W = 2 * SSM_HEAD_DIM


def _ssd_kernel(xbc_ref, dt_ref, h0_ref, cw, cb, dtb, alog, dsk, expand, o_ref, hfin_ref, xbc, dt, dta, hst, y_bwd, *,
                sl_len):
    nc = sl_len // SSM_CHUNK
    q = SSM_CHUNK
    xbc[...] = _silu(_dwconv(xbc_ref[...], cw[...], cb[...]))
    dtv = jax.nn.softplus(dt_ref[...] + dtb[...])
    dt[...] = dtv
    dta[...] = dtv * (-jnp.exp(alog[...]))
    hst[...] = h0_ref[0]
    ii = lax.broadcasted_iota(jnp.int32, (q, q), 0)
    jj = lax.broadcasted_iota(jnp.int32, (q, q), 1)
    low_lanes = lax.broadcasted_iota(jnp.int32, (1, PAIR_W), 1) < SSM_HEAD_DIM
    low_rows = lax.broadcasted_iota(jnp.int32, (2 * SSM_STATE, 1), 0) < SSM_STATE

    def chunk(c, d):
        r = pl.multiple_of(c * q, q)
        rows = pl.ds(r, q)
        xs = xbc[rows, 0:SSM_W]
        bm = xbc[rows, SSM_W:SSM_W + SSM_GROUPS * SSM_STATE]
        cm = xbc[rows, SSM_W + SSM_GROUPS * SSM_STATE:XBC_W]
        tri = (jj <= ii) if d == 0 else (jj >= ii)
        trib = tri.astype(BF16)
        a1, a2, a3 = _split3(dta[rows, :])
        acum = (jnp.dot(trib, a1, preferred_element_type=F32) + jnp.dot(trib, a2, preferred_element_type=F32)
                + jnp.dot(trib, a3, preferred_element_type=F32))
        acum_t = acum.T
        dtc = dt[rows, :]
        dt_t = dtc.T
        last = acum[q - 1:q, :] if d == 0 else acum[0:1, :]
        cdec = jnp.exp(last)
        w1, w2 = _split2(dtc * jnp.exp(last - acum))
        w_wide = (jnp.dot(w1, expand[d], preferred_element_type=F32)
                  + jnp.dot(w2, expand[d], preferred_element_type=F32))
        xdd = xs * w_wide
        bm_t = bm.T.astype(BF16)
        cb_t = [_bdot(jnp.where(low_lanes, cm, 0.0), bm_t), _bdot(jnp.where(low_lanes, 0.0, cm), bm_t)]
        ys = []
        for p in range(SSM_PAIRS):
            g = p // (SSM_PAIRS // SSM_GROUPS)
            lanes = slice(p * PAIR_W, (p + 1) * PAIR_W)
            scores, grow, cd = [], [], []
            for hh in range(2):
                col = d * SSM_HEADS + 2 * p + hh
                a_col = jnp.broadcast_to(acum[:, col:col + 1], (q, q))
                lmat = jnp.exp(jnp.where(tri, a_col - acum_t[col:col + 1, :], -jnp.inf))
                scores.append((cb_t[g] * lmat * dt_t[col:col + 1, :]).astype(BF16))
                grow.append(jnp.exp(a_col))
                cd.append(cdec[:, col:col + 1])
            xp = xs[:, lanes]
            x_blocks = jnp.concatenate([jnp.where(low_lanes, xp, 0.0), jnp.where(low_lanes, 0.0, xp)], axis=0)
            y_diag = jnp.dot(jnp.concatenate(scores, axis=1), x_blocks.astype(BF16), preferred_element_type=F32)
            slab = d * SSM_PAIRS + p
            h_prev = hst[slab]
            y_off = _bdot(cm, h_prev) * jnp.where(low_lanes, grow[0], grow[1])
            states = _bdot(bm_t, xdd[:, lanes])
            own_rows = low_rows if g == 0 else jnp.logical_not(low_rows)
            hst[slab] = h_prev * jnp.where(low_lanes, cd[0], cd[1]) + jnp.where(own_rows, states, 0.0)
            ys.append(y_diag + y_off)
        y = jnp.concatenate(ys, axis=1)
        if d == 0:
            o_ref[rows, :] = y + dsk[...] * xs
        else:
            y_bwd[rows, :] = y

    def step(c, carry):
        chunk(c, 0)
        chunk(nc - 1 - c, 1)
        return carry

    lax.fori_loop(0, nc, step, 0, unroll=2)
    o_ref[...] += y_bwd[...]
    for slab in range(2 * SSM_PAIRS):
        g = (slab % SSM_PAIRS) // (SSM_PAIRS // SSM_GROUPS)
        st_t = hst[slab].T
        for hh in range(2):
            hfin_ref[0, 2 * slab + hh] = st_t[hh * SSM_HEAD_DIM:(hh + 1) * SSM_HEAD_DIM,
                                              g * SSM_STATE:(g + 1) * SSM_STATE]


def _ssd_call(s_xbc, s_dt, h0, cw, cb, dtb, alog, dsk, expand, nb, sl, shared_h0):
    seq = lambda b: (b, 0)
    st_shape = (1, 2 * SSM_PAIRS, 2 * SSM_STATE, PAIR_W)
    fin_shape = (1, 2 * SSM_HEADS, SSM_HEAD_DIM, SSM_STATE)
    h0_map =(lambda b: (0, 0, 0, 0)) if shared_h0 else (lambda b: (b, 0, 0, 0))
    return pl.pallas_call(
        functools.partial(_ssd_kernel, sl_len=sl),
        grid=(nb,),
        in_specs=[
            pl.BlockSpec((sl, XBC_W), seq), pl.BlockSpec((sl, LANES), seq), pl.BlockSpec(st_shape, h0_map),
            _resident((3, XBC_W)), _resident((1, XBC_W)), _resident((1, LANES)), _resident((1, LANES)),
            _resident((1, SSM_W)), _resident((2, LANES, SSM_W)),
        ],
        out_specs=[pl.BlockSpec((sl, SSM_W), seq), pl.BlockSpec(fin_shape, lambda b: (b, 0, 0, 0))],
        out_shape=[jax.ShapeDtypeStruct((nb * sl, SSM_W), F32), jax.ShapeDtypeStruct((nb, *fin_shape[1:]), F32)],
        scratch_shapes=[pltpu.VMEM((sl, XBC_W), F32), pltpu.VMEM((sl, LANES), F32), pltpu.VMEM((sl, LANES), F32),
                        pltpu.VMEM(st_shape[1:], F32), pltpu.VMEM((sl, SSM_W), F32)],
        compiler_params=_params(1),
        name="ssd_scan",
    )(s_xbc, s_dt, h0, cw, cb, dtb, alog, dsk, expand)


def _pack_states(h):
    nb = h.shape[0]
    t = h.reshape(nb, 2, SSM_PAIRS, 2, SSM_HEAD_DIM, SSM_STATE).transpose(0, 1, 2, 5, 3, 4)
    t = t.reshape(nb, 2, SSM_PAIRS, SSM_STATE, PAIR_W)
    z = jnp.zeros_like(t)
    first = (jnp.arange(SSM_PAIRS) < SSM_PAIRS // SSM_GROUPS)[None, None, :, None, None]
    slabs = jnp.where(first, jnp.concatenate([t, z], axis=3), jnp.concatenate([z, t], axis=3))
    return slabs.reshape(nb, 2 * SSM_PAIRS, 2 * SSM_STATE, PAIR_W)


def _expand_table():
    r = jnp.arange(LANES)[:, None]
    head = jnp.arange(SSM_W)[None, :] // SSM_HEAD_DIM
    return jnp.stack([r == head, r == SSM_HEADS + head]).astype(BF16)


def _attend_group(keys, q_t, vals_t, mask, sink_row, g):
    z = jnp.zeros_like(q_t)
    rhs = jnp.concatenate([q_t, z] if g == 0 else [z, q_t], axis=0)
    s = jnp.dot(keys, rhs, preferred_element_type=F32)
    if mask is not None:
        s = jnp.where(mask, s, -jnp.inf)
    m = jnp.maximum(jnp.max(s, axis=0, keepdims=True), sink_row)
    e = jnp.exp(s - m)
    denom = jnp.sum(e, axis=0, keepdims=True) + jnp.exp(sink_row - m)
    return jnp.dot(vals_t, e.astype(BF16), preferred_element_type=F32) / denom


def _ctx_attn_kernel(q_ref, k_ref, v_ref, sink_ref, o_ref, *, sl_len):
    scale = HEAD_DIM ** -0.5
    q_t = (q_ref[...] * scale).T.astype(BF16)
    keys = k_ref[...].astype(BF16)
    v_t = v_ref[...].T.astype(BF16)
    outs = []
    for g in range(KV_HEADS):
        q_g = jnp.concatenate(
            [q_t[(g * Q_PER_KV + hq) * HEAD_DIM:(g * Q_PER_KV + hq + 1) * HEAD_DIM, :] for hq in range(Q_PER_KV)], axis=1)
        o_t = _attend_group(keys, q_g, v_t[g * HEAD_DIM:(g + 1) * HEAD_DIM, :], None, sink_ref[g], g)
        outs += [o_t[:, hq * sl_len:(hq + 1) * sl_len] for hq in range(Q_PER_KV)]
    o_ref[...] = jnp.concatenate(outs, axis=0).T.astype(BF16)


def _ctx_attn_call(q, k, v, sink_rows, nb, sl):
    seq = lambda b: (b, 0)
    return pl.pallas_call(
        functools.partial(_ctx_attn_kernel, sl_len=sl),
        grid=(nb,),
        in_specs=[pl.BlockSpec((sl, ATT_W), seq), pl.BlockSpec((sl, KV_W), seq), pl.BlockSpec((sl, KV_W), seq),
                  _resident((KV_HEADS, 1, Q_PER_KV * sl))],
        out_specs=pl.BlockSpec((sl, ATT_W), seq),
        out_shape=jax.ShapeDtypeStruct((nb * sl, ATT_W), BF16),
        compiler_params=_params(1),
        name="ctx_attention",
    )(q, k, v, sink_rows)


def _rope(x, cos, sin_signed):
    w = x.shape[1]
    lane = lax.broadcasted_iota(jnp.int32, (1, w), 1)
    partner = jnp.where(lane % 32 < 16, pltpu.roll(x, w - 16, 1), pltpu.roll(x, 16, 1))
    return x * cos + partner * sin_signed


def _lat_attn_kernel(q_ref, k_ref, v_ref, ck_ref, cv_ref, cos_ref, sin_ref, sink_ref, o_ref, q_t, kcat, v_t, o_t, *,
                     sl_len, past):
    blk = ATT_BLOCK
    nblk = sl_len // blk
    scale = HEAD_DIM ** -0.5
    q_t[...] = (_rope(q_ref[...], cos_ref[...], sin_ref[...]) * scale).T.astype(BF16)
    kcat[0:past, :] = ck_ref[0].astype(BF16)
    kcat[past:, :] = _rope(k_ref[...], cos_ref[:, 0:KV_W], sin_ref[:, 0:KV_W]).astype(BF16)
    v_t[:, 0:past] = cv_ref[0].T.astype(BF16)
    v_t[:, past:] = v_ref[...].T.astype(BF16)
    for i in range(nblk):
        cols = slice(i * blk, (i + 1) * blk)
        lo, hi = max(i - 1, 0), min(i + 2, nblk)
        band = slice(past + lo * blk, past + hi * blk)
        n_keys = past + (hi - lo) * blk
        kk = lax.broadcasted_iota(jnp.int32, (n_keys, 1), 0)
        qi = lax.broadcasted_iota(jnp.int32, (1, Q_PER_KV * blk), 1) % blk
        rel = kk - past + (lo - i) * blk
        mask = (kk < past) | (jnp.abs(qi - rel) <= WINDOW)
        keys = jnp.concatenate([kcat[0:past, :], kcat[band, :]], axis=0)
        for g in range(KV_HEADS):
            drows = slice(g * HEAD_DIM, (g + 1) * HEAD_DIM)
            q_g = jnp.concatenate(
                [q_t[(g * Q_PER_KV + hq) * HEAD_DIM:(g * Q_PER_KV + hq + 1) * HEAD_DIM, cols] for hq in range(Q_PER_KV)],
                axis=1)
            vals_t = jnp.concatenate([v_t[drows, 0:past], v_t[drows, band]], axis=1)
            out = _attend_group(keys, q_g, vals_t, mask, sink_ref[g], g)
            for hq in range(Q_PER_KV):
                h = g * Q_PER_KV + hq
                o_t[h * HEAD_DIM:(h + 1) * HEAD_DIM, cols] = out[:, hq * blk:(hq + 1) * blk]
    o_ref[...] = o_t[...].T.astype(BF16)


def _lat_attn_call(q, k, v, ck, cv, cos, sin, sink_rows, nb, sl):
    seq = lambda b: (b, 0)
    past = ck.shape[1]
    cache = pl.BlockSpec((1, past, KV_W), lambda b: (b, 0, 0))
    return pl.pallas_call(
        functools.partial(_lat_attn_kernel, sl_len=sl, past=past),
        grid=(nb,),
        in_specs=[pl.BlockSpec((sl, ATT_W), seq), pl.BlockSpec((sl, KV_W), seq), pl.BlockSpec((sl, KV_W), seq),
                  cache, cache, _resident((sl, ATT_W)), _resident((sl, ATT_W)),
                  _resident((KV_HEADS, 1, Q_PER_KV * ATT_BLOCK))],
        out_specs=pl.BlockSpec((sl, ATT_W), seq),
        out_shape=jax.ShapeDtypeStruct((nb * sl, ATT_W), BF16),
        scratch_shapes=[pltpu.VMEM((ATT_W, sl), BF16), pltpu.VMEM((past + sl, KV_W), BF16),
                        pltpu.VMEM((KV_W, past + sl), BF16), pltpu.VMEM((ATT_W, sl), F32)],
        compiler_params=_params(1),
        name="latent_attention",
    )(q, k, v, ck, cv, cos, sin, sink_rows)


def _sink_rows(sink, n_q):
    return jnp.repeat(sink.reshape(KV_HEADS, Q_PER_KV), n_q, axis=1).reshape(KV_HEADS, 1, Q_PER_KV * n_q)


def _post_kernel(x_ref, hy_ref, sy_ref, sz_ref, at_ref, g_ref, mod_ref, snorm, npost, nfpre, nfpost,
                 wh, ws, wa, wo, wg, wu, wd, o_ref):
    m = mod_ref[0]
    ssm_y = _rms(sy_ref[...] * _silu(sz_ref[...]), snorm[...])
    merged = (jax.nn.sigmoid(g_ref[:, 0:D_MODEL]) * _bdot(hy_ref[...], wh[...])
              + jax.nn.sigmoid(g_ref[:, D_MODEL:2 * D_MODEL]) * _bdot(ssm_y, ws[...])
              + jax.nn.sigmoid(g_ref[:, 2 * D_MODEL:3 * D_MODEL]) * _bdot(at_ref[...], wa[...]))
    x1 = x_ref[...] + m[2:3, :] * _rms(_bdot(merged, wo[...]), npost[...])
    hb = (_rms(x1, nfpre[...]) * (1.0 + m[4:5, :]) + m[3:4, :]).astype(BF16)
    f = None
    for lo, hi in FFN_CHUNKS:
        gate = jnp.dot(hb, wg[:, lo:hi], preferred_element_type=F32)
        up = jnp.dot(hb, wu[:, lo:hi], preferred_element_type=F32)
        part = _bdot(_silu(gate) * up, wd[lo:hi, :])
        f = part if f is None else f + part
    o_ref[...] = x1 + m[5:6, :] * _rms(f, nfpost[...])


def _post_call(x, hy, sy, sz, at, g, mod, snorm, npost, nfpre, nfpost, wh, ws, wa, wo, wg, wu, wd, layer, nb, sl,
               shared_mod):
    n_tok = nb * sl
    assert n_tok % ROW_TILE == 0
    tok = lambda wdt: pl.BlockSpec((ROW_TILE, wdt), lambda i: (i, 0))
    vec = lambda wdt: _resident((1, wdt))
    return pl.pallas_call(
        _post_kernel,
        grid=(n_tok // ROW_TILE,),
        in_specs=[tok(D_MODEL), tok(HY_W), tok(SSM_W), tok(SSM_W), tok(ATT_W), tok(3 * D_MODEL),
                  _mod_spec(shared_mod, sl),
                  vec(SSM_W), vec(D_MODEL), vec(D_MODEL), vec(D_MODEL),
                  _resident((HY_W, D_MODEL), layer), _resident((SSM_W, D_MODEL), layer),
                  _resident((ATT_W, D_MODEL), layer), _resident((D_MODEL, D_MODEL), layer),
                  _resident((D_MODEL, FFN_HIDDEN), layer), _resident((D_MODEL, FFN_HIDDEN), layer),
                  _resident((FFN_HIDDEN, D_MODEL), layer)],
        out_specs=tok(D_MODEL),
        out_shape=jax.ShapeDtypeStruct((n_tok, D_MODEL), F32),
        compiler_params=_params(1),
        name="merge_ffn",
    )(x, hy, sy, sz, at, g, mod, snorm, npost, nfpre, nfpost, wh, ws, wa, wo, wg, wu, wd)


def _dft_mats(sl):
    radix = 32
    assert sl % radix == 0
    s = jnp.arange(sl, dtype=jnp.int32)

    def angles(f):
        return ((f[:, None] * s[None, :]) % (2 * sl)).astype(F32) * (math.pi / sl)

    a_hi = angles(jnp.arange(sl // radix, dtype=jnp.int32) * radix)[:, None, :]
    a_lo = angles(jnp.arange(radix, dtype=jnp.int32))[None, :, :]
    cos = (jnp.cos(a_hi) * jnp.cos(a_lo) - jnp.sin(a_hi) * jnp.sin(a_lo)).reshape(sl, sl)
    sin = (jnp.sin(a_hi) * jnp.cos(a_lo) + jnp.cos(a_hi) * jnp.sin(a_lo)).reshape(sl, sl)
    sign = jnp.where(s % 2 == 1, -1.0, 1.0).astype(F32)
    sinp = jnp.where(s[:, None] == 0, sign[None, :], sin).astype(BF16)
    return cos.astype(BF16), sinp, sinp.T


def _filter_features(sl):
    t = jnp.arange(sl, dtype=F32) / sl
    ang = 2.0 * jnp.pi * t[:, None] * jnp.arange(1, HY_BANDS + 1, dtype=F32)[None]
    feat = jnp.concatenate([t[:, None], jnp.cos(ang), jnp.sin(ang)], axis=-1)
    return jnp.pad(feat, ((0, 0), (0, LANES - HY_FEAT)))


def _rope_tables(sl):
    rows = sl // GRID_W
    row = jnp.repeat(jnp.arange(rows), GRID_W).astype(F32)
    col = jnp.tile(jnp.arange(GRID_W), rows).astype(F32)
    nq = HEAD_DIM // 4
    inv = ROPE_BASE ** (-jnp.arange(nq, dtype=F32) / nq)
    ar = row[:, None] * inv
    ac = col[:, None] * inv
    cos = jnp.concatenate([jnp.cos(ar), jnp.cos(ar), jnp.cos(ac), jnp.cos(ac)], axis=-1)
    sin = jnp.concatenate([-jnp.sin(ar), jnp.sin(ar), -jnp.sin(ac), jnp.sin(ac)], axis=-1)
    return jnp.tile(cos, (1, N_HEADS)), jnp.tile(sin, (1, N_HEADS))


def _pad_lanes(v):
    return jnp.pad(v.reshape(1, -1), ((0, 0), (0, LANES - v.size)))


def kernel(x_prompt, x_sample, c, cache_k, cache_v, state_ssm, c_ctx, ada_w, ada_b,
           norm_mix_pre, norm_mix_post, norm_ffn_pre, norm_ffn_post, w_in,
           hy_conv_w, hy_conv_b, hy_w1, hy_b1, hy_freq1, hy_w2, hy_b2, hy_freq2, hy_w3,
           hy_decay, hy_bias, ssm_conv_w, ssm_conv_b, ssm_a_log, ssm_dt_bias, ssm_d, ssm_norm,
           attn_sink, hy_proj, ssm_proj, attn_proj, w_out, ffn_w_gate, ffn_w_up, ffn_w_down):
    nb_c, sl_c, _ = x_prompt.shape
    nb_l, sl_l, _ = x_sample.shape
    past = cache_k.shape[2]

    cond = jnp.concatenate([c_ctx[None], c, jnp.zeros((8 - 1 - nb_l, D_MODEL), F32)], axis=0)
    mod = _ada_call(cond, ada_w, ada_b).reshape(DEPTH, 8, 6, D_MODEL)

    streams = {}
    for name, x in (("ctx", x_prompt), ("lat", x_sample)):
        nb, sl, _ = x.shape
        streams[name] = dict(nb=nb, sl=sl, x=x.reshape(nb * sl, D_MODEL), mats=_dft_mats(sl),
                             feat=_filter_features(sl))
    rope_cos, rope_sin = _rope_tables(sl_l)
    zero_state = jnp.zeros((1, 2 * SSM_PAIRS, 2 * SSM_STATE, PAIR_W), F32)
    expand = _expand_table()
    row = lambda v: v.reshape(1, -1)

    w_in_p = w_in.astype(BF16)
    wh, ws, wa, wo = (w.astype(BF16) for w in (hy_proj, ssm_proj, attn_proj, w_out))
    wg, wu, wd = (w.astype(BF16) for w in (ffn_w_gate, ffn_w_up, ffn_w_down))

    ks, vs, ss = [], [], []
    for l in range(DEPTH):
        w1p = jnp.pad(hy_w1[l], ((0, LANES - HY_FEAT), (0, 0)))
        dtb = _pad_lanes(ssm_dt_bias[l])
        alog = _pad_lanes(ssm_a_log[l])
        dsk = jnp.repeat(ssm_d[l, 0] + ssm_d[l, 1], SSM_HEAD_DIM).reshape(1, SSM_W)
        sink = {"ctx": _sink_rows(attn_sink[l], sl_c), "lat": _sink_rows(attn_sink[l], ATT_BLOCK)}
        for name in ("ctx", "lat"):
            st = streams[name]
            nb, sl, is_ctx = st["nb"], st["sl"], name == "ctx"
            m = mod[l, 0:1] if is_ctx else mod[l, 1:1 + nb]
            kre, kim = _filter_call(st["feat"], w1p, row(hy_b1[l]), row(hy_freq1[l]), hy_w2[l], row(hy_b2[l]),
                                    row(hy_freq2[l]), hy_w3[l], row(hy_decay[l]), st["mats"][0], st["mats"][1], sl)
            hy_in, s_z, s_xbc, s_dt, a_q, a_k, a_v, gate = _in_call(
                st["x"], m, row(norm_mix_pre[l]), w_in_p, l, nb, sl, is_ctx)
            hy_y = _hyena_call(hy_in, hy_conv_w[l], row(hy_conv_b[l]), hy_bias[l], kre, kim, st["mats"], nb, sl)
            h0 = zero_state if is_ctx else _pack_states(state_ssm[:, l])
            ssm_y, h_fin = _ssd_call(s_xbc, s_dt, h0, ssm_conv_w[l], row(ssm_conv_b[l]), dtb, alog, dsk, expand,
                                     nb, sl, is_ctx)
            if is_ctx:
                att = _ctx_attn_call(a_q, a_k, a_v, sink[name], nb, sl)
                ks.append(a_k.reshape(nb, sl, KV_HEADS, HEAD_DIM))
                vs.append(a_v.reshape(nb, sl, KV_HEADS, HEAD_DIM))
                ss.append(h_fin.reshape(nb, 2, SSM_HEADS, SSM_HEAD_DIM, SSM_STATE))
            else:
                att = _lat_attn_call(a_q, a_k, a_v, cache_k[:, l].reshape(nb, past, KV_W),
                                     cache_v[:, l].reshape(nb, past, KV_W), rope_cos, rope_sin, sink[name], nb, sl)
            st["x"] = _post_call(st["x"], hy_y, ssm_y, s_z, att, gate, m, row(ssm_norm[l]), row(norm_mix_post[l]),
                                 row(norm_ffn_pre[l]), row(norm_ffn_post[l]), wh, ws, wa, wo, wg, wu, wd, l, nb, sl,
                                 is_ctx)

    y_prompt = streams["ctx"]["x"].reshape(nb_c, sl_c, D_MODEL)
    y_sample = streams["lat"]["x"].reshape(nb_l, sl_l, D_MODEL)
    return (y_prompt, y_sample, jnp.stack(ks, axis=1), jnp.stack(vs, axis=1), jnp.stack(ss, axis=1))
```

```python
import functools
import math

import jax
import jax.numpy as jnp
from jax import lax
from jax.experimental import pallas as pl
from jax.experimental.pallas import tpu as pltpu

F32 = jnp.float32
BF16 = jnp.bfloat16

D_MODEL = 1024
DEPTH = 2
GRID_W = 64
HY_W = 512
HY_BANDS = 16
HY_FEAT = 1 + 2 * HY_BANDS
HY_HIDDEN = 64
SSM_W = 512
SSM_HEAD_DIM = 64
SSM_HEADS = 8
SSM_GROUPS = 2
SSM_STATE = 64
SSM_CHUNK = 128
XBC_W = SSM_W + 2 * SSM_GROUPS * SSM_STATE
N_HEADS = 8
KV_HEADS = 2
HEAD_DIM = 64
Q_PER_KV = N_HEADS // KV_HEADS
ATT_W = N_HEADS * HEAD_DIM
KV_W = KV_HEADS * HEAD_DIM
WINDOW = 128
ATT_BLOCK = 128
ROPE_BASE = 10000.0
FFN_HIDDEN = ((8 * D_MODEL + 3 * 256 - 1) // (3 * 256)) * 256
EPS = 1e-6

LANES = 128
ROW_TILE = 512
DFT_TILE = 256
MXU_TILE = 256
FFN_SPLIT = (FFN_HIDDEN // MXU_TILE + 1) // 2 * MXU_TILE
FFN_CHUNKS = ((0, FFN_SPLIT), (FFN_SPLIT, FFN_HIDDEN))
IN_SPLIT = (3 * HY_W, SSM_W, XBC_W, 2 * SSM_HEADS, ATT_W, KV_W, KV_W, 3 * D_MODEL)
IN_WIDTHS = (3 * HY_W, SSM_W, XBC_W, LANES, ATT_W, KV_W, KV_W, 3 * D_MODEL)
IN_ALIGNED = sum(IN_SPLIT[:3])
IN_TAIL = sum(IN_WIDTHS) - IN_ALIGNED
VMEM_LIMIT = 56 * 1024 * 1024


def _params(n_grid):
    return pltpu.CompilerParams(dimension_semantics=("arbitrary",) * n_grid, vmem_limit_bytes=VMEM_LIMIT)


def _resident(shape, layer=None):
    nd = len(shape)
    if layer is None:
        return pl.BlockSpec(shape, lambda *_: (0,) * nd, pipeline_mode=pl.Buffered(1))
    return pl.BlockSpec((None, *shape), lambda *_: (layer,) + (0,) * nd, pipeline_mode=pl.Buffered(1))


def _bdot(a, b):
    return jnp.dot(a.astype(BF16), b.astype(BF16), preferred_element_type=F32)


def _dot_nt(a, b):
    return lax.dot_general(a.astype(BF16), b.astype(BF16), (((1,), (1,)), ((), ())), preferred_element_type=F32)


def _dot_tn(a, b):
    return lax.dot_general(a.astype(BF16), b.astype(BF16), (((0,), (0,)), ((), ())), preferred_element_type=F32)


def _split2(x):
    hi = x.astype(BF16)
    lo = (x - hi.astype(F32)).astype(BF16)
    return hi, lo


def _split3(x):
    h1 = x.astype(BF16)
    r1 = x - h1.astype(F32)
    h2 = r1.astype(BF16)
    h3 = (r1 - h2.astype(F32)).astype(BF16)
    return h1, h2, h3


def _dot3(a, b):
    ah, al = _split2(a)
    bh, bl = _split2(b)
    return (jnp.dot(ah, bh, preferred_element_type=F32) + jnp.dot(ah, bl, preferred_element_type=F32)
            + jnp.dot(al, bh, preferred_element_type=F32))


def _rms(x, g):
    return x * lax.rsqrt(jnp.mean(x * x, axis=-1, keepdims=True) + EPS) * g


def _silu(x):
    return x * jax.nn.sigmoid(x)


def _dwconv(x, w, b):
    n = x.shape[0]
    row = lax.broadcasted_iota(jnp.int32, (n, 1), 0)
    prev = jnp.where(row == 0, 0.0, pltpu.roll(x, 1, 0))
    nxt = jnp.where(row == n - 1, 0.0, pltpu.roll(x, n - 1, 0))
    return prev * w[0:1, :] + x * w[1:2, :] + nxt * w[2:3, :] + b


def _ada_kernel(cond_ref, w_ref, b_ref, o_ref):
    o_ref[0] = _bdot(_silu(cond_ref[...]), w_ref[0]) + b_ref[0]


def _ada_call(cond, ada_w, ada_b):
    n = 6 * D_MODEL
    tn = n // 4
    rows = cond.shape[0]
    return pl.pallas_call(
        _ada_kernel,
        grid=(DEPTH, n // tn),
        in_specs=[
            pl.BlockSpec((rows, D_MODEL), lambda l, j: (0, 0)),
            pl.BlockSpec((1, D_MODEL, tn), lambda l, j: (l, 0, j)),
            pl.BlockSpec((1, 1, tn), lambda l, j: (l, 0, j)),
        ],
        out_specs=pl.BlockSpec((1, rows, tn), lambda l, j: (l, 0, j)),
        out_shape=jax.ShapeDtypeStruct((DEPTH, rows, n), F32),
        compiler_params=_params(2),
        name="adaln",
    )(cond, ada_w, ada_b.reshape(DEPTH, 1, n))


def _in_kernel(x_ref, mod_ref, g_ref, wa_ref, wb_ref, *refs):
    o_refs, tail = refs[:-1], refs[-1]

    @pl.when(pl.program_id(0) == 0)
    def _():
        dt_w = IN_SPLIT[3]
        rest = wb_ref[...]
        lane = lax.broadcasted_iota(jnp.int32, (1, LANES), 1)
        tail[:, 0:LANES] = jnp.where(lane < dt_w, rest[:, 0:LANES], 0)
        tail[:, LANES:] = pltpu.roll(rest, IN_TAIL - dt_w, 1)[:, 0:IN_TAIL - LANES]

    h = _rms(x_ref[...], g_ref[...]) * (1.0 + mod_ref[0, 1:2, :]) + mod_ref[0, 0:1, :]
    hb = h.astype(BF16)
    off = 0
    for o_ref, wd in zip(o_refs, IN_WIDTHS):
        w = wa_ref[:, off:off + wd] if off < IN_ALIGNED else tail[:, off - IN_ALIGNED:off - IN_ALIGNED + wd]
        o_ref[...] = jnp.dot(hb, w, preferred_element_type=F32)
        off += wd


def _mod_spec(shared, sl):
    if shared:
        return pl.BlockSpec((1, 6, D_MODEL), lambda i: (0, 0, 0))
    assert sl % ROW_TILE == 0
    return pl.BlockSpec((1, 6, D_MODEL), lambda i: (i // (sl // ROW_TILE), 0, 0))


def _in_call(x, mod, g, wa, wb, layer, nb, sl, shared_mod):
    n_tok = nb * sl
    assert n_tok % ROW_TILE == 0
    row = lambda i: (i, 0)
    return pl.pallas_call(
        _in_kernel,
        grid=(n_tok // ROW_TILE,),
        in_specs=[
            pl.BlockSpec((ROW_TILE, D_MODEL), row),
            _mod_spec(shared_mod, sl),
            _resident((1, D_MODEL)),
            _resident((D_MODEL, IN_ALIGNED), layer),
            _resident((D_MODEL, IN_TAIL), layer),
        ],
        out_specs=[pl.BlockSpec((ROW_TILE, wd), row) for wd in IN_WIDTHS],
        out_shape=[jax.ShapeDtypeStruct((n_tok, wd), F32) for wd in IN_WIDTHS],
        scratch_shapes=[pltpu.VMEM((D_MODEL, IN_TAIL), BF16)],
        compiler_params=_params(1),
        name="in_proj",
    )(x, mod, g, wa, wb)


def _filter_kernel(feat_ref, w1, b1, f1, w2, b2, f2, w3, dec, cos_ref, sin_ref, kre_ref, kim_ref, *, sl_len):
    feat = feat_ref[...]
    h = jnp.sin(f1[...] * (_dot3(feat, w1[...]) + b1[...]))
    h = jnp.sin(f2[...] * (_dot3(h, w2[...]) + b2[...]))
    t = feat[:, 0:1]
    filt = _dot3(h, w3[...]) * jnp.exp(-t * dec[...])
    row = lax.broadcasted_iota(jnp.int32, (sl_len, 1), 0)
    hf = jnp.concatenate([filt[:, 0:HY_W], filt[:, 2 * HY_W:3 * HY_W]], axis=1)
    hb = jnp.where(row == 0, 0.0, jnp.concatenate([filt[:, HY_W:2 * HY_W], filt[:, 3 * HY_W:4 * HY_W]], axis=1))
    ksum = hf + hb
    kre = _bdot(cos_ref[...], ksum)
    kim = _bdot(sin_ref[...], hb - hf)
    nyq = jnp.sum(jnp.where(row % 2 == 1, -ksum, ksum), axis=0, keepdims=True)
    scale = jnp.where(row == 0, 0.5 / sl_len, 1.0 / sl_len)
    kre = kre * scale
    kim = jnp.where(row == 0, nyq, kim) * scale
    for o in range(2):
        kre_ref[o] = kre[:, o * HY_W:(o + 1) * HY_W]
        kim_ref[o] = kim[:, o * HY_W:(o + 1) * HY_W]


def _filter_call(feat, w1, b1, f1, w2, b2, f2, w3, dec, cos, sinp, sl):
    hid = HY_HIDDEN
    return pl.pallas_call(
        functools.partial(_filter_kernel, sl_len=sl),
        grid=(1,),
        in_specs=[
            _resident((sl, LANES)), _resident((LANES, hid)), _resident((1, hid)), _resident((1, hid)),
            _resident((hid, hid)), _resident((1, hid)), _resident((1, hid)),
            _resident((hid, 4 * HY_W)), _resident((1, 4 * HY_W)),
            _resident((sl, sl)), _resident((sl, sl)),
        ],
        out_specs=[pl.BlockSpec((2, sl, HY_W), lambda i: (0, 0, 0))] * 2,
        out_shape=[jax.ShapeDtypeStruct((2, sl, HY_W), F32)] * 2,
        compiler_params=_params(1),
        name="hyena_filter",
    )(feat, w1, b1, f1, w2, b2, f2, w3, dec, cos, sinp)


def _hyena_kernel(x_ref, cw, cb, bias, kre, kim, cos, sin, sin_t, o_ref, hc, u, pre_s, pim_s, *, sl_len):
    tile = min(DFT_TILE, sl_len)
    nt = sl_len // tile
    hc[...] = _dwconv(x_ref[...], cw[...], cb[...])
    u[...] = hc[:, 0:HY_W].astype(BF16)

    def conv(order, epilogue):
        def fwd(i, carry):
            r = pl.multiple_of(i * tile, tile)
            rows = pl.ds(r, tile)
            a = jnp.dot(cos[rows, :], u[...], preferred_element_type=F32)
            b = jnp.dot(sin[rows, :], u[...], preferred_element_type=F32)
            kr = kre[order, rows, :]
            ki = kim[order, rows, :]
            first = (r + lax.broadcasted_iota(jnp.int32, (tile, 1), 0)) == 0
            pre = jnp.where(first, a * kr, a * kr + b * ki)
            pim = jnp.where(first, b * ki, b * kr - a * ki)
            pre_s[rows, :] = pre.astype(BF16)
            pim_s[rows, :] = pim.astype(BF16)
            return carry

        lax.fori_loop(0, nt, fwd, 0, unroll=True)

        def inv(i, carry):
            r = pl.multiple_of(i * tile, tile)
            rows = pl.ds(r, tile)
            y = (jnp.dot(cos[rows, :], pre_s[...], preferred_element_type=F32)
                 + jnp.dot(sin_t[rows, :], pim_s[...], preferred_element_type=F32))
            epilogue(rows, y)
            return carry

        lax.fori_loop(0, nt, inv, 0, unroll=True)

    def after_first(rows, y):
        z = hc[rows, HY_W:2 * HY_W] * (y + hc[rows, 0:HY_W] * bias[0:1, :])
        hc[rows, 0:HY_W] = z
        u[rows, :] = z.astype(BF16)

    def after_second(rows, y):
        o_ref[rows, :] = (hc[rows, 2 * HY_W:3 * HY_W] * (y + hc[rows, 0:HY_W] * bias[1:2, :])).astype(BF16)

    conv(0, after_first)
    conv(1, after_second)


def _hyena_call(hy_in, cw, cb, bias, kre, kim, mats, nb, sl):
    seq = lambda b: (b, 0)
    mat = _resident((sl, sl))
    return pl.pallas_call(
        functools.partial(_hyena_kernel, sl_len=sl),
        grid=(nb,),
        in_specs=[
            pl.BlockSpec((sl, 3 * HY_W), seq),
            _resident((3, 3 * HY_W)), _resident((1, 3 * HY_W)), _resident((2, HY_W)),
            _resident((2, sl, HY_W)), _resident((2, sl, HY_W)),
            mat, mat, mat,
        ],
        out_specs=pl.BlockSpec((sl, HY_W), seq),
        out_shape=jax.ShapeDtypeStruct((nb * sl, HY_W), BF16),
        scratch_shapes=[pltpu.VMEM((sl, 3 * HY_W), F32)] + [pltpu.VMEM((sl, HY_W), BF16)] * 3,
        compiler_params=_params(1),
        name="hyena_conv",
    )(hy_in, cw, cb, bias, kre, kim, *mats)


SSM_PAIRS = SSM_HEADS // 2
PAIR_W = 2 * SSM_HEAD_DIM


def _ssd_kernel(xbc_ref, dt_ref, h0_ref, cw, cb, dtb, alog, dsk, expand, o_ref, hfin_ref, xbc, dt, dta, hst, y_bwd, *,
                sl_len):
    nc = sl_len // SSM_CHUNK
    q = SSM_CHUNK
    xbc[...] = _silu(_dwconv(xbc_ref[...], cw[...], cb[...]))
    dtv = jax.nn.softplus(dt_ref[...] + dtb[...])
    dt[...] = dtv
    dta[...] = dtv * (-jnp.exp(alog[...]))
    hst[...] = h0_ref[0]
    ii = lax.broadcasted_iota(jnp.int32, (q, q), 0)
    jj = lax.broadcasted_iota(jnp.int32, (q, q), 1)
    low_lanes = lax.broadcasted_iota(jnp.int32, (1, PAIR_W), 1) < SSM_HEAD_DIM
    low_rows = lax.broadcasted_iota(jnp.int32, (2 * SSM_STATE, 1), 0) < SSM_STATE

    def chunk(c, d):
        r = pl.multiple_of(c * q, q)
        rows = pl.ds(r, q)
        xs = xbc[rows, 0:SSM_W]
        bm = xbc[rows, SSM_W:SSM_W + SSM_GROUPS * SSM_STATE]
        cm = xbc[rows, SSM_W + SSM_GROUPS * SSM_STATE:XBC_W]
        tri = (jj <= ii) if d == 0 else (jj >= ii)
        trib = tri.astype(BF16)
        a1, a2, a3 = _split3(dta[rows, :])
        acum = (jnp.dot(trib, a1, preferred_element_type=F32) + jnp.dot(trib, a2, preferred_element_type=F32)
                + jnp.dot(trib, a3, preferred_element_type=F32))
        acum_t = acum.T
        dtc = dt[rows, :]
        dt_t = dtc.T
        last = acum[q - 1:q, :] if d == 0 else acum[0:1, :]
        cdec = jnp.exp(last)
        w1, w2 = _split2(dtc * jnp.exp(last - acum))
        w_wide = (jnp.dot(w1, expand[d], preferred_element_type=F32)
                  + jnp.dot(w2, expand[d], preferred_element_type=F32))
        xdd = xs * w_wide
        bm_t = bm.T.astype(BF16)
        cb_t = [_bdot(jnp.where(low_lanes, cm, 0.0), bm_t), _bdot(jnp.where(low_lanes, 0.0, cm), bm_t)]
        ys = []
        for p in range(SSM_PAIRS):
            g = p // (SSM_PAIRS // SSM_GROUPS)
            lanes = slice(p * PAIR_W, (p + 1) * PAIR_W)
            scores, grow, cd = [], [], []
            for hh in range(2):
                col = d * SSM_HEADS + 2 * p + hh
                a_col = jnp.broadcast_to(acum[:, col:col + 1], (q, q))
                lmat = jnp.exp(jnp.where(tri, a_col - acum_t[col:col + 1, :], -jnp.inf))
                scores.append((cb_t[g] * lmat * dt_t[col:col + 1, :]).astype(BF16))
                grow.append(jnp.exp(a_col))
                cd.append(cdec[:, col:col + 1])
            xp = xs[:, lanes]
            x_blocks = jnp.concatenate([jnp.where(low_lanes, xp, 0.0), jnp.where(low_lanes, 0.0, xp)], axis=0)
            y_diag = jnp.dot(jnp.concatenate(scores, axis=1), x_blocks.astype(BF16), preferred_element_type=F32)
            slab = d * SSM_PAIRS + p
            h_prev = hst[slab]
            y_off = _bdot(cm, h_prev) * jnp.where(low_lanes, grow[0], grow[1])
            states = _bdot(bm_t, xdd[:, lanes])
            own_rows = low_rows if g == 0 else jnp.logical_not(low_rows)
            hst[slab] = h_prev * jnp.where(low_lanes, cd[0], cd[1]) + jnp.where(own_rows, states, 0.0)
            ys.append(y_diag + y_off)
        y = jnp.concatenate(ys, axis=1)
        if d == 0:
            o_ref[rows, :] = y + dsk[...] * xs
        else:
            y_bwd[rows, :] = y

    def step(c, carry):
        chunk(c, 0)
        chunk(nc - 1 - c, 1)
        return carry

    lax.fori_loop(0, nc, step, 0, unroll=2)
    o_ref[...] += y_bwd[...]
    for slab in range(2 * SSM_PAIRS):
        g = (slab % SSM_PAIRS) // (SSM_PAIRS // SSM_GROUPS)
        st_t = hst[slab].T
        for hh in range(2):
            hfin_ref[0, 2 * slab + hh] = st_t[hh * SSM_HEAD_DIM:(hh + 1) * SSM_HEAD_DIM,
                                              g * SSM_STATE:(g + 1) * SSM_STATE]


def _ssd_call(s_xbc, s_dt, h0, cw, cb, dtb, alog, dsk, expand, nb, sl, shared_h0):
    seq = lambda b: (b, 0)
    st_shape = (1, 2 * SSM_PAIRS, 2 * SSM_STATE, PAIR_W)
    fin_shape = (1, 2 * SSM_HEADS, SSM_HEAD_DIM, SSM_STATE)
    h0_map =(lambda b: (0, 0, 0, 0)) if shared_h0 else (lambda b: (b, 0, 0, 0))
    return pl.pallas_call(
        functools.partial(_ssd_kernel, sl_len=sl),
        grid=(nb,),
        in_specs=[
            pl.BlockSpec((sl, XBC_W), seq), pl.BlockSpec((sl, LANES), seq), pl.BlockSpec(st_shape, h0_map),
            _resident((3, XBC_W)), _resident((1, XBC_W)), _resident((1, LANES)), _resident((1, LANES)),
            _resident((1, SSM_W)), _resident((2, LANES, SSM_W)),
        ],
        out_specs=[pl.BlockSpec((sl, SSM_W), seq), pl.BlockSpec(fin_shape, lambda b: (b, 0, 0, 0))],
        out_shape=[jax.ShapeDtypeStruct((nb * sl, SSM_W), F32), jax.ShapeDtypeStruct((nb, *fin_shape[1:]), F32)],
        scratch_shapes=[pltpu.VMEM((sl, XBC_W), F32), pltpu.VMEM((sl, LANES), F32), pltpu.VMEM((sl, LANES), F32),
                        pltpu.VMEM(st_shape[1:], F32), pltpu.VMEM((sl, SSM_W), F32)],
        compiler_params=_params(1),
        name="ssd_scan",
    )(s_xbc, s_dt, h0, cw, cb, dtb, alog, dsk, expand)


def _pack_states(h):
    nb = h.shape[0]
    t = h.reshape(nb, 2, SSM_PAIRS, 2, SSM_HEAD_DIM, SSM_STATE).transpose(0, 1, 2, 5, 3, 4)
    t = t.reshape(nb, 2, SSM_PAIRS, SSM_STATE, PAIR_W)
    z = jnp.zeros_like(t)
    first = (jnp.arange(SSM_PAIRS) < SSM_PAIRS // SSM_GROUPS)[None, None, :, None, None]
    slabs = jnp.where(first, jnp.concatenate([t, z], axis=3), jnp.concatenate([z, t], axis=3))
    return slabs.reshape(nb, 2 * SSM_PAIRS, 2 * SSM_STATE, PAIR_W)


def _expand_table():
    r = jnp.arange(LANES)[:, None]
    head = jnp.arange(SSM_W)[None, :] // SSM_HEAD_DIM
    return jnp.stack([r == head, r == SSM_HEADS + head]).astype(BF16)


def _attend_group(keys, q_t, vals_t, mask, sink_row, g):
    z = jnp.zeros_like(q_t)
    rhs = jnp.concatenate([q_t, z] if g == 0 else [z, q_t], axis=0)
    s = jnp.dot(keys, rhs, preferred_element_type=F32)
    if mask is not None:
        s = jnp.where(mask, s, -jnp.inf)
    m = jnp.maximum(jnp.max(s, axis=0, keepdims=True), sink_row)
    e = jnp.exp(s - m)
    denom = jnp.sum(e, axis=0, keepdims=True) + jnp.exp(sink_row - m)
    return jnp.dot(vals_t, e.astype(BF16), preferred_element_type=F32) / denom


def _ctx_attn_kernel(q_ref, k_ref, v_ref, sink_ref, o_ref, *, sl_len):
    scale = HEAD_DIM ** -0.5
    q_t = (q_ref[...] * scale).T.astype(BF16)
    keys = k_ref[...].astype(BF16)
    v_t = v_ref[...].T.astype(BF16)
    outs = []
    for g in range(KV_HEADS):
        q_g = jnp.concatenate(
            [q_t[(g * Q_PER_KV + hq) * HEAD_DIM:(g * Q_PER_KV + hq + 1) * HEAD_DIM, :] for hq in range(Q_PER_KV)], axis=1)
        o_t = _attend_group(keys, q_g, v_t[g * HEAD_DIM:(g + 1) * HEAD_DIM, :], None, sink_ref[g], g)
        outs += [o_t[:, hq * sl_len:(hq + 1) * sl_len] for hq in range(Q_PER_KV)]
    o_ref[...] = jnp.concatenate(outs, axis=0).T.astype(BF16)


def _ctx_attn_call(q, k, v, sink_rows, nb, sl):
    seq = lambda b: (b, 0)
    return pl.pallas_call(
        functools.partial(_ctx_attn_kernel, sl_len=sl),
        grid=(nb,),
        in_specs=[pl.BlockSpec((sl, ATT_W), seq), pl.BlockSpec((sl, KV_W), seq), pl.BlockSpec((sl, KV_W), seq),
                  _resident((KV_HEADS, 1, Q_PER_KV * sl))],
        out_specs=pl.BlockSpec((sl, ATT_W), seq),
        out_shape=jax.ShapeDtypeStruct((nb * sl, ATT_W), BF16),
        compiler_params=_params(1),
        name="ctx_attention",
    )(q, k, v, sink_rows)


def _rope(x, cos, sin_signed):
    w = x.shape[1]
    lane = lax.broadcasted_iota(jnp.int32, (1, w), 1)
    partner = jnp.where(lane % 32 < 16, pltpu.roll(x, w - 16, 1), pltpu.roll(x, 16, 1))
    return x * cos + partner * sin_signed


def _lat_attn_kernel(q_ref, k_ref, v_ref, ck_ref, cv_ref, cos_ref, sin_ref, sink_ref, o_ref, q_t, kcat, v_t, o_t, *,
                     sl_len, past):
    blk = ATT_BLOCK
    nblk = sl_len // blk
    scale = HEAD_DIM ** -0.5
    q_t[...] = (_rope(q_ref[...], cos_ref[...], sin_ref[...]) * scale).T.astype(BF16)
    kcat[0:past, :] = ck_ref[0].astype(BF16)
    kcat[past:, :] = _rope(k_ref[...], cos_ref[:, 0:KV_W], sin_ref[:, 0:KV_W]).astype(BF16)
    v_t[:, 0:past] = cv_ref[0].T.astype(BF16)
    v_t[:, past:] = v_ref[...].T.astype(BF16)
    for i in range(nblk):
        cols = slice(i * blk, (i + 1) * blk)
        lo, hi = max(i - 1, 0), min(i + 2, nblk)
        band = slice(past + lo * blk, past + hi * blk)
        n_keys = past + (hi - lo) * blk
        kk = lax.broadcasted_iota(jnp.int32, (n_keys, 1), 0)
        qi = lax.broadcasted_iota(jnp.int32, (1, Q_PER_KV * blk), 1) % blk
        rel = kk - past + (lo - i) * blk
        mask = (kk < past) | (jnp.abs(qi - rel) <= WINDOW)
        keys = jnp.concatenate([kcat[0:past, :], kcat[band, :]], axis=0)
        for g in range(KV_HEADS):
            drows = slice(g * HEAD_DIM, (g + 1) * HEAD_DIM)
            q_g = jnp.concatenate(
                [q_t[(g * Q_PER_KV + hq) * HEAD_DIM:(g * Q_PER_KV + hq + 1) * HEAD_DIM, cols] for hq in range(Q_PER_KV)],
                axis=1)
            vals_t = jnp.concatenate([v_t[drows, 0:past], v_t[drows, band]], axis=1)
            out = _attend_group(keys, q_g, vals_t, mask, sink_ref[g], g)
            for hq in range(Q_PER_KV):
                h = g * Q_PER_KV + hq
                o_t[h * HEAD_DIM:(h + 1) * HEAD_DIM, cols] = out[:, hq * blk:(hq + 1) * blk]
    o_ref[...] = o_t[...].T.astype(BF16)


def _lat_attn_call(q, k, v, ck, cv, cos, sin, sink_rows, nb, sl):
    seq = lambda b: (b, 0)
    past = ck.shape[1]
    cache = pl.BlockSpec((1, past, KV_W), lambda b: (b, 0, 0))
    return pl.pallas_call(
        functools.partial(_lat_attn_kernel, sl_len=sl, past=past),
        grid=(nb,),
        in_specs=[pl.BlockSpec((sl, ATT_W), seq), pl.BlockSpec((sl, KV_W), seq), pl.BlockSpec((sl, KV_W), seq),
                  cache, cache, _resident((sl, ATT_W)), _resident((sl, ATT_W)),
                  _resident((KV_HEADS, 1, Q_PER_KV * ATT_BLOCK))],
        out_specs=pl.BlockSpec((sl, ATT_W), seq),
        out_shape=jax.ShapeDtypeStruct((nb * sl, ATT_W), BF16),
        scratch_shapes=[pltpu.VMEM((ATT_W, sl), BF16), pltpu.VMEM((past + sl, KV_W), BF16),
                        pltpu.VMEM((KV_W, past + sl), BF16), pltpu.VMEM((ATT_W, sl), F32)],
        compiler_params=_params(1),
        name="latent_attention",
    )(q, k, v, ck, cv, cos, sin, sink_rows)


def _sink_rows(sink, n_q):
    return jnp.repeat(sink.reshape(KV_HEADS, Q_PER_KV), n_q, axis=1).reshape(KV_HEADS, 1, Q_PER_KV * n_q)


def _post_kernel(x_ref, hy_ref, sy_ref, sz_ref, at_ref, g_ref, mod_ref, snorm, npost, nfpre, nfpost,
                 wh, ws, wa, wo, wg, wu, wd, o_ref):
    m = mod_ref[0]
    ssm_y = _rms(sy_ref[...] * _silu(sz_ref[...]), snorm[...])
    merged = (jax.nn.sigmoid(g_ref[:, 0:D_MODEL]) * _bdot(hy_ref[...], wh[...])
              + jax.nn.sigmoid(g_ref[:, D_MODEL:2 * D_MODEL]) * _bdot(ssm_y, ws[...])
              + jax.nn.sigmoid(g_ref[:, 2 * D_MODEL:3 * D_MODEL]) * _bdot(at_ref[...], wa[...]))
    x1 = x_ref[...] + m[2:3, :] * _rms(_bdot(merged, wo[...]), npost[...])
    hb = (_rms(x1, nfpre[...]) * (1.0 + m[4:5, :]) + m[3:4, :]).astype(BF16)
    f = None
    for lo, hi in FFN_CHUNKS:
        gate = jnp.dot(hb, wg[:, lo:hi], preferred_element_type=F32)
        up = jnp.dot(hb, wu[:, lo:hi], preferred_element_type=F32)
        part = _bdot(_silu(gate) * up, wd[lo:hi, :])
        f = part if f is None else f + part
    o_ref[...] = x1 + m[5:6, :] * _rms(f, nfpost[...])


def _post_call(x, hy, sy, sz, at, g, mod, snorm, npost, nfpre, nfpost, wh, ws, wa, wo, wg, wu, wd, layer, nb, sl,
               shared_mod):
    n_tok = nb * sl
    assert n_tok % ROW_TILE == 0
    tok = lambda wdt: pl.BlockSpec((ROW_TILE, wdt), lambda i: (i, 0))
    vec = lambda wdt: _resident((1, wdt))
    return pl.pallas_call(
        _post_kernel,
        grid=(n_tok // ROW_TILE,),
        in_specs=[tok(D_MODEL), tok(HY_W), tok(SSM_W), tok(SSM_W), tok(ATT_W), tok(3 * D_MODEL),
                  _mod_spec(shared_mod, sl),
                  vec(SSM_W), vec(D_MODEL), vec(D_MODEL), vec(D_MODEL),
                  _resident((HY_W, D_MODEL), layer), _resident((SSM_W, D_MODEL), layer),
                  _resident((ATT_W, D_MODEL), layer), _resident((D_MODEL, D_MODEL), layer),
                  _resident((D_MODEL, FFN_HIDDEN), layer), _resident((D_MODEL, FFN_HIDDEN), layer),
                  _resident((FFN_HIDDEN, D_MODEL), layer)],
        out_specs=tok(D_MODEL),
        out_shape=jax.ShapeDtypeStruct((n_tok, D_MODEL), F32),
        compiler_params=_params(1),
        name="merge_ffn",
    )(x, hy, sy, sz, at, g, mod, snorm, npost, nfpre, nfpost, wh, ws, wa, wo, wg, wu, wd)


def _dft_mats(sl):
    radix = 32
    assert sl % radix == 0
    s = jnp.arange(sl, dtype=jnp.int32)

    def angles(f):
        return ((f[:, None] * s[None, :]) % (2 * sl)).astype(F32) * (math.pi / sl)

    a_hi = angles(jnp.arange(sl // radix, dtype=jnp.int32) * radix)[:, None, :]
    a_lo = angles(jnp.arange(radix, dtype=jnp.int32))[None, :, :]
    cos = (jnp.cos(a_hi) * jnp.cos(a_lo) - jnp.sin(a_hi) * jnp.sin(a_lo)).reshape(sl, sl)
    sin = (jnp.sin(a_hi) * jnp.cos(a_lo) + jnp.cos(a_hi) * jnp.sin(a_lo)).reshape(sl, sl)
    sign = jnp.where(s % 2 == 1, -1.0, 1.0).astype(F32)
    sinp = jnp.where(s[:, None] == 0, sign[None, :], sin).astype(BF16)
    return cos.astype(BF16), sinp, sinp.T


def _filter_features(sl):
    t = jnp.arange(sl, dtype=F32) / sl
    ang = 2.0 * jnp.pi * t[:, None] * jnp.arange(1, HY_BANDS + 1, dtype=F32)[None]
    feat = jnp.concatenate([t[:, None], jnp.cos(ang), jnp.sin(ang)], axis=-1)
    return jnp.pad(feat, ((0, 0), (0, LANES - HY_FEAT)))


def _rope_tables(sl):
    rows = sl // GRID_W
    row = jnp.repeat(jnp.arange(rows), GRID_W).astype(F32)
    col = jnp.tile(jnp.arange(GRID_W), rows).astype(F32)
    nq = HEAD_DIM // 4
    inv = ROPE_BASE ** (-jnp.arange(nq, dtype=F32) / nq)
    ar = row[:, None] * inv
    ac = col[:, None] * inv
    cos = jnp.concatenate([jnp.cos(ar), jnp.cos(ar), jnp.cos(ac), jnp.cos(ac)], axis=-1)
    sin = jnp.concatenate([-jnp.sin(ar), jnp.sin(ar), -jnp.sin(ac), jnp.sin(ac)], axis=-1)
    return jnp.tile(cos, (1, N_HEADS)), jnp.tile(sin, (1, N_HEADS))


def _pad_lanes(v):
    return jnp.pad(v.reshape(1, -1), ((0, 0), (0, LANES - v.size)))


def kernel(x_prompt, x_sample, c, cache_k, cache_v, state_ssm, c_ctx, ada_w, ada_b,
           norm_mix_pre, norm_mix_post, norm_ffn_pre, norm_ffn_post, w_in,
           hy_conv_w, hy_conv_b, hy_w1, hy_b1, hy_freq1, hy_w2, hy_b2, hy_freq2, hy_w3,
           hy_decay, hy_bias, ssm_conv_w, ssm_conv_b, ssm_a_log, ssm_dt_bias, ssm_d, ssm_norm,
           attn_sink, hy_proj, ssm_proj, attn_proj, w_out, ffn_w_gate, ffn_w_up, ffn_w_down):
    nb_c, sl_c, _ = x_prompt.shape
    nb_l, sl_l, _ = x_sample.shape
    past = cache_k.shape[2]

    cond = jnp.concatenate([c_ctx[None], c, jnp.zeros((8 - 1 - nb_l, D_MODEL), F32)], axis=0)
    mod = _ada_call(cond, ada_w, ada_b).reshape(DEPTH, 8, 6, D_MODEL)

    streams = {}
    for name, x in (("ctx", x_prompt), ("lat", x_sample)):
        nb, sl, _ = x.shape
        streams[name] = dict(nb=nb, sl=sl, x=x.reshape(nb * sl, D_MODEL), mats=_dft_mats(sl),
                             feat=_filter_features(sl))
    rope_cos, rope_sin = _rope_tables(sl_l)
    zero_state = jnp.zeros((1, 2 * SSM_PAIRS, 2 * SSM_STATE, PAIR_W), F32)
    expand = _expand_table()
    row = lambda v: v.reshape(1, -1)

    w_in_a = w_in[:, :, :IN_ALIGNED].astype(BF16)
    w_in_b = jnp.pad(w_in[:, :, IN_ALIGNED:], ((0, 0), (0, 0), (0, sum(IN_WIDTHS) - sum(IN_SPLIT)))).astype(BF16)
    wh, ws, wa, wo = (w.astype(BF16) for w in (hy_proj, ssm_proj, attn_proj, w_out))
    wg, wu, wd = (w.astype(BF16) for w in (ffn_w_gate, ffn_w_up, ffn_w_down))

    ks, vs, ss = [], [], []
    for l in range(DEPTH):
        w1p = jnp.pad(hy_w1[l], ((0, LANES - HY_FEAT), (0, 0)))
        dtb = _pad_lanes(ssm_dt_bias[l])
        alog = _pad_lanes(ssm_a_log[l])
        dsk = jnp.repeat(ssm_d[l, 0] + ssm_d[l, 1], SSM_HEAD_DIM).reshape(1, SSM_W)
        sink = {"ctx": _sink_rows(attn_sink[l], sl_c), "lat": _sink_rows(attn_sink[l], ATT_BLOCK)}
        for name in ("ctx", "lat"):
            st = streams[name]
            nb, sl, is_ctx = st["nb"], st["sl"], name == "ctx"
            m = mod[l, 0:1] if is_ctx else mod[l, 1:1 + nb]
            kre, kim = _filter_call(st["feat"], w1p, row(hy_b1[l]), row(hy_freq1[l]), hy_w2[l], row(hy_b2[l]),
                                    row(hy_freq2[l]), hy_w3[l], row(hy_decay[l]), st["mats"][0], st["mats"][1], sl)
            hy_in, s_z, s_xbc, s_dt, a_q, a_k, a_v, gate = _in_call(
                st["x"], m, row(norm_mix_pre[l]), w_in_a, w_in_b, l, nb, sl, is_ctx)
            hy_y = _hyena_call(hy_in, hy_conv_w[l], row(hy_conv_b[l]), hy_bias[l], kre, kim, st["mats"], nb, sl)
            h0 = zero_state if is_ctx else _pack_states(state_ssm[:, l])
            ssm_y, h_fin = _ssd_call(s_xbc, s_dt, h0, ssm_conv_w[l], row(ssm_conv_b[l]), dtb, alog, dsk, expand,
                                     nb, sl, is_ctx)
            if is_ctx:
                att = _ctx_attn_call(a_q, a_k, a_v, sink[name], nb, sl)
                ks.append(a_k.reshape(nb, sl, KV_HEADS, HEAD_DIM))
                vs.append(a_v.reshape(nb, sl, KV_HEADS, HEAD_DIM))
                ss.append(h_fin.reshape(nb, 2, SSM_HEADS, SSM_HEAD_DIM, SSM_STATE))
            else:
                att = _lat_attn_call(a_q, a_k, a_v, cache_k[:, l].reshape(nb, past, KV_W),
                                     cache_v[:, l].reshape(nb, past, KV_W), rope_cos, rope_sin, sink[name], nb, sl)
            st["x"] = _post_call(st["x"], hy_y, ssm_y, s_z, att, gate, m, row(ssm_norm[l]), row(norm_mix_post[l]),
                                 row(norm_ffn_pre[l]), row(norm_ffn_post[l]), wh, ws, wa, wo, wg, wu, wd, l, nb, sl,
                                 is_ctx)

    y_prompt = streams["ctx"]["x"].reshape(nb_c, sl_c, D_MODEL)
    y_sample = streams["lat"]["x"].reshape(nb_l, sl_l, D_MODEL)
    return (y_prompt, y_sample, jnp.stack(ks, axis=1), jnp.stack(vs, axis=1), jnp.stack(ss, axis=1))
```

```python
import functools
import math

import jax
import jax.numpy as jnp
from jax import lax
from jax.experimental import pallas as pl
from jax.experimental.pallas import tpu as pltpu

F32 = jnp.float32
BF16 = jnp.bfloat16

D_MODEL = 1024
DEPTH = 2
GRID_W = 64
HY_W = 512
HY_BANDS = 16
HY_FEAT = 1 + 2 * HY_BANDS
HY_HIDDEN = 64
SSM_W = 512
SSM_HEAD_DIM = 64
SSM_HEADS = 8
SSM_GROUPS = 2
SSM_STATE = 64
SSM_CHUNK = 128
XBC_W = SSM_W + 2 * SSM_GROUPS * SSM_STATE
N_HEADS = 8
KV_HEADS = 2
HEAD_DIM = 64
Q_PER_KV = N_HEADS // KV_HEADS
ATT_W = N_HEADS * HEAD_DIM
KV_W = KV_HEADS * HEAD_DIM
WINDOW = 128
ATT_BLOCK = 128
ROPE_BASE = 10000.0
FFN_HIDDEN = ((8 * D_MODEL + 3 * 256 - 1) // (3 * 256)) * 256
EPS = 1e-6

LANES = 128
ROW_TILE = 512
DFT_TILE = 256
MXU_TILE = 256
FFN_SPLIT = (FFN_HIDDEN // MXU_TILE + 1) // 2 * MXU_TILE
FFN_CHUNKS = ((0, FFN_SPLIT), (FFN_SPLIT, FFN_HIDDEN))
IN_SPLIT = (3 * HY_W, SSM_W, XBC_W, 2 * SSM_HEADS, ATT_W, KV_W, KV_W, 3 * D_MODEL)
IN_WIDTHS = (3 * HY_W, SSM_W, XBC_W, LANES, ATT_W, KV_W, KV_W, 3 * D_MODEL)
VMEM_LIMIT = 56 * 1024 * 1024


def _params(n_grid):
    return pltpu.CompilerParams(dimension_semantics=("arbitrary",) * n_grid, vmem_limit_bytes=VMEM_LIMIT)


def _resident(shape, layer=None):
    nd = len(shape)
    if layer is None:
        return pl.BlockSpec(shape, lambda *_: (0,) * nd, pipeline_mode=pl.Buffered(1))
    return pl.BlockSpec((None, *shape), lambda *_: (layer,) + (0,) * nd, pipeline_mode=pl.Buffered(1))


def _bdot(a, b):
    return jnp.dot(a.astype(BF16), b.astype(BF16), preferred_element_type=F32)


def _dot_nt(a, b):
    return lax.dot_general(a.astype(BF16), b.astype(BF16), (((1,), (1,)), ((), ())), preferred_element_type=F32)


def _dot_tn(a, b):
    return lax.dot_general(a.astype(BF16), b.astype(BF16), (((0,), (0,)), ((), ())), preferred_element_type=F32)


def _split2(x):
    hi = x.astype(BF16)
    lo = (x - hi.astype(F32)).astype(BF16)
    return hi, lo


def _split3(x):
    h1 = x.astype(BF16)
    r1 = x - h1.astype(F32)
    h2 = r1.astype(BF16)
    h3 = (r1 - h2.astype(F32)).astype(BF16)
    return h1, h2, h3


def _dot3(a, b):
    ah, al = _split2(a)
    bh, bl = _split2(b)
    return (jnp.dot(ah, bh, preferred_element_type=F32) + jnp.dot(ah, bl, preferred_element_type=F32)
            + jnp.dot(al, bh, preferred_element_type=F32))


def _rms(x, g):
    return x * lax.rsqrt(jnp.mean(x * x, axis=-1, keepdims=True) + EPS) * g


def _silu(x):
    return x * jax.nn.sigmoid(x)


def _dwconv(x, w, b):
    n = x.shape[0]
    row = lax.broadcasted_iota(jnp.int32, (n, 1), 0)
    prev = jnp.where(row == 0, 0.0, pltpu.roll(x, 1, 0))
    nxt = jnp.where(row == n - 1, 0.0, pltpu.roll(x, n - 1, 0))
    return prev * w[0:1, :] + x * w[1:2, :] + nxt * w[2:3, :] + b


def _ada_kernel(cond_ref, w_ref, b_ref, o_ref):
    o_ref[0] = _bdot(_silu(cond_ref[...]), w_ref[0]) + b_ref[0]


def _ada_call(cond, ada_w, ada_b):
    n = 6 * D_MODEL
    tn = n // 4
    rows = cond.shape[0]
    return pl.pallas_call(
        _ada_kernel,
        grid=(DEPTH, n // tn),
        in_specs=[
            pl.BlockSpec((rows, D_MODEL), lambda l, j: (0, 0)),
            pl.BlockSpec((1, D_MODEL, tn), lambda l, j: (l, 0, j)),
            pl.BlockSpec((1, 1, tn), lambda l, j: (l, 0, j)),
        ],
        out_specs=pl.BlockSpec((1, rows, tn), lambda l, j: (l, 0, j)),
        out_shape=jax.ShapeDtypeStruct((DEPTH, rows, n), F32),
        compiler_params=_params(2),
        name="adaln",
    )(cond, ada_w, ada_b.reshape(DEPTH, 1, n))


def _in_kernel(x_ref, mod_ref, g_ref, wt_ref, *o_refs):
    h = _rms(x_ref[...], g_ref[...]) * (1.0 + mod_ref[0, 1:2, :]) + mod_ref[0, 0:1, :]
    hb = h.astype(BF16)
    off = 0
    for o_ref, wd, padded in zip(o_refs, IN_SPLIT, IN_WIDTHS):
        out = lax.dot_general(hb, wt_ref[off:off + padded, :], (((1,), (1,)), ((), ())), preferred_element_type=F32)
        if padded != wd:
            out = jnp.where(lax.broadcasted_iota(jnp.int32, (1, padded), 1) < wd, out, 0.0)
        o_ref[...] = out
        off += wd


def _mod_spec(shared, sl):
    if shared:
        return pl.BlockSpec((1, 6, D_MODEL), lambda i: (0, 0, 0))
    assert sl % ROW_TILE == 0
    return pl.BlockSpec((1, 6, D_MODEL), lambda i: (i // (sl // ROW_TILE), 0, 0))


def _in_call(x, mod, g, wt, layer, nb, sl, shared_mod):
    n_tok = nb * sl
    assert n_tok % ROW_TILE == 0
    row = lambda i: (i, 0)
    return pl.pallas_call(
        _in_kernel,
        grid=(n_tok // ROW_TILE,),
        in_specs=[
            pl.BlockSpec((ROW_TILE, D_MODEL), row),
            _mod_spec(shared_mod, sl),
            _resident((1, D_MODEL)),
            _resident((sum(IN_SPLIT), D_MODEL), layer),
        ],
        out_specs=[pl.BlockSpec((ROW_TILE, wd), row) for wd in IN_WIDTHS],
        out_shape=[jax.ShapeDtypeStruct((n_tok, wd), F32) for wd in IN_WIDTHS],
        compiler_params=_params(1),
        name="in_proj",
    )(x, mod, g, wt)


def _filter_kernel(feat_ref, w1, b1, f1, w2, b2, f2, w3, dec, cos_ref, sin_ref, kre_ref, kim_ref, *, sl_len):
    feat = feat_ref[...]
    h = jnp.sin(f1[...] * (_dot3(feat, w1[...]) + b1[...]))
    h = jnp.sin(f2[...] * (_dot3(h, w2[...]) + b2[...]))
    t = feat[:, 0:1]
    filt = _dot3(h, w3[...]) * jnp.exp(-t * dec[...])
    row = lax.broadcasted_iota(jnp.int32, (sl_len, 1), 0)
    hf = jnp.concatenate([filt[:, 0:HY_W], filt[:, 2 * HY_W:3 * HY_W]], axis=1)
    hb = jnp.where(row == 0, 0.0, jnp.concatenate([filt[:, HY_W:2 * HY_W], filt[:, 3 * HY_W:4 * HY_W]], axis=1))
    ksum = hf + hb
    kre = _bdot(cos_ref[...], ksum)
    kim = _bdot(sin_ref[...], hb - hf)
    nyq = jnp.sum(jnp.where(row % 2 == 1, -ksum, ksum), axis=0, keepdims=True)
    scale = jnp.where(row == 0, 0.5 / sl_len, 1.0 / sl_len)
    kre = kre * scale
    kim = jnp.where(row == 0, nyq, kim) * scale
    for o in range(2):
        kre_ref[o] = kre[:, o * HY_W:(o + 1) * HY_W]
        kim_ref[o] = kim[:, o * HY_W:(o + 1) * HY_W]


def _filter_call(feat, w1, b1, f1, w2, b2, f2, w3, dec, cos, sinp, sl):
    hid = HY_HIDDEN
    return pl.pallas_call(
        functools.partial(_filter_kernel, sl_len=sl),
        grid=(1,),
        in_specs=[
            _resident((sl, LANES)), _resident((LANES, hid)), _resident((1, hid)), _resident((1, hid)),
            _resident((hid, hid)), _resident((1, hid)), _resident((1, hid)),
            _resident((hid, 4 * HY_W)), _resident((1, 4 * HY_W)),
            _resident((sl, sl)), _resident((sl, sl)),
        ],
        out_specs=[pl.BlockSpec((2, sl, HY_W), lambda i: (0, 0, 0))] * 2,
        out_shape=[jax.ShapeDtypeStruct((2, sl, HY_W), F32)] * 2,
        compiler_params=_params(1),
        name="hyena_filter",
    )(feat, w1, b1, f1, w2, b2, f2, w3, dec, cos, sinp)


def _hyena_kernel(x_ref, cw, cb, bias, kre, kim, cos, sin, sin_t, o_ref, hc, u, pre_s, pim_s, *, sl_len):
    tile = min(DFT_TILE, sl_len)
    nt = sl_len // tile
    hc[...] = _dwconv(x_ref[...], cw[...], cb[...])
    u[...] = hc[:, 0:HY_W].astype(BF16)

    def conv(order, epilogue):
        def fwd(i, carry):
            r = pl.multiple_of(i * tile, tile)
            rows = pl.ds(r, tile)
            a = jnp.dot(cos[rows, :], u[...], preferred_element_type=F32)
            b = jnp.dot(sin[rows, :], u[...], preferred_element_type=F32)
            kr = kre[order, rows, :]
            ki = kim[order, rows, :]
            first = (r + lax.broadcasted_iota(jnp.int32, (tile, 1), 0)) == 0
            pre = jnp.where(first, a * kr, a * kr + b * ki)
            pim = jnp.where(first, b * ki, b * kr - a * ki)
            pre_s[rows, :] = pre.astype(BF16)
            pim_s[rows, :] = pim.astype(BF16)
            return carry

        lax.fori_loop(0, nt, fwd, 0, unroll=True)

        def inv(i, carry):
            r = pl.multiple_of(i * tile, tile)
            rows = pl.ds(r, tile)
            y = (jnp.dot(cos[rows, :], pre_s[...], preferred_element_type=F32)
                 + jnp.dot(sin_t[rows, :], pim_s[...], preferred_element_type=F32))
            epilogue(rows, y)
            return carry

        lax.fori_loop(0, nt, inv, 0, unroll=True)

    def after_first(rows, y):
        z = hc[rows, HY_W:2 * HY_W] * (y + hc[rows, 0:HY_W] * bias[0:1, :])
        hc[rows, 0:HY_W] = z
        u[rows, :] = z.astype(BF16)

    def after_second(rows, y):
        o_ref[rows, :] = (hc[rows, 2 * HY_W:3 * HY_W] * (y + hc[rows, 0:HY_W] * bias[1:2, :])).astype(BF16)

    conv(0, after_first)
    conv(1, after_second)


def _hyena_call(hy_in, cw, cb, bias, kre, kim, mats, nb, sl):
    seq = lambda b: (b, 0)
    mat = _resident((sl, sl))
    return pl.pallas_call(
        functools.partial(_hyena_kernel, sl_len=sl),
        grid=(nb,),
        in_specs=[
            pl.BlockSpec((sl, 3 * HY_W), seq),
            _resident((3, 3 * HY_W)), _resident((1, 3 * HY_W)), _resident((2, HY_W)),
            _resident((2, sl, HY_W)), _resident((2, sl, HY_W)),
            mat, mat, mat,
        ],
        out_specs=pl.BlockSpec((sl, HY_W), seq),
        out_shape=jax.ShapeDtypeStruct((nb * sl, HY_W), BF16),
        scratch_shapes=[pltpu.VMEM((sl, 3 * HY_W), F32)] + [pltpu.VMEM((sl, HY_W), BF16)] * 3,
        compiler_params=_params(1),
        name="hyena_conv",
    )(hy_in, cw, cb, bias, kre, kim, *mats)


SSM_PAIRS = SSM_HEADS // 2
PAIR_W = 2 * SSM_HEAD_DIM


def _ssd_kernel(xbc_ref, dt_ref, h0_ref, cw, cb, dtb, alog, dsk, expand, o_ref, hfin_ref, xbc, dt, dta, hst, y_bwd, *,
                sl_len):
    nc = sl_len // SSM_CHUNK
    q = SSM_CHUNK
    xbc[...] = _silu(_dwconv(xbc_ref[...], cw[...], cb[...]))
    dtv = jax.nn.softplus(dt_ref[...] + dtb[...])
    dt[...] = dtv
    dta[...] = dtv * (-jnp.exp(alog[...]))
    hst[...] = h0_ref[0]
    ii = lax.broadcasted_iota(jnp.int32, (q, q), 0)
    jj = lax.broadcasted_iota(jnp.int32, (q, q), 1)
    low_lanes = lax.broadcasted_iota(jnp.int32, (1, PAIR_W), 1) < SSM_HEAD_DIM
    low_rows = lax.broadcasted_iota(jnp.int32, (2 * SSM_STATE, 1), 0) < SSM_STATE

    def chunk(c, d):
        r = pl.multiple_of(c * q, q)
        rows = pl.ds(r, q)
        xs = xbc[rows, 0:SSM_W]
        bm = xbc[rows, SSM_W:SSM_W + SSM_GROUPS * SSM_STATE]
        cm = xbc[rows, SSM_W + SSM_GROUPS * SSM_STATE:XBC_W]
        tri = (jj <= ii) if d == 0 else (jj >= ii)
        trib = tri.astype(BF16)
        a1, a2, a3 = _split3(dta[rows, :])
        acum = (jnp.dot(trib, a1, preferred_element_type=F32) + jnp.dot(trib, a2, preferred_element_type=F32)
                + jnp.dot(trib, a3, preferred_element_type=F32))
        acum_t = acum.T
        dtc = dt[rows, :]
        dt_t = dtc.T
        last = acum[q - 1:q, :] if d == 0 else acum[0:1, :]
        cdec = jnp.exp(last)
        w1, w2 = _split2(dtc * jnp.exp(last - acum))
        w_wide = (jnp.dot(w1, expand[d], preferred_element_type=F32)
                  + jnp.dot(w2, expand[d], preferred_element_type=F32))
        xdd = xs * w_wide
        bm_t = bm.T.astype(BF16)
        cb_t = [_bdot(jnp.where(low_lanes, cm, 0.0), bm_t), _bdot(jnp.where(low_lanes, 0.0, cm), bm_t)]
        ys = []
        for p in range(SSM_PAIRS):
            g = p // (SSM_PAIRS // SSM_GROUPS)
            lanes = slice(p * PAIR_W, (p + 1) * PAIR_W)
            scores, grow, cd = [], [], []
            for hh in range(2):
                col = d * SSM_HEADS + 2 * p + hh
                a_col = jnp.broadcast_to(acum[:, col:col + 1], (q, q))
                lmat = jnp.exp(jnp.where(tri, a_col - acum_t[col:col + 1, :], -jnp.inf))
                scores.append((cb_t[g] * lmat * dt_t[col:col + 1, :]).astype(BF16))
                grow.append(jnp.exp(a_col))
                cd.append(cdec[:, col:col + 1])
            xp = xs[:, lanes]
            x_blocks = jnp.concatenate([jnp.where(low_lanes, xp, 0.0), jnp.where(low_lanes, 0.0, xp)], axis=0)
            y_diag = jnp.dot(jnp.concatenate(scores, axis=1), x_blocks.astype(BF16), preferred_element_type=F32)
            slab = d * SSM_PAIRS + p
            h_prev = hst[slab]
            y_off = _bdot(cm, h_prev) * jnp.where(low_lanes, grow[0], grow[1])
            states = _bdot(bm_t, xdd[:, lanes])
            own_rows = low_rows if g == 0 else jnp.logical_not(low_rows)
            hst[slab] = h_prev * jnp.where(low_lanes, cd[0], cd[1]) + jnp.where(own_rows, states, 0.0)
            ys.append(y_diag + y_off)
        y = jnp.concatenate(ys, axis=1)
        if d == 0:
            o_ref[rows, :] = y + dsk[...] * xs
        else:
            y_bwd[rows, :] = y

    def step(c, carry):
        chunk(c, 0)
        chunk(nc - 1 - c, 1)
        return carry

    lax.fori_loop(0, nc, step, 0, unroll=2)
    o_ref[...] += y_bwd[...]
    for slab in range(2 * SSM_PAIRS):
        g = (slab % SSM_PAIRS) // (SSM_PAIRS // SSM_GROUPS)
        st_t = hst[slab].T
        for hh in range(2):
            hfin_ref[0, 2 * slab + hh] = st_t[hh * SSM_HEAD_DIM:(hh + 1) * SSM_HEAD_DIM,
                                              g * SSM_STATE:(g + 1) * SSM_STATE]


def _ssd_call(s_xbc, s_dt, h0, cw, cb, dtb, alog, dsk, expand, nb, sl, shared_h0):
    seq = lambda b: (b, 0)
    st_shape = (1, 2 * SSM_PAIRS, 2 * SSM_STATE, PAIR_W)
    fin_shape = (1, 2 * SSM_HEADS, SSM_HEAD_DIM, SSM_STATE)
    h0_map =(lambda b: (0, 0, 0, 0)) if shared_h0 else (lambda b: (b, 0, 0, 0))
    return pl.pallas_call(
        functools.partial(_ssd_kernel, sl_len=sl),
        grid=(nb,),
        in_specs=[
            pl.BlockSpec((sl, XBC_W), seq), pl.BlockSpec((sl, LANES), seq), pl.BlockSpec(st_shape, h0_map),
            _resident((3, XBC_W)), _resident((1, XBC_W)), _resident((1, LANES)), _resident((1, LANES)),
            _resident((1, SSM_W)), _resident((2, LANES, SSM_W)),
        ],
        out_specs=[pl.BlockSpec((sl, SSM_W), seq), pl.BlockSpec(fin_shape, lambda b: (b, 0, 0, 0))],
        out_shape=[jax.ShapeDtypeStruct((nb * sl, SSM_W), F32), jax.ShapeDtypeStruct((nb, *fin_shape[1:]), F32)],
        scratch_shapes=[pltpu.VMEM((sl, XBC_W), F32), pltpu.VMEM((sl, LANES), F32), pltpu.VMEM((sl, LANES), F32),
                        pltpu.VMEM(st_shape[1:], F32), pltpu.VMEM((sl, SSM_W), F32)],
        compiler_params=_params(1),
        name="ssd_scan",
    )(s_xbc, s_dt, h0, cw, cb, dtb, alog, dsk, expand)


def _pack_states(h):
    nb = h.shape[0]
    t = h.reshape(nb, 2, SSM_PAIRS, 2, SSM_HEAD_DIM, SSM_STATE).transpose(0, 1, 2, 5, 3, 4)
    t = t.reshape(nb, 2, SSM_PAIRS, SSM_STATE, PAIR_W)
    z = jnp.zeros_like(t)
    first = (jnp.arange(SSM_PAIRS) < SSM_PAIRS // SSM_GROUPS)[None, None, :, None, None]
    slabs = jnp.where(first, jnp.concatenate([t, z], axis=3), jnp.concatenate([z, t], axis=3))
    return slabs.reshape(nb, 2 * SSM_PAIRS, 2 * SSM_STATE, PAIR_W)


def _expand_table():
    r = jnp.arange(LANES)[:, None]
    head = jnp.arange(SSM_W)[None, :] // SSM_HEAD_DIM
    return jnp.stack([r == head, r == SSM_HEADS + head]).astype(BF16)


def _attend_group(keys, q_t, vals_t, mask, sink_row, g):
    z = jnp.zeros_like(q_t)
    rhs = jnp.concatenate([q_t, z] if g == 0 else [z, q_t], axis=0)
    s = jnp.dot(keys, rhs, preferred_element_type=F32)
    if mask is not None:
        s = jnp.where(mask, s, -jnp.inf)
    m = jnp.maximum(jnp.max(s, axis=0, keepdims=True), sink_row)
    e = jnp.exp(s - m)
    denom = jnp.sum(e, axis=0, keepdims=True) + jnp.exp(sink_row - m)
    return jnp.dot(vals_t, e.astype(BF16), preferred_element_type=F32) / denom


def _ctx_attn_kernel(q_ref, k_ref, v_ref, sink_ref, o_ref, *, sl_len):
    scale = HEAD_DIM ** -0.5
    q_t = (q_ref[...] * scale).T.astype(BF16)
    keys = k_ref[...].astype(BF16)
    v_t = v_ref[...].T.astype(BF16)
    outs = []
    for g in range(KV_HEADS):
        q_g = jnp.concatenate(
            [q_t[(g * Q_PER_KV + hq) * HEAD_DIM:(g * Q_PER_KV + hq + 1) * HEAD_DIM, :] for hq in range(Q_PER_KV)], axis=1)
        o_t = _attend_group(keys, q_g, v_t[g * HEAD_DIM:(g + 1) * HEAD_DIM, :], None, sink_ref[g], g)
        outs += [o_t[:, hq * sl_len:(hq + 1) * sl_len] for hq in range(Q_PER_KV)]
    o_ref[...] = jnp.concatenate(outs, axis=0).T.astype(BF16)


def _ctx_attn_call(q, k, v, sink_rows, nb, sl):
    seq = lambda b: (b, 0)
    return pl.pallas_call(
        functools.partial(_ctx_attn_kernel, sl_len=sl),
        grid=(nb,),
        in_specs=[pl.BlockSpec((sl, ATT_W), seq), pl.BlockSpec((sl, KV_W), seq), pl.BlockSpec((sl, KV_W), seq),
                  _resident((KV_HEADS, 1, Q_PER_KV * sl))],
        out_specs=pl.BlockSpec((sl, ATT_W), seq),
        out_shape=jax.ShapeDtypeStruct((nb * sl, ATT_W), BF16),
        compiler_params=_params(1),
        name="ctx_attention",
    )(q, k, v, sink_rows)


def _rope(x, cos, sin_signed):
    w = x.shape[1]
    lane = lax.broadcasted_iota(jnp.int32, (1, w), 1)
    partner = jnp.where(lane % 32 < 16, pltpu.roll(x, w - 16, 1), pltpu.roll(x, 16, 1))
    return x * cos + partner * sin_signed


def _lat_attn_kernel(q_ref, k_ref, v_ref, ck_ref, cv_ref, cos_ref, sin_ref, sink_ref, o_ref, q_t, kcat, v_t, o_t, *,
                     sl_len, past):
    blk = ATT_BLOCK
    nblk = sl_len // blk
    scale = HEAD_DIM ** -0.5
    q_t[...] = (_rope(q_ref[...], cos_ref[...], sin_ref[...]) * scale).T.astype(BF16)
    kcat[0:past, :] = ck_ref[0].astype(BF16)
    kcat[past:, :] = _rope(k_ref[...], cos_ref[:, 0:KV_W], sin_ref[:, 0:KV_W]).astype(BF16)
    v_t[:, 0:past] = cv_ref[0].T.astype(BF16)
    v_t[:, past:] = v_ref[...].T.astype(BF16)
    for i in range(nblk):
        cols = slice(i * blk, (i + 1) * blk)
        lo, hi = max(i - 1, 0), min(i + 2, nblk)
        band = slice(past + lo * blk, past + hi * blk)
        n_keys = past + (hi - lo) * blk
        kk = lax.broadcasted_iota(jnp.int32, (n_keys, 1), 0)
        qi = lax.broadcasted_iota(jnp.int32, (1, Q_PER_KV * blk), 1) % blk
        rel = kk - past + (lo - i) * blk
        mask = (kk < past) | (jnp.abs(qi - rel) <= WINDOW)
        keys = jnp.concatenate([kcat[0:past, :], kcat[band, :]], axis=0)
        for g in range(KV_HEADS):
            drows = slice(g * HEAD_DIM, (g + 1) * HEAD_DIM)
            q_g = jnp.concatenate(
                [q_t[(g * Q_PER_KV + hq) * HEAD_DIM:(g * Q_PER_KV + hq + 1) * HEAD_DIM, cols] for hq in range(Q_PER_KV)],
                axis=1)
            vals_t = jnp.concatenate([v_t[drows, 0:past], v_t[drows, band]], axis=1)
            out = _attend_group(keys, q_g, vals_t, mask, sink_ref[g], g)
            for hq in range(Q_PER_KV):
                h = g * Q_PER_KV + hq
                o_t[h * HEAD_DIM:(h + 1) * HEAD_DIM, cols] = out[:, hq * blk:(hq + 1) * blk]
    o_ref[...] = o_t[...].T.astype(BF16)


def _lat_attn_call(q, k, v, ck, cv, cos, sin, sink_rows, nb, sl):
    seq = lambda b: (b, 0)
    past = ck.shape[1]
    cache = pl.BlockSpec((1, past, KV_W), lambda b: (b, 0, 0))
    return pl.pallas_call(
        functools.partial(_lat_attn_kernel, sl_len=sl, past=past),
        grid=(nb,),
        in_specs=[pl.BlockSpec((sl, ATT_W), seq), pl.BlockSpec((sl, KV_W), seq), pl.BlockSpec((sl, KV_W), seq),
                  cache, cache, _resident((sl, ATT_W)), _resident((sl, ATT_W)),
                  _resident((KV_HEADS, 1, Q_PER_KV * ATT_BLOCK))],
        out_specs=pl.BlockSpec((sl, ATT_W), seq),
        out_shape=jax.ShapeDtypeStruct((nb * sl, ATT_W), BF16),
        scratch_shapes=[pltpu.VMEM((ATT_W, sl), BF16), pltpu.VMEM((past + sl, KV_W), BF16),
                        pltpu.VMEM((KV_W, past + sl), BF16), pltpu.VMEM((ATT_W, sl), F32)],
        compiler_params=_params(1),
        name="latent_attention",
    )(q, k, v, ck, cv, cos, sin, sink_rows)


def _sink_rows(sink, n_q):
    return jnp.repeat(sink.reshape(KV_HEADS, Q_PER_KV), n_q, axis=1).reshape(KV_HEADS, 1, Q_PER_KV * n_q)


def _post_kernel(x_ref, hy_ref, sy_ref, sz_ref, at_ref, g_ref, mod_ref, snorm, npost, nfpre, nfpost,
                 wh, ws, wa, wo, wg, wu, wd, o_ref):
    m = mod_ref[0]
    ssm_y = _rms(sy_ref[...] * _silu(sz_ref[...]), snorm[...])
    merged = (jax.nn.sigmoid(g_ref[:, 0:D_MODEL]) * _bdot(hy_ref[...], wh[...])
              + jax.nn.sigmoid(g_ref[:, D_MODEL:2 * D_MODEL]) * _bdot(ssm_y, ws[...])
              + jax.nn.sigmoid(g_ref[:, 2 * D_MODEL:3 * D_MODEL]) * _bdot(at_ref[...], wa[...]))
    x1 = x_ref[...] + m[2:3, :] * _rms(_bdot(merged, wo[...]), npost[...])
    hb = (_rms(x1, nfpre[...]) * (1.0 + m[4:5, :]) + m[3:4, :]).astype(BF16)
    f = None
    for lo, hi in FFN_CHUNKS:
        gate = jnp.dot(hb, wg[:, lo:hi], preferred_element_type=F32)
        up = jnp.dot(hb, wu[:, lo:hi], preferred_element_type=F32)
        part = _bdot(_silu(gate) * up, wd[lo:hi, :])
        f = part if f is None else f + part
    o_ref[...] = x1 + m[5:6, :] * _rms(f, nfpost[...])


def _post_call(x, hy, sy, sz, at, g, mod, snorm, npost, nfpre, nfpost, wh, ws, wa, wo, wg, wu, wd, layer, nb, sl,
               shared_mod):
    n_tok = nb * sl
    assert n_tok % ROW_TILE == 0
    tok = lambda wdt: pl.BlockSpec((ROW_TILE, wdt), lambda i: (i, 0))
    vec = lambda wdt: _resident((1, wdt))
    return pl.pallas_call(
        _post_kernel,
        grid=(n_tok // ROW_TILE,),
        in_specs=[tok(D_MODEL), tok(HY_W), tok(SSM_W), tok(SSM_W), tok(ATT_W), tok(3 * D_MODEL),
                  _mod_spec(shared_mod, sl),
                  vec(SSM_W), vec(D_MODEL), vec(D_MODEL), vec(D_MODEL),
                  _resident((HY_W, D_MODEL), layer), _resident((SSM_W, D_MODEL), layer),
                  _resident((ATT_W, D_MODEL), layer), _resident((D_MODEL, D_MODEL), layer),
                  _resident((D_MODEL, FFN_HIDDEN), layer), _resident((D_MODEL, FFN_HIDDEN), layer),
                  _resident((FFN_HIDDEN, D_MODEL), layer)],
        out_specs=tok(D_MODEL),
        out_shape=jax.ShapeDtypeStruct((n_tok, D_MODEL), F32),
        compiler_params=_params(1),
        name="merge_ffn",
    )(x, hy, sy, sz, at, g, mod, snorm, npost, nfpre, nfpost, wh, ws, wa, wo, wg, wu, wd)


def _dft_mats(sl):
    radix = 32
    assert sl % radix == 0
    s = jnp.arange(sl, dtype=jnp.int32)

    def angles(f):
        return ((f[:, None] * s[None, :]) % (2 * sl)).astype(F32) * (math.pi / sl)

    a_hi = angles(jnp.arange(sl // radix, dtype=jnp.int32) * radix)[:, None, :]
    a_lo = angles(jnp.arange(radix, dtype=jnp.int32))[None, :, :]
    cos = (jnp.cos(a_hi) * jnp.cos(a_lo) - jnp.sin(a_hi) * jnp.sin(a_lo)).reshape(sl, sl)
    sin = (jnp.sin(a_hi) * jnp.cos(a_lo) + jnp.cos(a_hi) * jnp.sin(a_lo)).reshape(sl, sl)
    sign = jnp.where(s % 2 == 1, -1.0, 1.0).astype(F32)
    sinp = jnp.where(s[:, None] == 0, sign[None, :], sin).astype(BF16)
    return cos.astype(BF16), sinp, sinp.T


def _filter_features(sl):
    t = jnp.arange(sl, dtype=F32) / sl
    ang = 2.0 * jnp.pi * t[:, None] * jnp.arange(1, HY_BANDS + 1, dtype=F32)[None]
    feat = jnp.concatenate([t[:, None], jnp.cos(ang), jnp.sin(ang)], axis=-1)
    return jnp.pad(feat, ((0, 0), (0, LANES - HY_FEAT)))


def _rope_tables(sl):
    rows = sl // GRID_W
    row = jnp.repeat(jnp.arange(rows), GRID_W).astype(F32)
    col = jnp.tile(jnp.arange(GRID_W), rows).astype(F32)
    nq = HEAD_DIM // 4
    inv = ROPE_BASE ** (-jnp.arange(nq, dtype=F32) / nq)
    ar = row[:, None] * inv
    ac = col[:, None] * inv
    cos = jnp.concatenate([jnp.cos(ar), jnp.cos(ar), jnp.cos(ac), jnp.cos(ac)], axis=-1)
    sin = jnp.concatenate([-jnp.sin(ar), jnp.sin(ar), -jnp.sin(ac), jnp.sin(ac)], axis=-1)
    return jnp.tile(cos, (1, N_HEADS)), jnp.tile(sin, (1, N_HEADS))


def _pad_lanes(v):
    return jnp.pad(v.reshape(1, -1), ((0, 0), (0, LANES - v.size)))


def kernel(x_prompt, x_sample, c, cache_k, cache_v, state_ssm, c_ctx, ada_w, ada_b,
           norm_mix_pre, norm_mix_post, norm_ffn_pre, norm_ffn_post, w_in,
           hy_conv_w, hy_conv_b, hy_w1, hy_b1, hy_freq1, hy_w2, hy_b2, hy_freq2, hy_w3,
           hy_decay, hy_bias, ssm_conv_w, ssm_conv_b, ssm_a_log, ssm_dt_bias, ssm_d, ssm_norm,
           attn_sink, hy_proj, ssm_proj, attn_proj, w_out, ffn_w_gate, ffn_w_up, ffn_w_down):
    nb_c, sl_c, _ = x_prompt.shape
    nb_l, sl_l, _ = x_sample.shape
    past = cache_k.shape[2]

    cond = jnp.concatenate([c_ctx[None], c, jnp.zeros((8 - 1 - nb_l, D_MODEL), F32)], axis=0)
    mod = _ada_call(cond, ada_w, ada_b).reshape(DEPTH, 8, 6, D_MODEL)

    streams = {}
    for name, x in (("ctx", x_prompt), ("lat", x_sample)):
        nb, sl, _ = x.shape
        streams[name] = dict(nb=nb, sl=sl, x=x.reshape(nb * sl, D_MODEL), mats=_dft_mats(sl),
                             feat=_filter_features(sl))
    rope_cos, rope_sin = _rope_tables(sl_l)
    zero_state = jnp.zeros((1, 2 * SSM_PAIRS, 2 * SSM_STATE, PAIR_W), F32)
    expand = _expand_table()
    row = lambda v: v.reshape(1, -1)

    w_in_t = jnp.swapaxes(w_in, 1, 2).astype(BF16)
    wh, ws, wa, wo = (w.astype(BF16) for w in (hy_proj, ssm_proj, attn_proj, w_out))
    wg, wu, wd = (w.astype(BF16) for w in (ffn_w_gate, ffn_w_up, ffn_w_down))

    ks, vs, ss = [], [], []
    for l in range(DEPTH):
        w1p = jnp.pad(hy_w1[l], ((0, LANES - HY_FEAT), (0, 0)))
        dtb = _pad_lanes(ssm_dt_bias[l])
        alog = _pad_lanes(ssm_a_log[l])
        dsk = jnp.repeat(ssm_d[l, 0] + ssm_d[l, 1], SSM_HEAD_DIM).reshape(1, SSM_W)
        sink = {"ctx": _sink_rows(attn_sink[l], sl_c), "lat": _sink_rows(attn_sink[l], ATT_BLOCK)}
        for name in ("ctx", "lat"):
            st = streams[name]
            nb, sl, is_ctx = st["nb"], st["sl"], name == "ctx"
            m = mod[l, 0:1] if is_ctx else mod[l, 1:1 + nb]
            kre, kim = _filter_call(st["feat"], w1p, row(hy_b1[l]), row(hy_freq1[l]), hy_w2[l], row(hy_b2[l]),
                                    row(hy_freq2[l]), hy_w3[l], row(hy_decay[l]), st["mats"][0], st["mats"][1], sl)
            hy_in, s_z, s_xbc, s_dt, a_q, a_k, a_v, gate = _in_call(
                st["x"], m, row(norm_mix_pre[l]), w_in_t, l, nb, sl, is_ctx)
            hy_y = _hyena_call(hy_in, hy_conv_w[l], row(hy_conv_b[l]), hy_bias[l], kre, kim, st["mats"], nb, sl)
            h0 = zero_state if is_ctx else _pack_states(state_ssm[:, l])
            ssm_y, h_fin = _ssd_call(s_xbc, s_dt, h0, ssm_conv_w[l], row(ssm_conv_b[l]), dtb, alog, dsk, expand,
                                     nb, sl, is_ctx)
            if is_ctx:
                att = _ctx_attn_call(a_q, a_k, a_v, sink[name], nb, sl)
                ks.append(a_k.reshape(nb, sl, KV_HEADS, HEAD_DIM))
                vs.append(a_v.reshape(nb, sl, KV_HEADS, HEAD_DIM))
                ss.append(h_fin.reshape(nb, 2, SSM_HEADS, SSM_HEAD_DIM, SSM_STATE))
            else:
                att = _lat_attn_call(a_q, a_k, a_v, cache_k[:, l].reshape(nb, past, KV_W),
                                     cache_v[:, l].reshape(nb, past, KV_W), rope_cos, rope_sin, sink[name], nb, sl)
            st["x"] = _post_call(st["x"], hy_y, ssm_y, s_z, att, gate, m, row(ssm_norm[l]), row(norm_mix_post[l]),
                                 row(norm_ffn_pre[l]), row(norm_ffn_post[l]), wh, ws, wa, wo, wg, wu, wd, l, nb, sl,
                                 is_ctx)

    y_prompt = streams["ctx"]["x"].reshape(nb_c, sl_c, D_MODEL)
    y_sample = streams["lat"]["x"].reshape(nb_l, sl_l, D_MODEL)
    return (y_prompt, y_sample, jnp.stack(ks, axis=1), jnp.stack(vs, axis=1), jnp.stack(ss, axis=1))
```

```python
import functools
import math

import jax
import jax.numpy as jnp
from jax import lax
from jax.experimental import pallas as pl
from jax.experimental.pallas import tpu as pltpu

F32 = jnp.float32
BF16 = jnp.bfloat16

D_MODEL = 1024
DEPTH = 2
GRID_W = 64
HY_W = 512
HY_BANDS = 16
HY_FEAT = 1 + 2 * HY_BANDS
HY_HIDDEN = 64
SSM_W = 512
SSM_HEAD_DIM = 64
SSM_HEADS = 8
SSM_GROUPS = 2
SSM_STATE = 64
SSM_CHUNK = 128
XBC_W = SSM_W + 2 * SSM_GROUPS * SSM_STATE
N_HEADS = 8
KV_HEADS = 2
HEAD_DIM = 64
Q_PER_KV = N_HEADS // KV_HEADS
ATT_W = N_HEADS * HEAD_DIM
KV_W = KV_HEADS * HEAD_DIM
WINDOW = 128
ATT_BLOCK = 128
ROPE_BASE = 10000.0
FFN_HIDDEN = ((8 * D_MODEL + 3 * 256 - 1) // (3 * 256)) * 256
EPS = 1e-6

LANES = 128
ROW_TILE = 512
DFT_TILE = 256
MXU_TILE = 256
FFN_SPLIT = (FFN_HIDDEN // MXU_TILE + 1) // 2 * MXU_TILE
FFN_CHUNKS = ((0, FFN_SPLIT), (FFN_SPLIT, FFN_HIDDEN))
IN_SPLIT = (3 * HY_W, SSM_W, XBC_W, 2 * SSM_HEADS, ATT_W, KV_W, KV_W, 3 * D_MODEL)
IN_WIDTHS = (3 * HY_W, SSM_W, XBC_W, LANES, ATT_W, KV_W, KV_W, 3 * D_MODEL)
VMEM_LIMIT = 56 * 1024 * 1024


def _params(n_grid):
    return pltpu.CompilerParams(dimension_semantics=("arbitrary",) * n_grid, vmem_limit_bytes=VMEM_LIMIT)


def _resident(shape, layer=None):
    nd = len(shape)
    if layer is None:
        return pl.BlockSpec(shape, lambda *_: (0,) * nd, pipeline_mode=pl.Buffered(1))
    return pl.BlockSpec((None, *shape), lambda *_: (layer,) + (0,) * nd, pipeline_mode=pl.Buffered(1))


def _bdot(a, b):
    return jnp.dot(a.astype(BF16), b.astype(BF16), preferred_element_type=F32)


def _dot_nt(a, b):
    return lax.dot_general(a.astype(BF16), b.astype(BF16), (((1,), (1,)), ((), ())), preferred_element_type=F32)


def _dot_tn(a, b):
    return lax.dot_general(a.astype(BF16), b.astype(BF16), (((0,), (0,)), ((), ())), preferred_element_type=F32)


def _split2(x):
    hi = x.astype(BF16)
    lo = (x - hi.astype(F32)).astype(BF16)
    return hi, lo


def _split3(x):
    h1 = x.astype(BF16)
    r1 = x - h1.astype(F32)
    h2 = r1.astype(BF16)
    h3 = (r1 - h2.astype(F32)).astype(BF16)
    return h1, h2, h3


def _dot3(a, b):
    ah, al = _split2(a)
    bh, bl = _split2(b)
    return (jnp.dot(ah, bh, preferred_element_type=F32) + jnp.dot(ah, bl, preferred_element_type=F32)
            + jnp.dot(al, bh, preferred_element_type=F32))


def _rms(x, g):
    return x * lax.rsqrt(jnp.mean(x * x, axis=-1, keepdims=True) + EPS) * g


def _silu(x):
    return x * jax.nn.sigmoid(x)


def _dwconv(x, w, b):
    n = x.shape[0]
    row = lax.broadcasted_iota(jnp.int32, (n, 1), 0)
    prev = jnp.where(row == 0, 0.0, pltpu.roll(x, 1, 0))
    nxt = jnp.where(row == n - 1, 0.0, pltpu.roll(x, n - 1, 0))
    return prev * w[0:1, :] + x * w[1:2, :] + nxt * w[2:3, :] + b


def _ada_kernel(cond_ref, w_ref, b_ref, o_ref):
    o_ref[0] = _bdot(_silu(cond_ref[...]), w_ref[0]) + b_ref[0]


def _ada_call(cond, ada_w, ada_b):
    n = 6 * D_MODEL
    tn = n // 4
    rows = cond.shape[0]
    return pl.pallas_call(
        _ada_kernel,
        grid=(DEPTH, n // tn),
        in_specs=[
            pl.BlockSpec((rows, D_MODEL), lambda l, j: (0, 0)),
            pl.BlockSpec((1, D_MODEL, tn), lambda l, j: (l, 0, j)),
            pl.BlockSpec((1, 1, tn), lambda l, j: (l, 0, j)),
        ],
        out_specs=pl.BlockSpec((1, rows, tn), lambda l, j: (l, 0, j)),
        out_shape=jax.ShapeDtypeStruct((DEPTH, rows, n), F32),
        compiler_params=_params(2),
        name="adaln",
    )(cond, ada_w, ada_b.reshape(DEPTH, 1, n))


def _in_kernel(x_ref, mod_ref, g_ref, wt_ref, *o_refs):
    h = _rms(x_ref[...], g_ref[...]) * (1.0 + mod_ref[0, 1:2, :]) + mod_ref[0, 0:1, :]
    hb = h.astype(BF16)
    off = 0
    for o_ref, wd, padded in zip(o_refs, IN_SPLIT, IN_WIDTHS):
        out = lax.dot_general(hb, wt_ref[off:off + padded, :], (((1,), (1,)), ((), ())), preferred_element_type=F32)
        if padded != wd:
            out = jnp.where(lax.broadcasted_iota(jnp.int32, (1, padded), 1) < wd, out, 0.0)
        o_ref[...] = out
        off += wd


def _mod_spec(shared, sl):
    if shared:
        return pl.BlockSpec((1, 6, D_MODEL), lambda i: (0, 0, 0))
    assert sl % ROW_TILE == 0
    return pl.BlockSpec((1, 6, D_MODEL), lambda i: (i // (sl // ROW_TILE), 0, 0))


def _in_call(x, mod, g, wt, layer, nb, sl, shared_mod):
    n_tok = nb * sl
    assert n_tok % ROW_TILE == 0
    row = lambda i: (i, 0)
    return pl.pallas_call(
        _in_kernel,
        grid=(n_tok // ROW_TILE,),
        in_specs=[
            pl.BlockSpec((ROW_TILE, D_MODEL), row),
            _mod_spec(shared_mod, sl),
            _resident((1, D_MODEL)),
            _resident((sum(IN_SPLIT), D_MODEL), layer),
        ],
        out_specs=[pl.BlockSpec((ROW_TILE, wd), row) for wd in IN_WIDTHS],
        out_shape=[jax.ShapeDtypeStruct((n_tok, wd), F32) for wd in IN_WIDTHS],
        compiler_params=_params(1),
        name="in_proj",
    )(x, mod, g, wt)


def _filter_kernel(feat_ref, w1, b1, f1, w2, b2, f2, w3, dec, cos_ref, sin_ref, kre_ref, kim_ref, *, sl_len):
    feat = feat_ref[...]
    h = jnp.sin(f1[...] * (_dot3(feat, w1[...]) + b1[...]))
    h = jnp.sin(f2[...] * (_dot3(h, w2[...]) + b2[...]))
    t = feat[:, 0:1]
    filt = _dot3(h, w3[...]) * jnp.exp(-t * dec[...])
    row = lax.broadcasted_iota(jnp.int32, (sl_len, 1), 0)
    hf = jnp.concatenate([filt[:, 0:HY_W], filt[:, 2 * HY_W:3 * HY_W]], axis=1)
    hb = jnp.where(row == 0, 0.0, jnp.concatenate([filt[:, HY_W:2 * HY_W], filt[:, 3 * HY_W:4 * HY_W]], axis=1))
    ksum = hf + hb
    kre = _bdot(cos_ref[...], ksum)
    kim = _bdot(sin_ref[...], hb - hf)
    nyq = jnp.sum(jnp.where(row % 2 == 1, -ksum, ksum), axis=0, keepdims=True)
    scale = jnp.where(row == 0, 0.5 / sl_len, 1.0 / sl_len)
    kre = kre * scale
    kim = jnp.where(row == 0, nyq, kim) * scale
    for o in range(2):
        kre_ref[o] = kre[:, o * HY_W:(o + 1) * HY_W]
        kim_ref[o] = kim[:, o * HY_W:(o + 1) * HY_W]


def _filter_call(feat, w1, b1, f1, w2, b2, f2, w3, dec, cos, sinp, sl):
    hid = HY_HIDDEN
    return pl.pallas_call(
        functools.partial(_filter_kernel, sl_len=sl),
        grid=(1,),
        in_specs=[
            _resident((sl, LANES)), _resident((LANES, hid)), _resident((1, hid)), _resident((1, hid)),
            _resident((hid, hid)), _resident((1, hid)), _resident((1, hid)),
            _resident((hid, 4 * HY_W)), _resident((1, 4 * HY_W)),
            _resident((sl, sl)), _resident((sl, sl)),
        ],
        out_specs=[pl.BlockSpec((2, sl, HY_W), lambda i: (0, 0, 0))] * 2,
        out_shape=[jax.ShapeDtypeStruct((2, sl, HY_W), F32)] * 2,
        compiler_params=_params(1),
        name="hyena_filter",
    )(feat, w1, b1, f1, w2, b2, f2, w3, dec, cos, sinp)


def _hyena_kernel(x_ref, cw, cb, bias, kre, kim, cos, sin, sin_t, o_ref, hc, u, pre_s, pim_s, *, sl_len):
    tile = min(DFT_TILE, sl_len)
    nt = sl_len // tile
    hc[...] = _dwconv(x_ref[...], cw[...], cb[...])
    u[...] = hc[:, 0:HY_W].astype(BF16)

    tiles = [slice(i * tile, (i + 1) * tile) for i in range(nt)]

    def skewed(matmuls, elementwise):
        pending = None
        for rows in tiles:
            out = matmuls(rows)
            if pending is not None:
                elementwise(*pending)
            pending = (rows, out)
        elementwise(*pending)

    def conv(order, epilogue):
        def spectrum(rows):
            return (jnp.dot(cos[rows, :], u[...], preferred_element_type=F32),
                    jnp.dot(sin[rows, :], u[...], preferred_element_type=F32))

        def multiply(rows, ab):
            a, b = ab
            kr = kre[order, rows, :]
            ki = kim[order, rows, :]
            if rows.start == 0:
                first = lax.broadcasted_iota(jnp.int32, (tile, 1), 0) == 0
                pre = jnp.where(first, a * kr, a * kr + b * ki)
                pim = jnp.where(first, b * ki, b * kr - a * ki)
            else:
                pre = a * kr + b * ki
                pim = b * kr - a * ki
            pre_s[rows, :] = pre.astype(BF16)
            pim_s[rows, :] = pim.astype(BF16)

        def back(rows):
            return (jnp.dot(cos[rows, :], pre_s[...], preferred_element_type=F32)
                    + jnp.dot(sin_t[rows, :], pim_s[...], preferred_element_type=F32))

        skewed(spectrum, multiply)
        skewed(back, epilogue)

    def after_first(rows, y):
        z = hc[rows, HY_W:2 * HY_W] * (y + hc[rows, 0:HY_W] * bias[0:1, :])
        hc[rows, 0:HY_W] = z
        u[rows, :] = z.astype(BF16)

    def after_second(rows, y):
        o_ref[rows, :] = (hc[rows, 2 * HY_W:3 * HY_W] * (y + hc[rows, 0:HY_W] * bias[1:2, :])).astype(BF16)

    conv(0, after_first)
    conv(1, after_second)


def _hyena_call(hy_in, cw, cb, bias, kre, kim, mats, nb, sl):
    seq = lambda b: (b, 0)
    mat = _resident((sl, sl))
    return pl.pallas_call(
        functools.partial(_hyena_kernel, sl_len=sl),
        grid=(nb,),
        in_specs=[
            pl.BlockSpec((sl, 3 * HY_W), seq),
            _resident((3, 3 * HY_W)), _resident((1, 3 * HY_W)), _resident((2, HY_W)),
            _resident((2, sl, HY_W)), _resident((2, sl, HY_W)),
            mat, mat, mat,
        ],
        out_specs=pl.BlockSpec((sl, HY_W), seq),
        out_shape=jax.ShapeDtypeStruct((nb * sl, HY_W), BF16),
        scratch_shapes=[pltpu.VMEM((sl, 3 * HY_W), F32)] + [pltpu.VMEM((sl, HY_W), BF16)] * 3,
        compiler_params=_params(1),
        name="hyena_conv",
    )(hy_in, cw, cb, bias, kre, kim, *mats)


SSM_PAIRS = SSM_HEADS // 2
PAIR_W = 2 * SSM_HEAD_DIM


def _ssd_kernel(xbc_ref, dt_ref, h0_ref, cw, cb, dtb, alog, dsk, expand, o_ref, hfin_ref, xbc, dt, dta, hst, y_bwd, *,
                sl_len):
    nc = sl_len // SSM_CHUNK
    q = SSM_CHUNK
    xbc[...] = _silu(_dwconv(xbc_ref[...], cw[...], cb[...]))
    dtv = jax.nn.softplus(dt_ref[...] + dtb[...])
    dt[...] = dtv
    dta[...] = dtv * (-jnp.exp(alog[...]))
    hst[...] = h0_ref[0]
    ii = lax.broadcasted_iota(jnp.int32, (q, q), 0)
    jj = lax.broadcasted_iota(jnp.int32, (q, q), 1)
    low_lanes = lax.broadcasted_iota(jnp.int32, (1, PAIR_W), 1) < SSM_HEAD_DIM
    low_rows = lax.broadcasted_iota(jnp.int32, (2 * SSM_STATE, 1), 0) < SSM_STATE

    def prepare(c, d):
        rows = pl.ds(pl.multiple_of(c * q, q), q)
        xs = xbc[rows, 0:SSM_W]
        bm = xbc[rows, SSM_W:SSM_W + SSM_GROUPS * SSM_STATE]
        cm = xbc[rows, SSM_W + SSM_GROUPS * SSM_STATE:XBC_W]
        tri = (jj <= ii) if d == 0 else (jj >= ii)
        trib = tri.astype(BF16)
        a1, a2, a3 = _split3(dta[rows, :])
        acum = (jnp.dot(trib, a1, preferred_element_type=F32) + jnp.dot(trib, a2, preferred_element_type=F32)
                + jnp.dot(trib, a3, preferred_element_type=F32))
        dtc = dt[rows, :]
        last = acum[q - 1:q, :] if d == 0 else acum[0:1, :]
        w1, w2 = _split2(dtc * jnp.exp(last - acum))
        w_wide = (jnp.dot(w1, expand[d], preferred_element_type=F32)
                  + jnp.dot(w2, expand[d], preferred_element_type=F32))
        bm_t = bm.T.astype(BF16)
        cb_t = [_bdot(jnp.where(low_lanes, cm, 0.0), bm_t), _bdot(jnp.where(low_lanes, 0.0, cm), bm_t)]
        return dict(d=d, rows=rows, xs=xs, cm=cm, tri=tri, acum=acum, acum_t=acum.T, dt_t=dtc.T,
                    cdec=jnp.exp(last), xdd=xs * w_wide, bm_t=bm_t, cb_t=cb_t)

    def step(c, carry):
        chunks = [prepare(c, 0), prepare(nc - 1 - c, 1)]
        units = [(ch, p) for ch in chunks for p in range(SSM_PAIRS)]
        scores, grow = [], []
        for ch, p in units:
            g = p // (SSM_PAIRS // SSM_GROUPS)
            sc, gr = [], []
            for hh in range(2):
                col = ch["d"] * SSM_HEADS + 2 * p + hh
                a_col = jnp.broadcast_to(ch["acum"][:, col:col + 1], (q, q))
                lmat = jnp.exp(jnp.where(ch["tri"], a_col - ch["acum_t"][col:col + 1, :], -jnp.inf))
                sc.append((ch["cb_t"][g] * lmat * ch["dt_t"][col:col + 1, :]).astype(BF16))
                gr.append(jnp.exp(a_col))
            scores.append(jnp.concatenate(sc, axis=1))
            grow.append(jnp.where(low_lanes, gr[0], gr[1]))
        y_diag, y_off, states = [], [], []
        for (ch, p), sc in zip(units, scores):
            xp = ch["xs"][:, p * PAIR_W:(p + 1) * PAIR_W]
            x_blocks = jnp.concatenate([jnp.where(low_lanes, xp, 0.0), jnp.where(low_lanes, 0.0, xp)], axis=0)
            y_diag.append(jnp.dot(sc, x_blocks.astype(BF16), preferred_element_type=F32))
        for ch, p in units:
            y_off.append(_bdot(ch["cm"], hst[ch["d"] * SSM_PAIRS + p]))
            states.append(_bdot(ch["bm_t"], ch["xdd"][:, p * PAIR_W:(p + 1) * PAIR_W]))
        for i, (ch, p) in enumerate(units):
            slab = ch["d"] * SSM_PAIRS + p
            col = ch["d"] * SSM_HEADS + 2 * p
            cd = jnp.where(low_lanes, ch["cdec"][:, col:col + 1], ch["cdec"][:, col + 1:col + 2])
            own_rows = low_rows if p < SSM_PAIRS // SSM_GROUPS else jnp.logical_not(low_rows)
            hst[slab] = hst[slab] * cd + jnp.where(own_rows, states[i], 0.0)
        for k, ch in enumerate(chunks):
            y = jnp.concatenate([y_diag[k * SSM_PAIRS + p] + y_off[k * SSM_PAIRS + p] * grow[k * SSM_PAIRS + p]
                                 for p in range(SSM_PAIRS)], axis=1)
            if ch["d"] == 0:
                o_ref[ch["rows"], :] = y + dsk[...] * ch["xs"]
            else:
                y_bwd[ch["rows"], :] = y
        return carry

    lax.fori_loop(0, nc, step, 0, unroll=2)
    o_ref[...] += y_bwd[...]
    for slab in range(2 * SSM_PAIRS):
        g = (slab % SSM_PAIRS) // (SSM_PAIRS // SSM_GROUPS)
        st_t = hst[slab].T
        for hh in range(2):
            hfin_ref[0, 2 * slab + hh] = st_t[hh * SSM_HEAD_DIM:(hh + 1) * SSM_HEAD_DIM,
                                              g * SSM_STATE:(g + 1) * SSM_STATE]


def _ssd_call(s_xbc, s_dt, h0, cw, cb, dtb, alog, dsk, expand, nb, sl, shared_h0):
    seq = lambda b: (b, 0)
    st_shape = (1, 2 * SSM_PAIRS, 2 * SSM_STATE, PAIR_W)
    fin_shape = (1, 2 * SSM_HEADS, SSM_HEAD_DIM, SSM_STATE)
    h0_map =(lambda b: (0, 0, 0, 0)) if shared_h0 else (lambda b: (b, 0, 0, 0))
    return pl.pallas_call(
        functools.partial(_ssd_kernel, sl_len=sl),
        grid=(nb,),
        in_specs=[
            pl.BlockSpec((sl, XBC_W), seq), pl.BlockSpec((sl, LANES), seq), pl.BlockSpec(st_shape, h0_map),
            _resident((3, XBC_W)), _resident((1, XBC_W)), _resident((1, LANES)), _resident((1, LANES)),
            _resident((1, SSM_W)), _resident((2, LANES, SSM_W)),
        ],
        out_specs=[pl.BlockSpec((sl, SSM_W), seq), pl.BlockSpec(fin_shape, lambda b: (b, 0, 0, 0))],
        out_shape=[jax.ShapeDtypeStruct((nb * sl, SSM_W), F32), jax.ShapeDtypeStruct((nb, *fin_shape[1:]), F32)],
        scratch_shapes=[pltpu.VMEM((sl, XBC_W), F32), pltpu.VMEM((sl, LANES), F32), pltpu.VMEM((sl, LANES), F32),
                        pltpu.VMEM(st_shape[1:], F32), pltpu.VMEM((sl, SSM_W), F32)],
        compiler_params=_params(1),
        name="ssd_scan",
    )(s_xbc, s_dt, h0, cw, cb, dtb, alog, dsk, expand)


def _pack_states(h):
    nb = h.shape[0]
    t = h.reshape(nb, 2, SSM_PAIRS, 2, SSM_HEAD_DIM, SSM_STATE).transpose(0, 1, 2, 5, 3, 4)
    t = t.reshape(nb, 2, SSM_PAIRS, SSM_STATE, PAIR_W)
    z = jnp.zeros_like(t)
    first = (jnp.arange(SSM_PAIRS) < SSM_PAIRS // SSM_GROUPS)[None, None, :, None, None]
    slabs = jnp.where(first, jnp.concatenate([t, z], axis=3), jnp.concatenate([z, t], axis=3))
    return slabs.reshape(nb, 2 * SSM_PAIRS, 2 * SSM_STATE, PAIR_W)


def _expand_table():
    r = jnp.arange(LANES)[:, None]
    head = jnp.arange(SSM_W)[None, :] // SSM_HEAD_DIM
    return jnp.stack([r == head, r == SSM_HEADS + head]).astype(BF16)


def _attend_groups(keys, q_ts, vals_ts, mask, sink_ref):
    logits = []
    for g, q_t in enumerate(q_ts):
        z = jnp.zeros_like(q_t)
        rhs = jnp.concatenate([q_t, z] if g == 0 else [z, q_t], axis=0)
        logits.append(jnp.dot(keys, rhs, preferred_element_type=F32))
    if mask is not None:
        logits = [jnp.where(mask, s, -jnp.inf) for s in logits]
    tops = [jnp.maximum(jnp.max(s, axis=0, keepdims=True), sink_ref[g]) for g, s in enumerate(logits)]
    exps = [jnp.exp(s - m) for s, m in zip(logits, tops)]
    denoms = [jnp.sum(e, axis=0, keepdims=True) + jnp.exp(sink_ref[g] - m) for g, (e, m) in enumerate(zip(exps, tops))]
    return [jnp.dot(v, e.astype(BF16), preferred_element_type=F32) / d for v, e, d in zip(vals_ts, exps, denoms)]


def _ctx_attn_kernel(q_ref, k_ref, v_ref, sink_ref, o_ref, *, sl_len):
    scale = HEAD_DIM ** -0.5
    q_t = (q_ref[...] * scale).T.astype(BF16)
    keys = k_ref[...].astype(BF16)
    v_t = v_ref[...].T.astype(BF16)
    q_gs = [jnp.concatenate(
        [q_t[(g * Q_PER_KV + hq) * HEAD_DIM:(g * Q_PER_KV + hq + 1) * HEAD_DIM, :] for hq in range(Q_PER_KV)], axis=1)
        for g in range(KV_HEADS)]
    v_gs = [v_t[g * HEAD_DIM:(g + 1) * HEAD_DIM, :] for g in range(KV_HEADS)]
    o_ts = _attend_groups(keys, q_gs, v_gs, None, sink_ref)
    outs = [o_t[:, hq * sl_len:(hq + 1) * sl_len] for o_t in o_ts for hq in range(Q_PER_KV)]
    o_ref[...] = jnp.concatenate(outs, axis=0).T.astype(BF16)


def _ctx_attn_call(q, k, v, sink_rows, nb, sl):
    seq = lambda b: (b, 0)
    return pl.pallas_call(
        functools.partial(_ctx_attn_kernel, sl_len=sl),
        grid=(nb,),
        in_specs=[pl.BlockSpec((sl, ATT_W), seq), pl.BlockSpec((sl, KV_W), seq), pl.BlockSpec((sl, KV_W), seq),
                  _resident((KV_HEADS, 1, Q_PER_KV * sl))],
        out_specs=pl.BlockSpec((sl, ATT_W), seq),
        out_shape=jax.ShapeDtypeStruct((nb * sl, ATT_W), BF16),
        compiler_params=_params(1),
        name="ctx_attention",
    )(q, k, v, sink_rows)


def _rope(x, cos, sin_signed):
    w = x.shape[1]
    lane = lax.broadcasted_iota(jnp.int32, (1, w), 1)
    partner = jnp.where(lane % 32 < 16, pltpu.roll(x, w - 16, 1), pltpu.roll(x, 16, 1))
    return x * cos + partner * sin_signed


def _lat_attn_kernel(q_ref, k_ref, v_ref, ck_ref, cv_ref, cos_ref, sin_ref, sink_ref, o_ref, q_t, kcat, v_t, o_t, *,
                     sl_len, past):
    blk = ATT_BLOCK
    nblk = sl_len // blk
    scale = HEAD_DIM ** -0.5
    q_t[...] = (_rope(q_ref[...], cos_ref[...], sin_ref[...]) * scale).T.astype(BF16)
    kcat[0:past, :] = ck_ref[0].astype(BF16)
    kcat[past:, :] = _rope(k_ref[...], cos_ref[:, 0:KV_W], sin_ref[:, 0:KV_W]).astype(BF16)
    v_t[:, 0:past] = cv_ref[0].T.astype(BF16)
    v_t[:, past:] = v_ref[...].T.astype(BF16)
    for i in range(nblk):
        cols = slice(i * blk, (i + 1) * blk)
        lo, hi = max(i - 1, 0), min(i + 2, nblk)
        band = slice(past + lo * blk, past + hi * blk)
        n_keys = past + (hi - lo) * blk
        kk = lax.broadcasted_iota(jnp.int32, (n_keys, 1), 0)
        qi = lax.broadcasted_iota(jnp.int32, (1, Q_PER_KV * blk), 1) % blk
        rel = kk - past + (lo - i) * blk
        mask = (kk < past) | (jnp.abs(qi - rel) <= WINDOW)
        keys = jnp.concatenate([kcat[0:past, :], kcat[band, :]], axis=0)
        q_gs, v_gs = [], []
        for g in range(KV_HEADS):
            drows = slice(g * HEAD_DIM, (g + 1) * HEAD_DIM)
            q_gs.append(jnp.concatenate(
                [q_t[(g * Q_PER_KV + hq) * HEAD_DIM:(g * Q_PER_KV + hq + 1) * HEAD_DIM, cols] for hq in range(Q_PER_KV)],
                axis=1))
            v_gs.append(jnp.concatenate([v_t[drows, 0:past], v_t[drows, band]], axis=1))
        for g, out in enumerate(_attend_groups(keys, q_gs, v_gs, mask, sink_ref)):
            for hq in range(Q_PER_KV):
                h = g * Q_PER_KV + hq
                o_t[h * HEAD_DIM:(h + 1) * HEAD_DIM, cols] = out[:, hq * blk:(hq + 1) * blk]
    o_ref[...] = o_t[...].T.astype(BF16)


def _lat_attn_call(q, k, v, ck, cv, cos, sin, sink_rows, nb, sl):
    seq = lambda b: (b, 0)
    past = ck.shape[1]
    cache = pl.BlockSpec((1, past, KV_W), lambda b: (b, 0, 0))
    return pl.pallas_call(
        functools.partial(_lat_attn_kernel, sl_len=sl, past=past),
        grid=(nb,),
        in_specs=[pl.BlockSpec((sl, ATT_W), seq), pl.BlockSpec((sl, KV_W), seq), pl.BlockSpec((sl, KV_W), seq),
                  cache, cache, _resident((sl, ATT_W)), _resident((sl, ATT_W)),
                  _resident((KV_HEADS, 1, Q_PER_KV * ATT_BLOCK))],
        out_specs=pl.BlockSpec((sl, ATT_W), seq),
        out_shape=jax.ShapeDtypeStruct((nb * sl, ATT_W), BF16),
        scratch_shapes=[pltpu.VMEM((ATT_W, sl), BF16), pltpu.VMEM((past + sl, KV_W), BF16),
                        pltpu.VMEM((KV_W, past + sl), BF16), pltpu.VMEM((ATT_W, sl), F32)],
        compiler_params=_params(1),
        name="latent_attention",
    )(q, k, v, ck, cv, cos, sin, sink_rows)


def _sink_rows(sink, n_q):
    return jnp.repeat(sink.reshape(KV_HEADS, Q_PER_KV), n_q, axis=1).reshape(KV_HEADS, 1, Q_PER_KV * n_q)


def _post_kernel(x_ref, hy_ref, sy_ref, sz_ref, at_ref, g_ref, mod_ref, snorm, npost, nfpre, nfpost,
                 wh, ws, wa, wo, wg, wu, wd, o_ref):
    m = mod_ref[0]
    halves = [slice(r * (ROW_TILE // 2), (r + 1) * (ROW_TILE // 2)) for r in range(2)]
    ssm_y = [_rms(sy_ref[r, :] * _silu(sz_ref[r, :]), snorm[...]) for r in halves]
    p_hy = [_bdot(hy_ref[r, :], wh[...]) for r in halves]
    p_ss = [_bdot(y, ws[...]) for y in ssm_y]
    p_at = [_bdot(at_ref[r, :], wa[...]) for r in halves]
    merged = [jax.nn.sigmoid(g_ref[r, 0:D_MODEL]) * a + jax.nn.sigmoid(g_ref[r, D_MODEL:2 * D_MODEL]) * b
              + jax.nn.sigmoid(g_ref[r, 2 * D_MODEL:3 * D_MODEL]) * c for r, a, b, c in zip(halves, p_hy, p_ss, p_at)]
    mix = [_bdot(v, wo[...]) for v in merged]
    x1 = [x_ref[r, :] + m[2:3, :] * _rms(v, npost[...]) for r, v in zip(halves, mix)]
    hb = [(_rms(v, nfpre[...]) * (1.0 + m[4:5, :]) + m[3:4, :]).astype(BF16) for v in x1]
    f = [None, None]
    for lo, hi in FFN_CHUNKS:
        gate = [jnp.dot(v, wg[:, lo:hi], preferred_element_type=F32) for v in hb]
        up = [jnp.dot(v, wu[:, lo:hi], preferred_element_type=F32) for v in hb]
        part = [_bdot(_silu(a) * b, wd[lo:hi, :]) for a, b in zip(gate, up)]
        f = [p if acc is None else acc + p for acc, p in zip(f, part)]
    for r, a, b in zip(halves, x1, f):
        o_ref[r, :] = a + m[5:6, :] * _rms(b, nfpost[...])


def _post_call(x, hy, sy, sz, at, g, mod, snorm, npost, nfpre, nfpost, wh, ws, wa, wo, wg, wu, wd, layer, nb, sl,
               shared_mod):
    n_tok = nb * sl
    assert n_tok % ROW_TILE == 0
    tok = lambda wdt: pl.BlockSpec((ROW_TILE, wdt), lambda i: (i, 0))
    vec = lambda wdt: _resident((1, wdt))
    return pl.pallas_call(
        _post_kernel,
        grid=(n_tok // ROW_TILE,),
        in_specs=[tok(D_MODEL), tok(HY_W), tok(SSM_W), tok(SSM_W), tok(ATT_W), tok(3 * D_MODEL),
                  _mod_spec(shared_mod, sl),
                  vec(SSM_W), vec(D_MODEL), vec(D_MODEL), vec(D_MODEL),
                  _resident((HY_W, D_MODEL), layer), _resident((SSM_W, D_MODEL), layer),
                  _resident((ATT_W, D_MODEL), layer), _resident((D_MODEL, D_MODEL), layer),
                  _resident((D_MODEL, FFN_HIDDEN), layer), _resident((D_MODEL, FFN_HIDDEN), layer),
                  _resident((FFN_HIDDEN, D_MODEL), layer)],
        out_specs=tok(D_MODEL),
        out_shape=jax.ShapeDtypeStruct((n_tok, D_MODEL), F32),
        compiler_params=_params(1),
        name="merge_ffn",
    )(x, hy, sy, sz, at, g, mod, snorm, npost, nfpre, nfpost, wh, ws, wa, wo, wg, wu, wd)


def _dft_mats(sl):
    radix = 32
    assert sl % radix == 0
    s = jnp.arange(sl, dtype=jnp.int32)

    def angles(f):
        return ((f[:, None] * s[None, :]) % (2 * sl)).astype(F32) * (math.pi / sl)

    a_hi = angles(jnp.arange(sl // radix, dtype=jnp.int32) * radix)[:, None, :]
    a_lo = angles(jnp.arange(radix, dtype=jnp.int32))[None, :, :]
    cos = (jnp.cos(a_hi) * jnp.cos(a_lo) - jnp.sin(a_hi) * jnp.sin(a_lo)).reshape(sl, sl)
    sin = (jnp.sin(a_hi) * jnp.cos(a_lo) + jnp.cos(a_hi) * jnp.sin(a_lo)).reshape(sl, sl)
    sign = jnp.where(s % 2 == 1, -1.0, 1.0).astype(F32)
    sinp = jnp.where(s[:, None] == 0, sign[None, :], sin).astype(BF16)
    return cos.astype(BF16), sinp, sinp.T


def _filter_features(sl):
    t = jnp.arange(sl, dtype=F32) / sl
    ang = 2.0 * jnp.pi * t[:, None] * jnp.arange(1, HY_BANDS + 1, dtype=F32)[None]
    feat = jnp.concatenate([t[:, None], jnp.cos(ang), jnp.sin(ang)], axis=-1)
    return jnp.pad(feat, ((0, 0), (0, LANES - HY_FEAT)))


def _rope_tables(sl):
    rows = sl // GRID_W
    row = jnp.repeat(jnp.arange(rows), GRID_W).astype(F32)
    col = jnp.tile(jnp.arange(GRID_W), rows).astype(F32)
    nq = HEAD_DIM // 4
    inv = ROPE_BASE ** (-jnp.arange(nq, dtype=F32) / nq)
    ar = row[:, None] * inv
    ac = col[:, None] * inv
    cos = jnp.concatenate([jnp.cos(ar), jnp.cos(ar), jnp.cos(ac), jnp.cos(ac)], axis=-1)
    sin = jnp.concatenate([-jnp.sin(ar), jnp.sin(ar), -jnp.sin(ac), jnp.sin(ac)], axis=-1)
    return jnp.tile(cos, (1, N_HEADS)), jnp.tile(sin, (1, N_HEADS))


def _pad_lanes(v):
    return jnp.pad(v.reshape(1, -1), ((0, 0), (0, LANES - v.size)))


def kernel(x_prompt, x_sample, c, cache_k, cache_v, state_ssm, c_ctx, ada_w, ada_b,
           norm_mix_pre, norm_mix_post, norm_ffn_pre, norm_ffn_post, w_in,
           hy_conv_w, hy_conv_b, hy_w1, hy_b1, hy_freq1, hy_w2, hy_b2, hy_freq2, hy_w3,
           hy_decay, hy_bias, ssm_conv_w, ssm_conv_b, ssm_a_log, ssm_dt_bias, ssm_d, ssm_norm,
           attn_sink, hy_proj, ssm_proj, attn_proj, w_out, ffn_w_gate, ffn_w_up, ffn_w_down):
    nb_c, sl_c, _ = x_prompt.shape
    nb_l, sl_l, _ = x_sample.shape
    past = cache_k.shape[2]

    cond = jnp.concatenate([c_ctx[None], c, jnp.zeros((8 - 1 - nb_l, D_MODEL), F32)], axis=0)
    mod = _ada_call(cond, ada_w, ada_b).reshape(DEPTH, 8, 6, D_MODEL)

    streams = {}
    for name, x in (("ctx", x_prompt), ("lat", x_sample)):
        nb, sl, _ = x.shape
        streams[name] = dict(nb=nb, sl=sl, x=x.reshape(nb * sl, D_MODEL), mats=_dft_mats(sl),
                             feat=_filter_features(sl))
    rope_cos, rope_sin = _rope_tables(sl_l)
    zero_state = jnp.zeros((1, 2 * SSM_PAIRS, 2 * SSM_STATE, PAIR_W), F32)
    expand = _expand_table()
    row = lambda v: v.reshape(1, -1)

    w_in_t = jnp.swapaxes(w_in, 1, 2).astype(BF16)
    wh, ws, wa, wo = (w.astype(BF16) for w in (hy_proj, ssm_proj, attn_proj, w_out))
    wg, wu, wd = (w.astype(BF16) for w in (ffn_w_gate, ffn_w_up, ffn_w_down))

    ks, vs, ss = [], [], []
    for l in range(DEPTH):
        w1p = jnp.pad(hy_w1[l], ((0, LANES - HY_FEAT), (0, 0)))
        dtb = _pad_lanes(ssm_dt_bias[l])
        alog = _pad_lanes(ssm_a_log[l])
        dsk = jnp.repeat(ssm_d[l, 0] + ssm_d[l, 1], SSM_HEAD_DIM).reshape(1, SSM_W)
        sink = {"ctx": _sink_rows(attn_sink[l], sl_c), "lat": _sink_rows(attn_sink[l], ATT_BLOCK)}
        for name in ("ctx", "lat"):
            st = streams[name]
            nb, sl, is_ctx = st["nb"], st["sl"], name == "ctx"
            m = mod[l, 0:1] if is_ctx else mod[l, 1:1 + nb]
            kre, kim = _filter_call(st["feat"], w1p, row(hy_b1[l]), row(hy_freq1[l]), hy_w2[l], row(hy_b2[l]),
                                    row(hy_freq2[l]), hy_w3[l], row(hy_decay[l]), st["mats"][0], st["mats"][1], sl)
            hy_in, s_z, s_xbc, s_dt, a_q, a_k, a_v, gate = _in_call(
                st["x"], m, row(norm_mix_pre[l]), w_in_t, l, nb, sl, is_ctx)
            hy_y = _hyena_call(hy_in, hy_conv_w[l], row(hy_conv_b[l]), hy_bias[l], kre, kim, st["mats"], nb, sl)
            h0 = zero_state if is_ctx else _pack_states(state_ssm[:, l])
            ssm_y, h_fin = _ssd_call(s_xbc, s_dt, h0, ssm_conv_w[l], row(ssm_conv_b[l]), dtb, alog, dsk, expand,
                                     nb, sl, is_ctx)
            if is_ctx:
                att = _ctx_attn_call(a_q, a_k, a_v, sink[name], nb, sl)
                ks.append(a_k.reshape(nb, sl, KV_HEADS, HEAD_DIM))
                vs.append(a_v.reshape(nb, sl, KV_HEADS, HEAD_DIM))
                ss.append(h_fin.reshape(nb, 2, SSM_HEADS, SSM_HEAD_DIM, SSM_STATE))
            else:
                att = _lat_attn_call(a_q, a_k, a_v, cache_k[:, l].reshape(nb, past, KV_W),
                                     cache_v[:, l].reshape(nb, past, KV_W), rope_cos, rope_sin, sink[name], nb, sl)
            st["x"] = _post_call(st["x"], hy_y, ssm_y, s_z, att, gate, m, row(ssm_norm[l]), row(norm_mix_post[l]),
                                 row(norm_ffn_pre[l]), row(norm_ffn_post[l]), wh, ws, wa, wo, wg, wu, wd, l, nb, sl,
                                 is_ctx)

    y_prompt = streams["ctx"]["x"].reshape(nb_c, sl_c, D_MODEL)
    y_sample = streams["lat"]["x"].reshape(nb_l, sl_l, D_MODEL)
    return (y_prompt, y_sample, jnp.stack(ks, axis=1), jnp.stack(vs, axis=1), jnp.stack(ss, axis=1))
```

```python
import functools
import math

import jax
import jax.numpy as jnp
from jax import lax
from jax.experimental import pallas as pl
from jax.experimental.pallas import tpu as pltpu

F32 = jnp.float32
BF16 = jnp.bfloat16

D_MODEL = 1024
DEPTH = 2
GRID_W = 64
HY_W = 512
HY_BANDS = 16
HY_FEAT = 1 + 2 * HY_BANDS
HY_HIDDEN = 64
SSM_W = 512
SSM_HEAD_DIM = 64
SSM_HEADS = 8
SSM_GROUPS = 2
SSM_STATE = 64
SSM_CHUNK = 128
XBC_W = SSM_W + 2 * SSM_GROUPS * SSM_STATE
N_HEADS = 8
KV_HEADS = 2
HEAD_DIM = 64
Q_PER_KV = N_HEADS // KV_HEADS
ATT_W = N_HEADS * HEAD_DIM
KV_W = KV_HEADS * HEAD_DIM
WINDOW = 128
ATT_BLOCK = 128
ROPE_BASE = 10000.0
FFN_HIDDEN = ((8 * D_MODEL + 3 * 256 - 1) // (3 * 256)) * 256
EPS = 1e-6

LANES = 128
ROW_TILE = 512
DFT_TILE = 256
SEQ_ROWS_PER_STEP = 1024
LAT_BLOCKS_PER_ROUND = 2
MXU_TILE = 256
FFN_SPLIT = (FFN_HIDDEN // MXU_TILE + 1) // 2 * MXU_TILE
FFN_CHUNKS = ((0, FFN_SPLIT), (FFN_SPLIT, FFN_HIDDEN))
IN_SPLIT = (3 * HY_W, SSM_W, XBC_W, 2 * SSM_HEADS, ATT_W, KV_W, KV_W, 3 * D_MODEL)
IN_WIDTHS = (3 * HY_W, SSM_W, XBC_W, LANES, ATT_W, KV_W, KV_W, 3 * D_MODEL)
VMEM_LIMIT = 56 * 1024 * 1024


def _params(n_grid):
    return pltpu.CompilerParams(dimension_semantics=("arbitrary",) * n_grid, vmem_limit_bytes=VMEM_LIMIT)


def _resident(shape, layer=None):
    nd = len(shape)
    if layer is None:
        return pl.BlockSpec(shape, lambda *_: (0,) * nd, pipeline_mode=pl.Buffered(1))
    return pl.BlockSpec((None, *shape), lambda *_: (layer,) + (0,) * nd, pipeline_mode=pl.Buffered(1))


def _bdot(a, b):
    return jnp.dot(a.astype(BF16), b.astype(BF16), preferred_element_type=F32)


def _dot_nt(a, b):
    return lax.dot_general(a.astype(BF16), b.astype(BF16), (((1,), (1,)), ((), ())), preferred_element_type=F32)


def _dot_tn(a, b):
    return lax.dot_general(a.astype(BF16), b.astype(BF16), (((0,), (0,)), ((), ())), preferred_element_type=F32)


def _split2(x):
    hi = x.astype(BF16)
    lo = (x - hi.astype(F32)).astype(BF16)
    return hi, lo


def _split3(x):
    h1 = x.astype(BF16)
    r1 = x - h1.astype(F32)
    h2 = r1.astype(BF16)
    h3 = (r1 - h2.astype(F32)).astype(BF16)
    return h1, h2, h3


def _dot3(a, b):
    ah, al = _split2(a)
    bh, bl = _split2(b)
    return (jnp.dot(ah, bh, preferred_element_type=F32) + jnp.dot(ah, bl, preferred_element_type=F32)
            + jnp.dot(al, bh, preferred_element_type=F32))


def _rms(x, g):
    return x * lax.rsqrt(jnp.mean(x * x, axis=-1, keepdims=True) + EPS) * g


def _silu(x):
    return x * jax.nn.sigmoid(x)


def _dwconv(x, w, b, period):
    n = x.shape[0]
    row = lax.broadcasted_iota(jnp.int32, (n, 1), 0) % period
    prev = jnp.where(row == 0, 0.0, pltpu.roll(x, 1, 0))
    nxt = jnp.where(row == period - 1, 0.0, pltpu.roll(x, n - 1, 0))
    return prev * w[0:1, :] + x * w[1:2, :] + nxt * w[2:3, :] + b


def _ada_kernel(cond_ref, w_ref, b_ref, o_ref):
    o_ref[0] = _bdot(_silu(cond_ref[...]), w_ref[0]) + b_ref[0]


def _ada_call(cond, ada_w, ada_b):
    n = 6 * D_MODEL
    tn = n // 4
    rows = cond.shape[0]
    return pl.pallas_call(
        _ada_kernel,
        grid=(DEPTH, n // tn),
        in_specs=[
            pl.BlockSpec((rows, D_MODEL), lambda l, j: (0, 0)),
            pl.BlockSpec((1, D_MODEL, tn), lambda l, j: (l, 0, j)),
            pl.BlockSpec((1, 1, tn), lambda l, j: (l, 0, j)),
        ],
        out_specs=pl.BlockSpec((1, rows, tn), lambda l, j: (l, 0, j)),
        out_shape=jax.ShapeDtypeStruct((DEPTH, rows, n), F32),
        compiler_params=_params(2),
        name="adaln",
    )(cond, ada_w, ada_b.reshape(DEPTH, 1, n))


def _in_kernel(x_ref, mod_ref, g_ref, wt_ref, *o_refs):
    h = _rms(x_ref[...], g_ref[...]) * (1.0 + mod_ref[0, 1:2, :]) + mod_ref[0, 0:1, :]
    hb = h.astype(BF16)
    off = 0
    for o_ref, wd, padded in zip(o_refs, IN_SPLIT, IN_WIDTHS):
        out = lax.dot_general(hb, wt_ref[off:off + padded, :], (((1,), (1,)), ((), ())), preferred_element_type=F32)
        if padded != wd:
            out = jnp.where(lax.broadcasted_iota(jnp.int32, (1, padded), 1) < wd, out, 0.0)
        o_ref[...] = out
        off += wd


def _mod_spec(shared, sl):
    if shared:
        return pl.BlockSpec((1, 6, D_MODEL), lambda i: (0, 0, 0))
    assert sl % ROW_TILE == 0
    return pl.BlockSpec((1, 6, D_MODEL), lambda i: (i // (sl // ROW_TILE), 0, 0))


def _in_call(x, mod, g, wt, layer, nb, sl, shared_mod):
    n_tok = nb * sl
    assert n_tok % ROW_TILE == 0
    row = lambda i: (i, 0)
    return pl.pallas_call(
        _in_kernel,
        grid=(n_tok // ROW_TILE,),
        in_specs=[
            pl.BlockSpec((ROW_TILE, D_MODEL), row),
            _mod_spec(shared_mod, sl),
            _resident((1, D_MODEL)),
            _resident((sum(IN_SPLIT), D_MODEL), layer),
        ],
        out_specs=[pl.BlockSpec((ROW_TILE, wd), row) for wd in IN_WIDTHS],
        out_shape=[jax.ShapeDtypeStruct((n_tok, wd), F32) for wd in IN_WIDTHS],
        compiler_params=_params(1),
        name="in_proj",
    )(x, mod, g, wt)


def _filter_kernel(feat_ref, w1, b1, f1, w2, b2, f2, w3, dec, cos_ref, sin_ref, kre_ref, kim_ref, *, sl_len):
    feat = feat_ref[...]
    h = jnp.sin(f1[...] * (_dot3(feat, w1[...]) + b1[...]))
    h = jnp.sin(f2[...] * (_dot3(h, w2[...]) + b2[...]))
    t = feat[:, 0:1]
    filt = _dot3(h, w3[...]) * jnp.exp(-t * dec[...])
    row = lax.broadcasted_iota(jnp.int32, (sl_len, 1), 0)
    hf = jnp.concatenate([filt[:, 0:HY_W], filt[:, 2 * HY_W:3 * HY_W]], axis=1)
    hb = jnp.where(row == 0, 0.0, jnp.concatenate([filt[:, HY_W:2 * HY_W], filt[:, 3 * HY_W:4 * HY_W]], axis=1))
    ksum = hf + hb
    kre = _bdot(cos_ref[...], ksum)
    kim = _bdot(sin_ref[...], hb - hf)
    nyq = jnp.sum(jnp.where(row % 2 == 1, -ksum, ksum), axis=0, keepdims=True)
    scale = jnp.where(row == 0, 0.5 / sl_len, 1.0 / sl_len)
    kre = kre * scale
    kim = jnp.where(row == 0, nyq, kim) * scale
    for o in range(2):
        kre_ref[o] = kre[:, o * HY_W:(o + 1) * HY_W]
        kim_ref[o] = kim[:, o * HY_W:(o + 1) * HY_W]


def _filter_call(feat, w1, b1, f1, w2, b2, f2, w3, dec, cos, sinp, sl):
    hid = HY_HIDDEN
    return pl.pallas_call(
        functools.partial(_filter_kernel, sl_len=sl),
        grid=(1,),
        in_specs=[
            _resident((sl, LANES)), _resident((LANES, hid)), _resident((1, hid)), _resident((1, hid)),
            _resident((hid, hid)), _resident((1, hid)), _resident((1, hid)),
            _resident((hid, 4 * HY_W)), _resident((1, 4 * HY_W)),
            _resident((sl, sl)), _resident((sl, sl)),
        ],
        out_specs=[pl.BlockSpec((2, sl, HY_W), lambda i: (0, 0, 0))] * 2,
        out_shape=[jax.ShapeDtypeStruct((2, sl, HY_W), F32)] * 2,
        compiler_params=_params(1),
        name="hyena_filter",
    )(feat, w1, b1, f1, w2, b2, f2, w3, dec, cos, sinp)


def _hyena_kernel(x_ref, cw, cb, bias, kre, kim, cos, sin, sin_t, o_ref, hc, u, pre_s, pim_s, *, sl_len, n_seq):
    tile = min(DFT_TILE, sl_len)
    hc[...] = _dwconv(x_ref[...], cw[...], cb[...], sl_len)
    u[...] = hc[:, 0:HY_W].astype(BF16)

    units = [(slice(s * sl_len + t, s * sl_len + t + tile), slice(t, t + tile), slice(s * sl_len, (s + 1) * sl_len))
             for s in range(n_seq) for t in range(0, sl_len, tile)]

    def skewed(matmuls, elementwise):
        pending = None
        for unit in units:
            out = matmuls(unit)
            if pending is not None:
                elementwise(*pending)
            pending = (unit, out)
        elementwise(*pending)

    def conv(order, epilogue):
        def spectrum(unit):
            _, trows, seq_rows = unit
            return (jnp.dot(cos[trows, :], u[seq_rows, :], preferred_element_type=F32),
                    jnp.dot(sin[trows, :], u[seq_rows, :], preferred_element_type=F32))

        def multiply(unit, ab):
            rows, trows, _ = unit
            a, b = ab
            kr = kre[order, trows, :]
            ki = kim[order, trows, :]
            if trows.start == 0:
                first = lax.broadcasted_iota(jnp.int32, (tile, 1), 0) == 0
                pre = jnp.where(first, a * kr, a * kr + b * ki)
                pim = jnp.where(first, b * ki, b * kr - a * ki)
            else:
                pre = a * kr + b * ki
                pim = b * kr - a * ki
            pre_s[rows, :] = pre.astype(BF16)
            pim_s[rows, :] = pim.astype(BF16)

        def back(unit):
            _, trows, seq_rows = unit
            return (jnp.dot(cos[trows, :], pre_s[seq_rows, :], preferred_element_type=F32)
                    + jnp.dot(sin_t[trows, :], pim_s[seq_rows, :], preferred_element_type=F32))

        skewed(spectrum, multiply)
        skewed(back, lambda unit, y: epilogue(unit[0], y))

    def after_first(rows, y):
        z = hc[rows, HY_W:2 * HY_W] * (y + hc[rows, 0:HY_W] * bias[0:1, :])
        hc[rows, 0:HY_W] = z
        u[rows, :] = z.astype(BF16)

    def after_second(rows, y):
        o_ref[rows, :] = (hc[rows, 2 * HY_W:3 * HY_W] * (y + hc[rows, 0:HY_W] * bias[1:2, :])).astype(BF16)

    conv(0, after_first)
    conv(1, after_second)


def _hyena_call(hy_in, cw, cb, bias, kre, kim, mats, nb, sl, n_seq):
    assert nb % n_seq == 0
    seq = lambda b: (b, 0)
    mat = _resident((sl, sl))
    rows = n_seq * sl
    return pl.pallas_call(
        functools.partial(_hyena_kernel, sl_len=sl, n_seq=n_seq),
        grid=(nb // n_seq,),
        in_specs=[
            pl.BlockSpec((rows, 3 * HY_W), seq),
            _resident((3, 3 * HY_W)), _resident((1, 3 * HY_W)), _resident((2, HY_W)),
            _resident((2, sl, HY_W)), _resident((2, sl, HY_W)),
            mat, mat, mat,
        ],
        out_specs=pl.BlockSpec((rows, HY_W), seq),
        out_shape=jax.ShapeDtypeStruct((nb * sl, HY_W), BF16),
        scratch_shapes=[pltpu.VMEM((rows, 3 * HY_W), F32)] + [pltpu.VMEM((rows, HY_W), BF16)] * 3,
        compiler_params=_params(1),
        name="hyena_conv",
    )(hy_in, cw, cb, bias, kre, kim, *mats)


SSM_PAIRS = SSM_HEADS // 2
PAIR_W = 2 * SSM_HEAD_DIM


def _ssd_kernel(xbc_ref, dt_ref, h0_ref, cw, cb, dtb, alog, dsk, expand, o_ref, hfin_ref, xbc, dt, dta, hst, y_bwd, *,
                sl_len):
    nc = sl_len // SSM_CHUNK
    q = SSM_CHUNK
    xbc[...] = _silu(_dwconv(xbc_ref[...], cw[...], cb[...], sl_len))
    dtv = jax.nn.softplus(dt_ref[...] + dtb[...])
    dt[...] = dtv
    dta[...] = dtv * (-jnp.exp(alog[...]))
    hst[...] = h0_ref[0]
    ii = lax.broadcasted_iota(jnp.int32, (q, q), 0)
    jj = lax.broadcasted_iota(jnp.int32, (q, q), 1)
    low_lanes = lax.broadcasted_iota(jnp.int32, (1, PAIR_W), 1) < SSM_HEAD_DIM
    low_rows = lax.broadcasted_iota(jnp.int32, (2 * SSM_STATE, 1), 0) < SSM_STATE

    def prepare(c, d):
        rows = pl.ds(pl.multiple_of(c * q, q), q)
        xs = xbc[rows, 0:SSM_W]
        bm = xbc[rows, SSM_W:SSM_W + SSM_GROUPS * SSM_STATE]
        cm = xbc[rows, SSM_W + SSM_GROUPS * SSM_STATE:XBC_W]
        tri = (jj <= ii) if d == 0 else (jj >= ii)
        trib = tri.astype(BF16)
        a1, a2, a3 = _split3(dta[rows, :])
        acum = (jnp.dot(trib, a1, preferred_element_type=F32) + jnp.dot(trib, a2, preferred_element_type=F32)
                + jnp.dot(trib, a3, preferred_element_type=F32))
        dtc = dt[rows, :]
        last = acum[q - 1:q, :] if d == 0 else acum[0:1, :]
        w1, w2 = _split2(dtc * jnp.exp(last - acum))
        w_wide = (jnp.dot(w1, expand[d], preferred_element_type=F32)
                  + jnp.dot(w2, expand[d], preferred_element_type=F32))
        bm_t = bm.T.astype(BF16)
        cb_t = [_bdot(jnp.where(low_lanes, cm, 0.0), bm_t), _bdot(jnp.where(low_lanes, 0.0, cm), bm_t)]
        return dict(d=d, rows=rows, xs=xs, cm=cm, tri=tri, acum=acum, acum_t=acum.T, dt_t=dtc.T,
                    cdec=jnp.exp(last), xdd=xs * w_wide, bm_t=bm_t, cb_t=cb_t)

    def step(c, carry):
        chunks = [prepare(c, 0), prepare(nc - 1 - c, 1)]
        units = [(ch, p) for ch in chunks for p in range(SSM_PAIRS)]
        scores, grow = [], []
        for ch, p in units:
            g = p // (SSM_PAIRS // SSM_GROUPS)
            sc, gr = [], []
            for hh in range(2):
                col = ch["d"] * SSM_HEADS + 2 * p + hh
                a_col = jnp.broadcast_to(ch["acum"][:, col:col + 1], (q, q))
                lmat = jnp.exp(jnp.where(ch["tri"], a_col - ch["acum_t"][col:col + 1, :], -jnp.inf))
                sc.append((ch["cb_t"][g] * lmat * ch["dt_t"][col:col + 1, :]).astype(BF16))
                gr.append(jnp.exp(a_col))
            scores.append(jnp.concatenate(sc, axis=1))
            grow.append(jnp.where(low_lanes, gr[0], gr[1]))
        y_diag, y_off, states = [], [], []
        for (ch, p), sc in zip(units, scores):
            xp = ch["xs"][:, p * PAIR_W:(p + 1) * PAIR_W]
            x_blocks = jnp.concatenate([jnp.where(low_lanes, xp, 0.0), jnp.where(low_lanes, 0.0, xp)], axis=0)
            y_diag.append(jnp.dot(sc, x_blocks.astype(BF16), preferred_element_type=F32))
        for ch, p in units:
            y_off.append(_bdot(ch["cm"], hst[ch["d"] * SSM_PAIRS + p]))
            states.append(_bdot(ch["bm_t"], ch["xdd"][:, p * PAIR_W:(p + 1) * PAIR_W]))
        for i, (ch, p) in enumerate(units):
            slab = ch["d"] * SSM_PAIRS + p
            col = ch["d"] * SSM_HEADS + 2 * p
            cd = jnp.where(low_lanes, ch["cdec"][:, col:col + 1], ch["cdec"][:, col + 1:col + 2])
            own_rows = low_rows if p < SSM_PAIRS // SSM_GROUPS else jnp.logical_not(low_rows)
            hst[slab] = hst[slab] * cd + jnp.where(own_rows, states[i], 0.0)
        for k, ch in enumerate(chunks):
            y = jnp.concatenate([y_diag[k * SSM_PAIRS + p] + y_off[k * SSM_PAIRS + p] * grow[k * SSM_PAIRS + p]
                                 for p in range(SSM_PAIRS)], axis=1)
            if ch["d"] == 0:
                o_ref[ch["rows"], :] = y + dsk[...] * ch["xs"]
            else:
                y_bwd[ch["rows"], :] = y
        return carry

    lax.fori_loop(0, nc, step, 0, unroll=2)
    o_ref[...] += y_bwd[...]
    for slab in range(2 * SSM_PAIRS):
        g = (slab % SSM_PAIRS) // (SSM_PAIRS // SSM_GROUPS)
        st_t = hst[slab].T
        for hh in range(2):
            hfin_ref[0, 2 * slab + hh] = st_t[hh * SSM_HEAD_DIM:(hh + 1) * SSM_HEAD_DIM,
                                              g * SSM_STATE:(g + 1) * SSM_STATE]


def _ssd_call(s_xbc, s_dt, h0, cw, cb, dtb, alog, dsk, expand, nb, sl, shared_h0):
    seq = lambda b: (b, 0)
    st_shape = (1, 2 * SSM_PAIRS, 2 * SSM_STATE, PAIR_W)
    fin_shape = (1, 2 * SSM_HEADS, SSM_HEAD_DIM, SSM_STATE)
    h0_map =(lambda b: (0, 0, 0, 0)) if shared_h0 else (lambda b: (b, 0, 0, 0))
    return pl.pallas_call(
        functools.partial(_ssd_kernel, sl_len=sl),
        grid=(nb,),
        in_specs=[
            pl.BlockSpec((sl, XBC_W), seq), pl.BlockSpec((sl, LANES), seq), pl.BlockSpec(st_shape, h0_map),
            _resident((3, XBC_W)), _resident((1, XBC_W)), _resident((1, LANES)), _resident((1, LANES)),
            _resident((1, SSM_W)), _resident((2, LANES, SSM_W)),
        ],
        out_specs=[pl.BlockSpec((sl, SSM_W), seq), pl.BlockSpec(fin_shape, lambda b: (b, 0, 0, 0))],
        out_shape=[jax.ShapeDtypeStruct((nb * sl, SSM_W), F32), jax.ShapeDtypeStruct((nb, *fin_shape[1:]), F32)],
        scratch_shapes=[pltpu.VMEM((sl, XBC_W), F32), pltpu.VMEM((sl, LANES), F32), pltpu.VMEM((sl, LANES), F32),
                        pltpu.VMEM(st_shape[1:], F32), pltpu.VMEM((sl, SSM_W), F32)],
        compiler_params=_params(1),
        name="ssd_scan",
    )(s_xbc, s_dt, h0, cw, cb, dtb, alog, dsk, expand)


def _pack_states(h):
    nb = h.shape[0]
    t = h.reshape(nb, 2, SSM_PAIRS, 2, SSM_HEAD_DIM, SSM_STATE).transpose(0, 1, 2, 5, 3, 4)
    t = t.reshape(nb, 2, SSM_PAIRS, SSM_STATE, PAIR_W)
    z = jnp.zeros_like(t)
    first = (jnp.arange(SSM_PAIRS) < SSM_PAIRS // SSM_GROUPS)[None, None, :, None, None]
    slabs = jnp.where(first, jnp.concatenate([t, z], axis=3), jnp.concatenate([z, t], axis=3))
    return slabs.reshape(nb, 2 * SSM_PAIRS, 2 * SSM_STATE, PAIR_W)


def _expand_table():
    r = jnp.arange(LANES)[:, None]
    head = jnp.arange(SSM_W)[None, :] // SSM_HEAD_DIM
    return jnp.stack([r == head, r == SSM_HEADS + head]).astype(BF16)


def _attend_groups(units):
    logits = []
    for keys, q_t, _, _, g, _ in units:
        z = jnp.zeros_like(q_t)
        rhs = jnp.concatenate([q_t, z] if g == 0 else [z, q_t], axis=0)
        logits.append(jnp.dot(keys, rhs, preferred_element_type=F32))
    logits = [s if u[5] is None else jnp.where(u[5], s, -jnp.inf) for s, u in zip(logits, units)]
    sinks = [u[3] for u in units]
    tops = [jnp.maximum(jnp.max(s, axis=0, keepdims=True), sk) for s, sk in zip(logits, sinks)]
    exps = [jnp.exp(s - m) for s, m in zip(logits, tops)]
    denoms = [jnp.sum(e, axis=0, keepdims=True) + jnp.exp(sk - m) for e, m, sk in zip(exps, tops, sinks)]
    return [jnp.dot(u[2], e.astype(BF16), preferred_element_type=F32) / d for u, e, d in zip(units, exps, denoms)]


def _ctx_attn_kernel(q_ref, k_ref, v_ref, sink_ref, o_ref, *, sl_len, n_seq):
    scale = HEAD_DIM ** -0.5
    units = []
    for s in range(n_seq):
        rows = slice(s * sl_len, (s + 1) * sl_len)
        q_t = (q_ref[rows, :] * scale).T.astype(BF16)
        keys = k_ref[rows, :].astype(BF16)
        v_t = v_ref[rows, :].T.astype(BF16)
        for g in range(KV_HEADS):
            q_g = jnp.concatenate([q_t[(g * Q_PER_KV + hq) * HEAD_DIM:(g * Q_PER_KV + hq + 1) * HEAD_DIM, :]
                                   for hq in range(Q_PER_KV)], axis=1)
            units.append((keys, q_g, v_t[g * HEAD_DIM:(g + 1) * HEAD_DIM, :], sink_ref[g], g, None))
    o_ts = _attend_groups(units)
    for s in range(n_seq):
        outs = [o_t[:, hq * sl_len:(hq + 1) * sl_len] for o_t in o_ts[s * KV_HEADS:(s + 1) * KV_HEADS]
                for hq in range(Q_PER_KV)]
        o_ref[s * sl_len:(s + 1) * sl_len, :] = jnp.concatenate(outs, axis=0).T.astype(BF16)


def _ctx_attn_call(q, k, v, sink_rows, nb, sl, n_seq):
    assert nb % n_seq == 0
    seq = lambda b: (b, 0)
    rows = n_seq * sl
    return pl.pallas_call(
        functools.partial(_ctx_attn_kernel, sl_len=sl, n_seq=n_seq),
        grid=(nb // n_seq,),
        in_specs=[pl.BlockSpec((rows, ATT_W), seq), pl.BlockSpec((rows, KV_W), seq), pl.BlockSpec((rows, KV_W), seq),
                  _resident((KV_HEADS, 1, Q_PER_KV * sl))],
        out_specs=pl.BlockSpec((rows, ATT_W), seq),
        out_shape=jax.ShapeDtypeStruct((nb * sl, ATT_W), BF16),
        compiler_params=_params(1),
        name="ctx_attention",
    )(q, k, v, sink_rows)


def _rope(x, cos, sin_signed):
    w = x.shape[1]
    lane = lax.broadcasted_iota(jnp.int32, (1, w), 1)
    partner = jnp.where(lane % 32 < 16, pltpu.roll(x, w - 16, 1), pltpu.roll(x, 16, 1))
    return x * cos + partner * sin_signed


def _lat_attn_kernel(q_ref, k_ref, v_ref, ck_ref, cv_ref, cos_ref, sin_ref, sink_ref, o_ref, q_t, kcat, v_t, o_t, *,
                     sl_len, past):
    blk = ATT_BLOCK
    nblk = sl_len // blk
    scale = HEAD_DIM ** -0.5
    q_t[...] = (_rope(q_ref[...], cos_ref[...], sin_ref[...]) * scale).T.astype(BF16)
    kcat[0:past, :] = ck_ref[0].astype(BF16)
    kcat[past:, :] = _rope(k_ref[...], cos_ref[:, 0:KV_W], sin_ref[:, 0:KV_W]).astype(BF16)
    v_t[:, 0:past] = cv_ref[0].T.astype(BF16)
    v_t[:, past:] = v_ref[...].T.astype(BF16)
    for i0 in range(0, nblk, LAT_BLOCKS_PER_ROUND):
        units, places = [], []
        for i in range(i0, min(i0 + LAT_BLOCKS_PER_ROUND, nblk)):
            cols = slice(i * blk, (i + 1) * blk)
            lo, hi = max(i - 1, 0), min(i + 2, nblk)
            band = slice(past + lo * blk, past + hi * blk)
            n_keys = past + (hi - lo) * blk
            kk = lax.broadcasted_iota(jnp.int32, (n_keys, 1), 0)
            qi = lax.broadcasted_iota(jnp.int32, (1, Q_PER_KV * blk), 1) % blk
            rel = kk - past + (lo - i) * blk
            mask = (kk < past) | (jnp.abs(qi - rel) <= WINDOW)
            keys = jnp.concatenate([kcat[0:past, :], kcat[band, :]], axis=0)
            for g in range(KV_HEADS):
                drows = slice(g * HEAD_DIM, (g + 1) * HEAD_DIM)
                q_g = jnp.concatenate([q_t[(g * Q_PER_KV + hq) * HEAD_DIM:(g * Q_PER_KV + hq + 1) * HEAD_DIM, cols]
                                       for hq in range(Q_PER_KV)], axis=1)
                vals = jnp.concatenate([v_t[drows, 0:past], v_t[drows, band]], axis=1)
                units.append((keys, q_g, vals, sink_ref[g], g, mask))
                places.append((g, cols))
        for (g, cols), out in zip(places, _attend_groups(units)):
            for hq in range(Q_PER_KV):
                h = g * Q_PER_KV + hq
                o_t[h * HEAD_DIM:(h + 1) * HEAD_DIM, cols] = out[:, hq * blk:(hq + 1) * blk]
    o_ref[...] = o_t[...].T.astype(BF16)


def _lat_attn_call(q, k, v, ck, cv, cos, sin, sink_rows, nb, sl):
    seq = lambda b: (b, 0)
    past = ck.shape[1]
    cache = pl.BlockSpec((1, past, KV_W), lambda b: (b, 0, 0))
    return pl.pallas_call(
        functools.partial(_lat_attn_kernel, sl_len=sl, past=past),
        grid=(nb,),
        in_specs=[pl.BlockSpec((sl, ATT_W), seq), pl.BlockSpec((sl, KV_W), seq), pl.BlockSpec((sl, KV_W), seq),
                  cache, cache, _resident((sl, ATT_W)), _resident((sl, ATT_W)),
                  _resident((KV_HEADS, 1, Q_PER_KV * ATT_BLOCK))],
        out_specs=pl.BlockSpec((sl, ATT_W), seq),
        out_shape=jax.ShapeDtypeStruct((nb * sl, ATT_W), BF16),
        scratch_shapes=[pltpu.VMEM((ATT_W, sl), BF16), pltpu.VMEM((past + sl, KV_W), BF16),
                        pltpu.VMEM((KV_W, past + sl), BF16), pltpu.VMEM((ATT_W, sl), F32)],
        compiler_params=_params(1),
        name="latent_attention",
    )(q, k, v, ck, cv, cos, sin, sink_rows)


def _sink_rows(sink, n_q):
    return jnp.repeat(sink.reshape(KV_HEADS, Q_PER_KV), n_q, axis=1).reshape(KV_HEADS, 1, Q_PER_KV * n_q)


def _post_kernel(x_ref, hy_ref, sy_ref, sz_ref, at_ref, g_ref, mod_ref, snorm, npost, nfpre, nfpost,
                 wh, ws, wa, wo, wg, wu, wd, o_ref):
    m = mod_ref[0]
    halves = [slice(r * (ROW_TILE // 2), (r + 1) * (ROW_TILE // 2)) for r in range(2)]
    ssm_y = [_rms(sy_ref[r, :] * _silu(sz_ref[r, :]), snorm[...]) for r in halves]
    p_hy = [_bdot(hy_ref[r, :], wh[...]) for r in halves]
    p_ss = [_bdot(y, ws[...]) for y in ssm_y]
    p_at = [_bdot(at_ref[r, :], wa[...]) for r in halves]
    merged = [jax.nn.sigmoid(g_ref[r, 0:D_MODEL]) * a + jax.nn.sigmoid(g_ref[r, D_MODEL:2 * D_MODEL]) * b
              + jax.nn.sigmoid(g_ref[r, 2 * D_MODEL:3 * D_MODEL]) * c for r, a, b, c in zip(halves, p_hy, p_ss, p_at)]
    mix = [_bdot(v, wo[...]) for v in merged]
    x1 = [x_ref[r, :] + m[2:3, :] * _rms(v, npost[...]) for r, v in zip(halves, mix)]
    hb = [(_rms(v, nfpre[...]) * (1.0 + m[4:5, :]) + m[3:4, :]).astype(BF16) for v in x1]
    f = [None, None]
    for lo, hi in FFN_CHUNKS:
        gate = [jnp.dot(v, wg[:, lo:hi], preferred_element_type=F32) for v in hb]
        up = [jnp.dot(v, wu[:, lo:hi], preferred_element_type=F32) for v in hb]
        part = [_bdot(_silu(a) * b, wd[lo:hi, :]) for a, b in zip(gate, up)]
        f = [p if acc is None else acc + p for acc, p in zip(f, part)]
    for r, a, b in zip(halves, x1, f):
        o_ref[r, :] = a + m[5:6, :] * _rms(b, nfpost[...])


def _post_call(x, hy, sy, sz, at, g, mod, snorm, npost, nfpre, nfpost, wh, ws, wa, wo, wg, wu, wd, layer, nb, sl,
               shared_mod):
    n_tok = nb * sl
    assert n_tok % ROW_TILE == 0
    tok = lambda wdt: pl.BlockSpec((ROW_TILE, wdt), lambda i: (i, 0))
    vec = lambda wdt: _resident((1, wdt))
    return pl.pallas_call(
        _post_kernel,
        grid=(n_tok // ROW_TILE,),
        in_specs=[tok(D_MODEL), tok(HY_W), tok(SSM_W), tok(SSM_W), tok(ATT_W), tok(3 * D_MODEL),
                  _mod_spec(shared_mod, sl),
                  vec(SSM_W), vec(D_MODEL), vec(D_MODEL), vec(D_MODEL),
                  _resident((HY_W, D_MODEL), layer), _resident((SSM_W, D_MODEL), layer),
                  _resident((ATT_W, D_MODEL), layer), _resident((D_MODEL, D_MODEL), layer),
                  _resident((D_MODEL, FFN_HIDDEN), layer), _resident((D_MODEL, FFN_HIDDEN), layer),
                  _resident((FFN_HIDDEN, D_MODEL), layer)],
        out_specs=tok(D_MODEL),
        out_shape=jax.ShapeDtypeStruct((n_tok, D_MODEL), F32),
        compiler_params=_params(1),
        name="merge_ffn",
    )(x, hy, sy, sz, at, g, mod, snorm, npost, nfpre, nfpost, wh, ws, wa, wo, wg, wu, wd)


def _dft_mats(sl):
    radix = 32
    assert sl % radix == 0
    s = jnp.arange(sl, dtype=jnp.int32)

    def angles(f):
        return ((f[:, None] * s[None, :]) % (2 * sl)).astype(F32) * (math.pi / sl)

    a_hi = angles(jnp.arange(sl // radix, dtype=jnp.int32) * radix)[:, None, :]
    a_lo = angles(jnp.arange(radix, dtype=jnp.int32))[None, :, :]
    cos = (jnp.cos(a_hi) * jnp.cos(a_lo) - jnp.sin(a_hi) * jnp.sin(a_lo)).reshape(sl, sl)
    sin = (jnp.sin(a_hi) * jnp.cos(a_lo) + jnp.cos(a_hi) * jnp.sin(a_lo)).reshape(sl, sl)
    sign = jnp.where(s % 2 == 1, -1.0, 1.0).astype(F32)
    sinp = jnp.where(s[:, None] == 0, sign[None, :], sin).astype(BF16)
    return cos.astype(BF16), sinp, sinp.T


def _filter_features(sl):
    t = jnp.arange(sl, dtype=F32) / sl
    ang = 2.0 * jnp.pi * t[:, None] * jnp.arange(1, HY_BANDS + 1, dtype=F32)[None]
    feat = jnp.concatenate([t[:, None], jnp.cos(ang), jnp.sin(ang)], axis=-1)
    return jnp.pad(feat, ((0, 0), (0, LANES - HY_FEAT)))


def _rope_tables(sl):
    rows = sl // GRID_W
    row = jnp.repeat(jnp.arange(rows), GRID_W).astype(F32)
    col = jnp.tile(jnp.arange(GRID_W), rows).astype(F32)
    nq = HEAD_DIM // 4
    inv = ROPE_BASE ** (-jnp.arange(nq, dtype=F32) / nq)
    ar = row[:, None] * inv
    ac = col[:, None] * inv
    cos = jnp.concatenate([jnp.cos(ar), jnp.cos(ar), jnp.cos(ac), jnp.cos(ac)], axis=-1)
    sin = jnp.concatenate([-jnp.sin(ar), jnp.sin(ar), -jnp.sin(ac), jnp.sin(ac)], axis=-1)
    return jnp.tile(cos, (1, N_HEADS)), jnp.tile(sin, (1, N_HEADS))


def _pad_lanes(v):
    return jnp.pad(v.reshape(1, -1), ((0, 0), (0, LANES - v.size)))


def kernel(x_prompt, x_sample, c, cache_k, cache_v, state_ssm, c_ctx, ada_w, ada_b,
           norm_mix_pre, norm_mix_post, norm_ffn_pre, norm_ffn_post, w_in,
           hy_conv_w, hy_conv_b, hy_w1, hy_b1, hy_freq1, hy_w2, hy_b2, hy_freq2, hy_w3,
           hy_decay, hy_bias, ssm_conv_w, ssm_conv_b, ssm_a_log, ssm_dt_bias, ssm_d, ssm_norm,
           attn_sink, hy_proj, ssm_proj, attn_proj, w_out, ffn_w_gate, ffn_w_up, ffn_w_down):
    nb_c, sl_c, _ = x_prompt.shape
    nb_l, sl_l, _ = x_sample.shape
    past = cache_k.shape[2]

    cond = jnp.concatenate([c_ctx[None], c, jnp.zeros((8 - 1 - nb_l, D_MODEL), F32)], axis=0)
    mod = _ada_call(cond, ada_w, ada_b).reshape(DEPTH, 8, 6, D_MODEL)

    streams = {}
    for name, x in (("ctx", x_prompt), ("lat", x_sample)):
        nb, sl, _ = x.shape
        streams[name] = dict(nb=nb, sl=sl, x=x.reshape(nb * sl, D_MODEL), mats=_dft_mats(sl),
                             feat=_filter_features(sl))
    rope_cos, rope_sin = _rope_tables(sl_l)
    ctx_per_step = max(1, min(nb_c, SEQ_ROWS_PER_STEP // sl_c))
    while nb_c % ctx_per_step:
        ctx_per_step -= 1
    zero_state = jnp.zeros((1, 2 * SSM_PAIRS, 2 * SSM_STATE, PAIR_W), F32)
    expand = _expand_table()
    row = lambda v: v.reshape(1, -1)

    w_in_t = jnp.swapaxes(w_in, 1, 2).astype(BF16)
    wh, ws, wa, wo = (w.astype(BF16) for w in (hy_proj, ssm_proj, attn_proj, w_out))
    wg, wu, wd = (w.astype(BF16) for w in (ffn_w_gate, ffn_w_up, ffn_w_down))

    ks, vs, ss = [], [], []
    for l in range(DEPTH):
        w1p = jnp.pad(hy_w1[l], ((0, LANES - HY_FEAT), (0, 0)))
        dtb = _pad_lanes(ssm_dt_bias[l])
        alog = _pad_lanes(ssm_a_log[l])
        dsk = jnp.repeat(ssm_d[l, 0] + ssm_d[l, 1], SSM_HEAD_DIM).reshape(1, SSM_W)
        sink = {"ctx": _sink_rows(attn_sink[l], sl_c), "lat": _sink_rows(attn_sink[l], ATT_BLOCK)}
        for name in ("ctx", "lat"):
            st = streams[name]
            nb, sl, is_ctx = st["nb"], st["sl"], name == "ctx"
            m = mod[l, 0:1] if is_ctx else mod[l, 1:1 + nb]
            kre, kim = _filter_call(st["feat"], w1p, row(hy_b1[l]), row(hy_freq1[l]), hy_w2[l], row(hy_b2[l]),
                                    row(hy_freq2[l]), hy_w3[l], row(hy_decay[l]), st["mats"][0], st["mats"][1], sl)
            hy_in, s_z, s_xbc, s_dt, a_q, a_k, a_v, gate = _in_call(
                st["x"], m, row(norm_mix_pre[l]), w_in_t, l, nb, sl, is_ctx)
            hy_y = _hyena_call(hy_in, hy_conv_w[l], row(hy_conv_b[l]), hy_bias[l], kre, kim, st["mats"], nb, sl,
                               ctx_per_step if is_ctx else 1)
            h0 = zero_state if is_ctx else _pack_states(state_ssm[:, l])
            ssm_y, h_fin = _ssd_call(s_xbc, s_dt, h0, ssm_conv_w[l], row(ssm_conv_b[l]), dtb, alog, dsk, expand,
                                     nb, sl, is_ctx)
            if is_ctx:
                att = _ctx_attn_call(a_q, a_k, a_v, sink[name], nb, sl, ctx_per_step)
                ks.append(a_k.reshape(nb, sl, KV_HEADS, HEAD_DIM))
                vs.append(a_v.reshape(nb, sl, KV_HEADS, HEAD_DIM))
                ss.append(h_fin.reshape(nb, 2, SSM_HEADS, SSM_HEAD_DIM, SSM_STATE))
            else:
                att = _lat_attn_call(a_q, a_k, a_v, cache_k[:, l].reshape(nb, past, KV_W),
                                     cache_v[:, l].reshape(nb, past, KV_W), rope_cos, rope_sin, sink[name], nb, sl)
            st["x"] = _post_call(st["x"], hy_y, ssm_y, s_z, att, gate, m, row(ssm_norm[l]), row(norm_mix_post[l]),
                                 row(norm_ffn_pre[l]), row(norm_ffn_post[l]), wh, ws, wa, wo, wg, wu, wd, l, nb, sl,
                                 is_ctx)

    y_prompt = streams["ctx"]["x"].reshape(nb_c, sl_c, D_MODEL)
    y_sample = streams["lat"]["x"].reshape(nb_l, sl_l, D_MODEL)
    return (y_prompt, y_sample, jnp.stack(ks, axis=1), jnp.stack(vs, axis=1), jnp.stack(ss, axis=1))
```

```python
import functools
import math

import jax
import jax.numpy as jnp
from jax import lax
from jax.experimental import pallas as pl
from jax.experimental.pallas import tpu as pltpu

F32 = jnp.float32
BF16 = jnp.bfloat16

D_MODEL = 1024
DEPTH = 2
GRID_W = 64
HY_W = 512
HY_BANDS = 16
HY_FEAT = 1 + 2 * HY_BANDS
HY_HIDDEN = 64
SSM_W = 512
SSM_HEAD_DIM = 64
SSM_HEADS = 8
SSM_GROUPS = 2
SSM_STATE = 64
SSM_CHUNK = 128
XBC_W = SSM_W + 2 * SSM_GROUPS * SSM_STATE
N_HEADS = 8
KV_HEADS = 2
HEAD_DIM = 64
Q_PER_KV = N_HEADS // KV_HEADS
ATT_W = N_HEADS * HEAD_DIM
KV_W = KV_HEADS * HEAD_DIM
WINDOW = 128
ATT_BLOCK = 128
ROPE_BASE = 10000.0
FFN_HIDDEN = ((8 * D_MODEL + 3 * 256 - 1) // (3 * 256)) * 256
EPS = 1e-6

LANES = 128
ROW_TILE = 512
DFT_TILE = 256
HY_GROUPS = 2
SEQ_ROWS_PER_STEP = 1024
LAT_BLOCKS_PER_ROUND = 2
MXU_TILE = 256
FFN_SPLIT = (FFN_HIDDEN // MXU_TILE + 1) // 2 * MXU_TILE
FFN_CHUNKS = ((0, FFN_SPLIT), (FFN_SPLIT, FFN_HIDDEN))
IN_SPLIT = (3 * HY_W, SSM_W, XBC_W, 2 * SSM_HEADS, ATT_W, KV_W, KV_W, 3 * D_MODEL)
IN_WIDTHS = (3 * HY_W, SSM_W, XBC_W, LANES, ATT_W, KV_W, KV_W, 3 * D_MODEL)
VMEM_LIMIT = 56 * 1024 * 1024


def _params(n_grid):
    return pltpu.CompilerParams(dimension_semantics=("arbitrary",) * n_grid, vmem_limit_bytes=VMEM_LIMIT)


def _resident(shape, layer=None):
    nd = len(shape)
    if layer is None:
        return pl.BlockSpec(shape, lambda *_: (0,) * nd, pipeline_mode=pl.Buffered(1))
    return pl.BlockSpec((None, *shape), lambda *_: (layer,) + (0,) * nd, pipeline_mode=pl.Buffered(1))


def _bdot(a, b):
    return jnp.dot(a.astype(BF16), b.astype(BF16), preferred_element_type=F32)


def _dot_nt(a, b):
    return lax.dot_general(a.astype(BF16), b.astype(BF16), (((1,), (1,)), ((), ())), preferred_element_type=F32)


def _dot_tn(a, b):
    return lax.dot_general(a.astype(BF16), b.astype(BF16), (((0,), (0,)), ((), ())), preferred_element_type=F32)


def _split2(x):
    hi = x.astype(BF16)
    lo = (x - hi.astype(F32)).astype(BF16)
    return hi, lo


def _split3(x):
    h1 = x.astype(BF16)
    r1 = x - h1.astype(F32)
    h2 = r1.astype(BF16)
    h3 = (r1 - h2.astype(F32)).astype(BF16)
    return h1, h2, h3


def _dot3(a, b):
    ah, al = _split2(a)
    bh, bl = _split2(b)
    return (jnp.dot(ah, bh, preferred_element_type=F32) + jnp.dot(ah, bl, preferred_element_type=F32)
            + jnp.dot(al, bh, preferred_element_type=F32))


def _rms(x, g):
    return x * lax.rsqrt(jnp.mean(x * x, axis=-1, keepdims=True) + EPS) * g


def _silu(x):
    return x * jax.nn.sigmoid(x)


def _dwconv(x, w, b, period):
    n = x.shape[0]
    row = lax.broadcasted_iota(jnp.int32, (n, 1), 0) % period
    prev = jnp.where(row == 0, 0.0, pltpu.roll(x, 1, 0))
    nxt = jnp.where(row == period - 1, 0.0, pltpu.roll(x, n - 1, 0))
    return prev * w[0:1, :] + x * w[1:2, :] + nxt * w[2:3, :] + b


def _ada_kernel(cond_ref, w_ref, b_ref, o_ref):
    o_ref[0] = _bdot(_silu(cond_ref[...]), w_ref[0]) + b_ref[0]


def _ada_call(cond, ada_w, ada_b):
    n = 6 * D_MODEL
    tn = n // 4
    rows = cond.shape[0]
    return pl.pallas_call(
        _ada_kernel,
        grid=(DEPTH, n // tn),
        in_specs=[
            pl.BlockSpec((rows, D_MODEL), lambda l, j: (0, 0)),
            pl.BlockSpec((1, D_MODEL, tn), lambda l, j: (l, 0, j)),
            pl.BlockSpec((1, 1, tn), lambda l, j: (l, 0, j)),
        ],
        out_specs=pl.BlockSpec((1, rows, tn), lambda l, j: (l, 0, j)),
        out_shape=jax.ShapeDtypeStruct((DEPTH, rows, n), F32),
        compiler_params=_params(2),
        name="adaln",
    )(cond, ada_w, ada_b.reshape(DEPTH, 1, n))


def _in_kernel(x_ref, mod_ref, g_ref, wt_ref, *o_refs):
    h = _rms(x_ref[...], g_ref[...]) * (1.0 + mod_ref[0, 1:2, :]) + mod_ref[0, 0:1, :]
    hb = h.astype(BF16)
    off = 0
    for o_ref, wd, padded in zip(o_refs, IN_SPLIT, IN_WIDTHS):
        out = lax.dot_general(hb, wt_ref[off:off + padded, :], (((1,), (1,)), ((), ())), preferred_element_type=F32)
        if padded != wd:
            out = jnp.where(lax.broadcasted_iota(jnp.int32, (1, padded), 1) < wd, out, 0.0)
        o_ref[...] = out
        off += wd


def _mod_spec(shared, sl):
    if shared:
        return pl.BlockSpec((1, 6, D_MODEL), lambda i: (0, 0, 0))
    assert sl % ROW_TILE == 0
    return pl.BlockSpec((1, 6, D_MODEL), lambda i: (i // (sl // ROW_TILE), 0, 0))


def _in_call(x, mod, g, wt, layer, nb, sl, shared_mod):
    n_tok = nb * sl
    assert n_tok % ROW_TILE == 0
    row = lambda i: (i, 0)
    return pl.pallas_call(
        _in_kernel,
        grid=(n_tok // ROW_TILE,),
        in_specs=[
            pl.BlockSpec((ROW_TILE, D_MODEL), row),
            _mod_spec(shared_mod, sl),
            _resident((1, D_MODEL)),
            _resident((sum(IN_SPLIT), D_MODEL), layer),
        ],
        out_specs=[pl.BlockSpec((ROW_TILE, wd), row) for wd in IN_WIDTHS],
        out_shape=[jax.ShapeDtypeStruct((n_tok, wd), F32) for wd in IN_WIDTHS],
        compiler_params=_params(1),
        name="in_proj",
    )(x, mod, g, wt)


def _filter_kernel(feat_ref, w1, b1, f1, w2, b2, f2, w3, dec, cos_ref, sin_ref, kre_ref, kim_ref, *, sl_len):
    feat = feat_ref[...]
    h = jnp.sin(f1[...] * (_dot3(feat, w1[...]) + b1[...]))
    h = jnp.sin(f2[...] * (_dot3(h, w2[...]) + b2[...]))
    t = feat[:, 0:1]
    row = lax.broadcasted_iota(jnp.int32, (sl_len, 1), 0)
    scale = jnp.where(row == 0, 0.5 / sl_len, 1.0 / sl_len)
    gw = HY_W // HY_GROUPS

    def taps(o, c):
        fwd = slice(2 * o * HY_W + c * gw, 2 * o * HY_W + (c + 1) * gw)
        bwd = slice((2 * o + 1) * HY_W + c * gw, (2 * o + 1) * HY_W + (c + 1) * gw)
        hf = _dot3(h, w3[:, fwd]) * jnp.exp(-t * dec[:, fwd])
        hb = jnp.where(row == 0, 0.0, _dot3(h, w3[:, bwd]) * jnp.exp(-t * dec[:, bwd]))
        return hf + hb, hb - hf

    def spectrum(o, c, ksum, kdiff):
        kre = _bdot(cos_ref[...], ksum)
        kim = _bdot(sin_ref[...], kdiff)
        nyq = jnp.sum(jnp.where(row % 2 == 1, -ksum, ksum), axis=0, keepdims=True)
        kre_ref[o, :, c * gw:(c + 1) * gw] = kre * scale
        kim_ref[o, :, c * gw:(c + 1) * gw] = jnp.where(row == 0, nyq, kim) * scale

    pending = None
    for o in range(2):
        for c in range(HY_GROUPS):
            ready = (o, c, *taps(o, c))
            if pending is not None:
                spectrum(*pending)
            pending = ready
    spectrum(*pending)


def _filter_call(feat, w1, b1, f1, w2, b2, f2, w3, dec, cos, sinp, sl):
    hid = HY_HIDDEN
    return pl.pallas_call(
        functools.partial(_filter_kernel, sl_len=sl),
        grid=(1,),
        in_specs=[
            _resident((sl, LANES)), _resident((LANES, hid)), _resident((1, hid)), _resident((1, hid)),
            _resident((hid, hid)), _resident((1, hid)), _resident((1, hid)),
            _resident((hid, 4 * HY_W)), _resident((1, 4 * HY_W)),
            _resident((sl, sl)), _resident((sl, sl)),
        ],
        out_specs=[pl.BlockSpec((2, sl, HY_W), lambda i: (0, 0, 0))] * 2,
        out_shape=[jax.ShapeDtypeStruct((2, sl, HY_W), F32)] * 2,
        compiler_params=_params(1),
        name="hyena_filter",
    )(feat, w1, b1, f1, w2, b2, f2, w3, dec, cos, sinp)


def _hyena_kernel(x_ref, cw, cb, bias, kre, kim, cos, sin, sin_t, o_ref, hc, u, pre_s, pim_s, *, sl_len, n_seq):
    tile = min(DFT_TILE, sl_len)
    hc[...] = _dwconv(x_ref[...], cw[...], cb[...], sl_len)
    u[...] = hc[:, 0:HY_W].astype(BF16)

    gw = HY_W // HY_GROUPS
    chains = [(s, c) for s in range(n_seq) for c in range(HY_GROUPS)]
    held = {}

    def phase(chain, k):
        s, c = chains[chain]
        order, kind = divmod(k, 4)
        cols = slice(c * gw, (c + 1) * gw)
        x1_cols = slice(HY_W + c * gw, HY_W + (c + 1) * gw)
        x2_cols = slice(2 * HY_W + c * gw, 2 * HY_W + (c + 1) * gw)
        seq_rows = slice(s * sl_len, (s + 1) * sl_len)

        def spectrum(t, trows, rows):
            held[chain, t] = (jnp.dot(cos[trows, :], u[seq_rows, cols], preferred_element_type=F32),
                              jnp.dot(sin[trows, :], u[seq_rows, cols], preferred_element_type=F32))

        def multiply(t, trows, rows):
            a, b = held.pop((chain, t))
            kr = kre[order, trows, cols]
            ki = kim[order, trows, cols]
            if t == 0:
                first = lax.broadcasted_iota(jnp.int32, (tile, 1), 0) == 0
                pre = jnp.where(first, a * kr, a * kr + b * ki)
                pim = jnp.where(first, b * ki, b * kr - a * ki)
            else:
                pre = a * kr + b * ki
                pim = b * kr - a * ki
            pre_s[rows, cols] = pre.astype(BF16)
            pim_s[rows, cols] = pim.astype(BF16)

        def back(t, trows, rows):
            held[chain, t] = (jnp.dot(cos[trows, :], pre_s[seq_rows, cols], preferred_element_type=F32)
                              + jnp.dot(sin_t[trows, :], pim_s[seq_rows, cols], preferred_element_type=F32))

        def gate(t, trows, rows):
            y = held.pop((chain, t)) + hc[rows, cols] * bias[order:order + 1, cols]
            if order == 0:
                z = hc[rows, x1_cols] * y
                hc[rows, cols] = z
                u[rows, cols] = z.astype(BF16)
            else:
                o_ref[rows, cols] = (hc[rows, x2_cols] * y).astype(BF16)

        body = (spectrum, multiply, back, gate)[kind]
        return [functools.partial(body, t, slice(t, t + tile), slice(s * sl_len + t, s * sl_len + t + tile))
                for t in range(0, sl_len, tile)]

    n_phase = 8
    for slot in range(n_phase + 1):
        ahead = [op for ch in range(0, len(chains), 2) if slot < n_phase for op in phase(ch, slot)]
        behind = [op for ch in range(1, len(chains), 2) if slot >= 1 for op in phase(ch, slot - 1)]
        for i in range(max(len(ahead), len(behind))):
            for ops in (ahead, behind):
                if i < len(ops):
                    ops[i]()


def _hyena_call(hy_in, cw, cb, bias, kre, kim, mats, nb, sl, n_seq):
    assert nb % n_seq == 0
    seq = lambda b: (b, 0)
    mat = _resident((sl, sl))
    rows = n_seq * sl
    return pl.pallas_call(
        functools.partial(_hyena_kernel, sl_len=sl, n_seq=n_seq),
        grid=(nb // n_seq,),
        in_specs=[
            pl.BlockSpec((rows, 3 * HY_W), seq),
            _resident((3, 3 * HY_W)), _resident((1, 3 * HY_W)), _resident((2, HY_W)),
            _resident((2, sl, HY_W)), _resident((2, sl, HY_W)),
            mat, mat, mat,
        ],
        out_specs=pl.BlockSpec((rows, HY_W), seq),
        out_shape=jax.ShapeDtypeStruct((nb * sl, HY_W), BF16),
        scratch_shapes=[pltpu.VMEM((rows, 3 * HY_W), F32)] + [pltpu.VMEM((rows, HY_W), BF16)] * 3,
        compiler_params=_params(1),
        name="hyena_conv",
    )(hy_in, cw, cb, bias, kre, kim, *mats)


SSM_PAIRS = SSM_HEADS // 2
PAIR_W = 2 * SSM_HEAD_DIM


def _ssd_kernel(xbc_ref, dt_ref, h0_ref, cw, cb, dtb, alog, dsk, expand, o_ref, hfin_ref, xbc, dt, dta, hst, y_bwd, *,
                sl_len):
    nc = sl_len // SSM_CHUNK
    q = SSM_CHUNK
    xbc[...] = _silu(_dwconv(xbc_ref[...], cw[...], cb[...], sl_len))
    dtv = jax.nn.softplus(dt_ref[...] + dtb[...])
    dt[...] = dtv
    dta[...] = dtv * (-jnp.exp(alog[...]))
    hst[...] = h0_ref[0]
    ii = lax.broadcasted_iota(jnp.int32, (q, q), 0)
    jj = lax.broadcasted_iota(jnp.int32, (q, q), 1)
    low_lanes = lax.broadcasted_iota(jnp.int32, (1, PAIR_W), 1) < SSM_HEAD_DIM
    low_rows = lax.broadcasted_iota(jnp.int32, (2 * SSM_STATE, 1), 0) < SSM_STATE

    def prepare(c, d):
        rows = pl.ds(pl.multiple_of(c * q, q), q)
        xs = xbc[rows, 0:SSM_W]
        bm = xbc[rows, SSM_W:SSM_W + SSM_GROUPS * SSM_STATE]
        cm = xbc[rows, SSM_W + SSM_GROUPS * SSM_STATE:XBC_W]
        tri = (jj <= ii) if d == 0 else (jj >= ii)
        trib = tri.astype(BF16)
        a1, a2, a3 = _split3(dta[rows, :])
        acum = (jnp.dot(trib, a1, preferred_element_type=F32) + jnp.dot(trib, a2, preferred_element_type=F32)
                + jnp.dot(trib, a3, preferred_element_type=F32))
        dtc = dt[rows, :]
        last = acum[q - 1:q, :] if d == 0 else acum[0:1, :]
        w1, w2 = _split2(dtc * jnp.exp(last - acum))
        w_wide = (jnp.dot(w1, expand[d], preferred_element_type=F32)
                  + jnp.dot(w2, expand[d], preferred_element_type=F32))
        bm_t = bm.T.astype(BF16)
        cb_t = [_bdot(jnp.where(low_lanes, cm, 0.0), bm_t), _bdot(jnp.where(low_lanes, 0.0, cm), bm_t)]
        return dict(d=d, rows=rows, xs=xs, cm=cm, tri=tri, acum=acum, acum_t=acum.T, dt_t=dtc.T,
                    cdec=jnp.exp(last), xdd=xs * w_wide, bm_t=bm_t, cb_t=cb_t)

    def step(c, carry):
        chunks = [prepare(c, 0), prepare(nc - 1 - c, 1)]
        units = [(ch, p) for ch in chunks for p in range(SSM_PAIRS)]
        scores, grow = [], []
        for ch, p in units:
            g = p // (SSM_PAIRS // SSM_GROUPS)
            sc, gr = [], []
            for hh in range(2):
                col = ch["d"] * SSM_HEADS + 2 * p + hh
                a_col = jnp.broadcast_to(ch["acum"][:, col:col + 1], (q, q))
                lmat = jnp.exp(jnp.where(ch["tri"], a_col - ch["acum_t"][col:col + 1, :], -jnp.inf))
                sc.append((ch["cb_t"][g] * lmat * ch["dt_t"][col:col + 1, :]).astype(BF16))
                gr.append(jnp.exp(a_col))
            scores.append(jnp.concatenate(sc, axis=1))
            grow.append(jnp.where(low_lanes, gr[0], gr[1]))
        y_diag, y_off, states = [], [], []
        for (ch, p), sc in zip(units, scores):
            xp = ch["xs"][:, p * PAIR_W:(p + 1) * PAIR_W]
            x_blocks = jnp.concatenate([jnp.where(low_lanes, xp, 0.0), jnp.where(low_lanes, 0.0, xp)], axis=0)
            y_diag.append(jnp.dot(sc, x_blocks.astype(BF16), preferred_element_type=F32))
        for ch, p in units:
            y_off.append(_bdot(ch["cm"], hst[ch["d"] * SSM_PAIRS + p]))
            states.append(_bdot(ch["bm_t"], ch["xdd"][:, p * PAIR_W:(p + 1) * PAIR_W]))
        for i, (ch, p) in enumerate(units):
            slab = ch["d"] * SSM_PAIRS + p
            col = ch["d"] * SSM_HEADS + 2 * p
            cd = jnp.where(low_lanes, ch["cdec"][:, col:col + 1], ch["cdec"][:, col + 1:col + 2])
            own_rows = low_rows if p < SSM_PAIRS // SSM_GROUPS else jnp.logical_not(low_rows)
            hst[slab] = hst[slab] * cd + jnp.where(own_rows, states[i], 0.0)
        for k, ch in enumerate(chunks):
            y = jnp.concatenate([y_diag[k * SSM_PAIRS + p] + y_off[k * SSM_PAIRS + p] * grow[k * SSM_PAIRS + p]
                                 for p in range(SSM_PAIRS)], axis=1)
            if ch["d"] == 0:
                o_ref[ch["rows"], :] = y + dsk[...] * ch["xs"]
            else:
                y_bwd[ch["rows"], :] = y
        return carry

    lax.fori_loop(0, nc, step, 0, unroll=2)
    o_ref[...] += y_bwd[...]
    for slab in range(2 * SSM_PAIRS):
        g = (slab % SSM_PAIRS) // (SSM_PAIRS // SSM_GROUPS)
        st_t = hst[slab].T
        for hh in range(2):
            hfin_ref[0, 2 * slab + hh] = st_t[hh * SSM_HEAD_DIM:(hh + 1) * SSM_HEAD_DIM,
                                              g * SSM_STATE:(g + 1) * SSM_STATE]


def _ssd_call(s_xbc, s_dt, h0, cw, cb, dtb, alog, dsk, expand, nb, sl, shared_h0):
    seq = lambda b: (b, 0)
    st_shape = (1, 2 * SSM_PAIRS, 2 * SSM_STATE, PAIR_W)
    fin_shape = (1, 2 * SSM_HEADS, SSM_HEAD_DIM, SSM_STATE)
    h0_map =(lambda b: (0, 0, 0, 0)) if shared_h0 else (lambda b: (b, 0, 0, 0))
    return pl.pallas_call(
        functools.partial(_ssd_kernel, sl_len=sl),
        grid=(nb,),
        in_specs=[
            pl.BlockSpec((sl, XBC_W), seq), pl.BlockSpec((sl, LANES), seq), pl.BlockSpec(st_shape, h0_map),
            _resident((3, XBC_W)), _resident((1, XBC_W)), _resident((1, LANES)), _resident((1, LANES)),
            _resident((1, SSM_W)), _resident((2, LANES, SSM_W)),
        ],
        out_specs=[pl.BlockSpec((sl, SSM_W), seq), pl.BlockSpec(fin_shape, lambda b: (b, 0, 0, 0))],
        out_shape=[jax.ShapeDtypeStruct((nb * sl, SSM_W), F32), jax.ShapeDtypeStruct((nb, *fin_shape[1:]), F32)],
        scratch_shapes=[pltpu.VMEM((sl, XBC_W), F32), pltpu.VMEM((sl, LANES), F32), pltpu.VMEM((sl, LANES), F32),
                        pltpu.VMEM(st_shape[1:], F32), pltpu.VMEM((sl, SSM_W), F32)],
        compiler_params=_params(1),
        name="ssd_scan",
    )(s_xbc, s_dt, h0, cw, cb, dtb, alog, dsk, expand)


def _pack_states(h):
    nb = h.shape[0]
    t = h.reshape(nb, 2, SSM_PAIRS, 2, SSM_HEAD_DIM, SSM_STATE).transpose(0, 1, 2, 5, 3, 4)
    t = t.reshape(nb, 2, SSM_PAIRS, SSM_STATE, PAIR_W)
    z = jnp.zeros_like(t)
    first = (jnp.arange(SSM_PAIRS) < SSM_PAIRS // SSM_GROUPS)[None, None, :, None, None]
    slabs = jnp.where(first, jnp.concatenate([t, z], axis=3), jnp.concatenate([z, t], axis=3))
    return slabs.reshape(nb, 2 * SSM_PAIRS, 2 * SSM_STATE, PAIR_W)


def _expand_table():
    r = jnp.arange(LANES)[:, None]
    head = jnp.arange(SSM_W)[None, :] // SSM_HEAD_DIM
    return jnp.stack([r == head, r == SSM_HEADS + head]).astype(BF16)


def _attend_groups(units):
    logits = []
    for keys, q_t, _, _, g, _ in units:
        z = jnp.zeros_like(q_t)
        rhs = jnp.concatenate([q_t, z] if g == 0 else [z, q_t], axis=0)
        logits.append(jnp.dot(keys, rhs, preferred_element_type=F32))
    logits = [s if u[5] is None else jnp.where(u[5], s, -jnp.inf) for s, u in zip(logits, units)]
    sinks = [u[3] for u in units]
    tops = [jnp.maximum(jnp.max(s, axis=0, keepdims=True), sk) for s, sk in zip(logits, sinks)]
    exps = [jnp.exp(s - m) for s, m in zip(logits, tops)]
    denoms = [jnp.sum(e, axis=0, keepdims=True) + jnp.exp(sk - m) for e, m, sk in zip(exps, tops, sinks)]
    return [jnp.dot(u[2], e.astype(BF16), preferred_element_type=F32) / d for u, e, d in zip(units, exps, denoms)]


def _ctx_attn_kernel(q_ref, k_ref, v_ref, sink_ref, o_ref, *, sl_len, n_seq):
    scale = HEAD_DIM ** -0.5
    units = []
    for s in range(n_seq):
        rows = slice(s * sl_len, (s + 1) * sl_len)
        q_t = (q_ref[rows, :] * scale).T.astype(BF16)
        keys = k_ref[rows, :].astype(BF16)
        v_t = v_ref[rows, :].T.astype(BF16)
        for g in range(KV_HEADS):
            q_g = jnp.concatenate([q_t[(g * Q_PER_KV + hq) * HEAD_DIM:(g * Q_PER_KV + hq + 1) * HEAD_DIM, :]
                                   for hq in range(Q_PER_KV)], axis=1)
            units.append((keys, q_g, v_t[g * HEAD_DIM:(g + 1) * HEAD_DIM, :], sink_ref[g], g, None))
    o_ts = _attend_groups(units)
    for s in range(n_seq):
        outs = [o_t[:, hq * sl_len:(hq + 1) * sl_len] for o_t in o_ts[s * KV_HEADS:(s + 1) * KV_HEADS]
                for hq in range(Q_PER_KV)]
        o_ref[s * sl_len:(s + 1) * sl_len, :] = jnp.concatenate(outs, axis=0).T.astype(BF16)


def _ctx_attn_call(q, k, v, sink_rows, nb, sl, n_seq):
    assert nb % n_seq == 0
    seq = lambda b: (b, 0)
    rows = n_seq * sl
    return pl.pallas_call(
        functools.partial(_ctx_attn_kernel, sl_len=sl, n_seq=n_seq),
        grid=(nb // n_seq,),
        in_specs=[pl.BlockSpec((rows, ATT_W), seq), pl.BlockSpec((rows, KV_W), seq), pl.BlockSpec((rows, KV_W), seq),
                  _resident((KV_HEADS, 1, Q_PER_KV * sl))],
        out_specs=pl.BlockSpec((rows, ATT_W), seq),
        out_shape=jax.ShapeDtypeStruct((nb * sl, ATT_W), BF16),
        compiler_params=_params(1),
        name="ctx_attention",
    )(q, k, v, sink_rows)


def _rope(x, cos, sin_signed):
    w = x.shape[1]
    lane = lax.broadcasted_iota(jnp.int32, (1, w), 1)
    partner = jnp.where(lane % 32 < 16, pltpu.roll(x, w - 16, 1), pltpu.roll(x, 16, 1))
    return x * cos + partner * sin_signed


def _lat_attn_kernel(q_ref, k_ref, v_ref, ck_ref, cv_ref, cos_ref, sin_ref, sink_ref, o_ref, q_t, kcat, v_t, o_t, *,
                     sl_len, past):
    blk = ATT_BLOCK
    nblk = sl_len // blk
    scale = HEAD_DIM ** -0.5
    q_t[...] = (_rope(q_ref[...], cos_ref[...], sin_ref[...]) * scale).T.astype(BF16)
    kcat[0:past, :] = ck_ref[0].astype(BF16)
    kcat[past:, :] = _rope(k_ref[...], cos_ref[:, 0:KV_W], sin_ref[:, 0:KV_W]).astype(BF16)
    v_t[:, 0:past] = cv_ref[0].T.astype(BF16)
    v_t[:, past:] = v_ref[...].T.astype(BF16)
    for i0 in range(0, nblk, LAT_BLOCKS_PER_ROUND):
        units, places = [], []
        for i in range(i0, min(i0 + LAT_BLOCKS_PER_ROUND, nblk)):
            cols = slice(i * blk, (i + 1) * blk)
            lo, hi = max(i - 1, 0), min(i + 2, nblk)
            band = slice(past + lo * blk, past + hi * blk)
            n_keys = past + (hi - lo) * blk
            kk = lax.broadcasted_iota(jnp.int32, (n_keys, 1), 0)
            qi = lax.broadcasted_iota(jnp.int32, (1, Q_PER_KV * blk), 1) % blk
            rel = kk - past + (lo - i) * blk
            mask = (kk < past) | (jnp.abs(qi - rel) <= WINDOW)
            keys = jnp.concatenate([kcat[0:past, :], kcat[band, :]], axis=0)
            for g in range(KV_HEADS):
                drows = slice(g * HEAD_DIM, (g + 1) * HEAD_DIM)
                q_g = jnp.concatenate([q_t[(g * Q_PER_KV + hq) * HEAD_DIM:(g * Q_PER_KV + hq + 1) * HEAD_DIM, cols]
                                       for hq in range(Q_PER_KV)], axis=1)
                vals = jnp.concatenate([v_t[drows, 0:past], v_t[drows, band]], axis=1)
                units.append((keys, q_g, vals, sink_ref[g], g, mask))
                places.append((g, cols))
        for (g, cols), out in zip(places, _attend_groups(units)):
            for hq in range(Q_PER_KV):
                h = g * Q_PER_KV + hq
                o_t[h * HEAD_DIM:(h + 1) * HEAD_DIM, cols] = out[:, hq * blk:(hq + 1) * blk]
    o_ref[...] = o_t[...].T.astype(BF16)


def _lat_attn_call(q, k, v, ck, cv, cos, sin, sink_rows, nb, sl):
    seq = lambda b: (b, 0)
    past = ck.shape[1]
    cache = pl.BlockSpec((1, past, KV_W), lambda b: (b, 0, 0))
    return pl.pallas_call(
        functools.partial(_lat_attn_kernel, sl_len=sl, past=past),
        grid=(nb,),
        in_specs=[pl.BlockSpec((sl, ATT_W), seq), pl.BlockSpec((sl, KV_W), seq), pl.BlockSpec((sl, KV_W), seq),
                  cache, cache, _resident((sl, ATT_W)), _resident((sl, ATT_W)),
                  _resident((KV_HEADS, 1, Q_PER_KV * ATT_BLOCK))],
        out_specs=pl.BlockSpec((sl, ATT_W), seq),
        out_shape=jax.ShapeDtypeStruct((nb * sl, ATT_W), BF16),
        scratch_shapes=[pltpu.VMEM((ATT_W, sl), BF16), pltpu.VMEM((past + sl, KV_W), BF16),
                        pltpu.VMEM((KV_W, past + sl), BF16), pltpu.VMEM((ATT_W, sl), F32)],
        compiler_params=_params(1),
        name="latent_attention",
    )(q, k, v, ck, cv, cos, sin, sink_rows)


def _sink_rows(sink, n_q):
    return jnp.repeat(sink.reshape(KV_HEADS, Q_PER_KV), n_q, axis=1).reshape(KV_HEADS, 1, Q_PER_KV * n_q)


def _post_kernel(x_ref, hy_ref, sy_ref, sz_ref, at_ref, g_ref, mod_ref, snorm, npost, nfpre, nfpost,
                 wh, ws, wa, wo, wg, wu, wd, o_ref):
    m = mod_ref[0]
    halves = [slice(r * (ROW_TILE // 2), (r + 1) * (ROW_TILE // 2)) for r in range(2)]
    ssm_y = [_rms(sy_ref[r, :] * _silu(sz_ref[r, :]), snorm[...]) for r in halves]
    p_hy = [_bdot(hy_ref[r, :], wh[...]) for r in halves]
    p_ss = [_bdot(y, ws[...]) for y in ssm_y]
    p_at = [_bdot(at_ref[r, :], wa[...]) for r in halves]
    merged = [jax.nn.sigmoid(g_ref[r, 0:D_MODEL]) * a + jax.nn.sigmoid(g_ref[r, D_MODEL:2 * D_MODEL]) * b
              + jax.nn.sigmoid(g_ref[r, 2 * D_MODEL:3 * D_MODEL]) * c for r, a, b, c in zip(halves, p_hy, p_ss, p_at)]
    mix = [_bdot(v, wo[...]) for v in merged]
    x1 = [x_ref[r, :] + m[2:3, :] * _rms(v, npost[...]) for r, v in zip(halves, mix)]
    hb = [(_rms(v, nfpre[...]) * (1.0 + m[4:5, :]) + m[3:4, :]).astype(BF16) for v in x1]
    f = [None, None]
    for lo, hi in FFN_CHUNKS:
        gate = [jnp.dot(v, wg[:, lo:hi], preferred_element_type=F32) for v in hb]
        up = [jnp.dot(v, wu[:, lo:hi], preferred_element_type=F32) for v in hb]
        part = [_bdot(_silu(a) * b, wd[lo:hi, :]) for a, b in zip(gate, up)]
        f = [p if acc is None else acc + p for acc, p in zip(f, part)]
    for r, a, b in zip(halves, x1, f):
        o_ref[r, :] = a + m[5:6, :] * _rms(b, nfpost[...])


def _post_call(x, hy, sy, sz, at, g, mod, snorm, npost, nfpre, nfpost, wh, ws, wa, wo, wg, wu, wd, layer, nb, sl,
               shared_mod):
    n_tok = nb * sl
    assert n_tok % ROW_TILE == 0
    tok = lambda wdt: pl.BlockSpec((ROW_TILE, wdt), lambda i: (i, 0))
    vec = lambda wdt: _resident((1, wdt))
    return pl.pallas_call(
        _post_kernel,
        grid=(n_tok // ROW_TILE,),
        in_specs=[tok(D_MODEL), tok(HY_W), tok(SSM_W), tok(SSM_W), tok(ATT_W), tok(3 * D_MODEL),
                  _mod_spec(shared_mod, sl),
                  vec(SSM_W), vec(D_MODEL), vec(D_MODEL), vec(D_MODEL),
                  _resident((HY_W, D_MODEL), layer), _resident((SSM_W, D_MODEL), layer),
                  _resident((ATT_W, D_MODEL), layer), _resident((D_MODEL, D_MODEL), layer),
                  _resident((D_MODEL, FFN_HIDDEN), layer), _resident((D_MODEL, FFN_HIDDEN), layer),
                  _resident((FFN_HIDDEN, D_MODEL), layer)],
        out_specs=tok(D_MODEL),
        out_shape=jax.ShapeDtypeStruct((n_tok, D_MODEL), F32),
        compiler_params=_params(1),
        name="merge_ffn",
    )(x, hy, sy, sz, at, g, mod, snorm, npost, nfpre, nfpost, wh, ws, wa, wo, wg, wu, wd)


def _dft_mats(sl):
    radix = 32
    assert sl % radix == 0
    s = jnp.arange(sl, dtype=jnp.int32)

    def angles(f):
        return ((f[:, None] * s[None, :]) % (2 * sl)).astype(F32) * (math.pi / sl)

    a_hi = angles(jnp.arange(sl // radix, dtype=jnp.int32) * radix)[:, None, :]
    a_lo = angles(jnp.arange(radix, dtype=jnp.int32))[None, :, :]
    cos = (jnp.cos(a_hi) * jnp.cos(a_lo) - jnp.sin(a_hi) * jnp.sin(a_lo)).reshape(sl, sl)
    sin = (jnp.sin(a_hi) * jnp.cos(a_lo) + jnp.cos(a_hi) * jnp.sin(a_lo)).reshape(sl, sl)
    sign = jnp.where(s % 2 == 1, -1.0, 1.0).astype(F32)
    sinp = jnp.where(s[:, None] == 0, sign[None, :], sin).astype(BF16)
    return cos.astype(BF16), sinp, sinp.T


def _filter_features(sl):
    t = jnp.arange(sl, dtype=F32) / sl
    ang = 2.0 * jnp.pi * t[:, None] * jnp.arange(1, HY_BANDS + 1, dtype=F32)[None]
    feat = jnp.concatenate([t[:, None], jnp.cos(ang), jnp.sin(ang)], axis=-1)
    return jnp.pad(feat, ((0, 0), (0, LANES - HY_FEAT)))


def _rope_tables(sl):
    rows = sl // GRID_W
    row = jnp.repeat(jnp.arange(rows), GRID_W).astype(F32)
    col = jnp.tile(jnp.arange(GRID_W), rows).astype(F32)
    nq = HEAD_DIM // 4
    inv = ROPE_BASE ** (-jnp.arange(nq, dtype=F32) / nq)
    ar = row[:, None] * inv
    ac = col[:, None] * inv
    cos = jnp.concatenate([jnp.cos(ar), jnp.cos(ar), jnp.cos(ac), jnp.cos(ac)], axis=-1)
    sin = jnp.concatenate([-jnp.sin(ar), jnp.sin(ar), -jnp.sin(ac), jnp.sin(ac)], axis=-1)
    return jnp.tile(cos, (1, N_HEADS)), jnp.tile(sin, (1, N_HEADS))


def _pad_lanes(v):
    return jnp.pad(v.reshape(1, -1), ((0, 0), (0, LANES - v.size)))


def kernel(x_prompt, x_sample, c, cache_k, cache_v, state_ssm, c_ctx, ada_w, ada_b,
           norm_mix_pre, norm_mix_post, norm_ffn_pre, norm_ffn_post, w_in,
           hy_conv_w, hy_conv_b, hy_w1, hy_b1, hy_freq1, hy_w2, hy_b2, hy_freq2, hy_w3,
           hy_decay, hy_bias, ssm_conv_w, ssm_conv_b, ssm_a_log, ssm_dt_bias, ssm_d, ssm_norm,
           attn_sink, hy_proj, ssm_proj, attn_proj, w_out, ffn_w_gate, ffn_w_up, ffn_w_down):
    nb_c, sl_c, _ = x_prompt.shape
    nb_l, sl_l, _ = x_sample.shape
    past = cache_k.shape[2]

    cond = jnp.concatenate([c_ctx[None], c, jnp.zeros((8 - 1 - nb_l, D_MODEL), F32)], axis=0)
    mod = _ada_call(cond, ada_w, ada_b).reshape(DEPTH, 8, 6, D_MODEL)

    streams = {}
    for name, x in (("ctx", x_prompt), ("lat", x_sample)):
        nb, sl, _ = x.shape
        streams[name] = dict(nb=nb, sl=sl, x=x.reshape(nb * sl, D_MODEL), mats=_dft_mats(sl),
                             feat=_filter_features(sl))
    rope_cos, rope_sin = _rope_tables(sl_l)
    ctx_per_step = max(1, min(nb_c, SEQ_ROWS_PER_STEP // sl_c))
    while nb_c % ctx_per_step:
        ctx_per_step -= 1
    zero_state = jnp.zeros((1, 2 * SSM_PAIRS, 2 * SSM_STATE, PAIR_W), F32)
    expand = _expand_table()
    row = lambda v: v.reshape(1, -1)

    w_in_t = jnp.swapaxes(w_in, 1, 2).astype(BF16)
    wh, ws, wa, wo = (w.astype(BF16) for w in (hy_proj, ssm_proj, attn_proj, w_out))
    wg, wu, wd = (w.astype(BF16) for w in (ffn_w_gate, ffn_w_up, ffn_w_down))

    ks, vs, ss = [], [], []
    for l in range(DEPTH):
        w1p = jnp.pad(hy_w1[l], ((0, LANES - HY_FEAT), (0, 0)))
        dtb = _pad_lanes(ssm_dt_bias[l])
        alog = _pad_lanes(ssm_a_log[l])
        dsk = jnp.repeat(ssm_d[l, 0] + ssm_d[l, 1], SSM_HEAD_DIM).reshape(1, SSM_W)
        sink = {"ctx": _sink_rows(attn_sink[l], sl_c), "lat": _sink_rows(attn_sink[l], ATT_BLOCK)}
        for name in ("ctx", "lat"):
            st = streams[name]
            nb, sl, is_ctx = st["nb"], st["sl"], name == "ctx"
            m = mod[l, 0:1] if is_ctx else mod[l, 1:1 + nb]
            kre, kim = _filter_call(st["feat"], w1p, row(hy_b1[l]), row(hy_freq1[l]), hy_w2[l], row(hy_b2[l]),
                                    row(hy_freq2[l]), hy_w3[l], row(hy_decay[l]), st["mats"][0], st["mats"][1], sl)
            hy_in, s_z, s_xbc, s_dt, a_q, a_k, a_v, gate = _in_call(
                st["x"], m, row(norm_mix_pre[l]), w_in_t, l, nb, sl, is_ctx)
            hy_y = _hyena_call(hy_in, hy_conv_w[l], row(hy_conv_b[l]), hy_bias[l], kre, kim, st["mats"], nb, sl,
                               ctx_per_step if is_ctx else 1)
            h0 = zero_state if is_ctx else _pack_states(state_ssm[:, l])
            ssm_y, h_fin = _ssd_call(s_xbc, s_dt, h0, ssm_conv_w[l], row(ssm_conv_b[l]), dtb, alog, dsk, expand,
                                     nb, sl, is_ctx)
            if is_ctx:
                att = _ctx_attn_call(a_q, a_k, a_v, sink[name], nb, sl, ctx_per_step)
                ks.append(a_k.reshape(nb, sl, KV_HEADS, HEAD_DIM))
                vs.append(a_v.reshape(nb, sl, KV_HEADS, HEAD_DIM))
                ss.append(h_fin.reshape(nb, 2, SSM_HEADS, SSM_HEAD_DIM, SSM_STATE))
            else:
                att = _lat_attn_call(a_q, a_k, a_v, cache_k[:, l].reshape(nb, past, KV_W),
                                     cache_v[:, l].reshape(nb, past, KV_W), rope_cos, rope_sin, sink[name], nb, sl)
            st["x"] = _post_call(st["x"], hy_y, ssm_y, s_z, att, gate, m, row(ssm_norm[l]), row(norm_mix_post[l]),
                                 row(norm_ffn_pre[l]), row(norm_ffn_post[l]), wh, ws, wa, wo, wg, wu, wd, l, nb, sl,
                                 is_ctx)

    y_prompt = streams["ctx"]["x"].reshape(nb_c, sl_c, D_MODEL)
    y_sample = streams["lat"]["x"].reshape(nb_l, sl_l, D_MODEL)
    return (y_prompt, y_sample, jnp.stack(ks, axis=1), jnp.stack(vs, axis=1), jnp.stack(ss, axis=1))
```

```python
import functools
import math

import jax
import jax.numpy as jnp
from jax import lax
from jax.experimental import pallas as pl
from jax.experimental.pallas import tpu as pltpu

F32 = jnp.float32
BF16 = jnp.bfloat16

D_MODEL = 1024
DEPTH = 2
GRID_W = 64
HY_W = 512
HY_BANDS = 16
HY_FEAT = 1 + 2 * HY_BANDS
HY_HIDDEN = 64
SSM_W = 512
SSM_HEAD_DIM = 64
SSM_HEADS = 8
SSM_GROUPS = 2
SSM_STATE = 64
SSM_CHUNK = 128
XBC_W = SSM_W + 2 * SSM_GROUPS * SSM_STATE
N_HEADS = 8
KV_HEADS = 2
HEAD_DIM = 64
Q_PER_KV = N_HEADS // KV_HEADS
ATT_W = N_HEADS * HEAD_DIM
KV_W = KV_HEADS * HEAD_DIM
WINDOW = 128
ATT_BLOCK = 128
ROPE_BASE = 10000.0
FFN_HIDDEN = ((8 * D_MODEL + 3 * 256 - 1) // (3 * 256)) * 256
EPS = 1e-6

LANES = 128
ROW_TILE = 512
DFT_TILE = 512
HY_GROUPS = 2
SEQ_ROWS_PER_STEP = 1024
LAT_BLOCKS_PER_ROUND = 4
MXU_TILE = 256
FFN_SPLIT = (FFN_HIDDEN // MXU_TILE + 1) // 2 * MXU_TILE
FFN_CHUNKS = ((0, FFN_SPLIT), (FFN_SPLIT, FFN_HIDDEN))
IN_SPLIT = (3 * HY_W, SSM_W, XBC_W, 2 * SSM_HEADS, ATT_W, KV_W, KV_W, 3 * D_MODEL)
IN_WIDTHS = (3 * HY_W, SSM_W, XBC_W, LANES, ATT_W, KV_W, KV_W, 3 * D_MODEL)
VMEM_LIMIT = 56 * 1024 * 1024


def _params(n_grid):
    return pltpu.CompilerParams(dimension_semantics=("arbitrary",) * n_grid, vmem_limit_bytes=VMEM_LIMIT)


def _resident(shape, layer=None):
    nd = len(shape)
    if layer is None:
        return pl.BlockSpec(shape, lambda *_: (0,) * nd, pipeline_mode=pl.Buffered(1))
    return pl.BlockSpec((None, *shape), lambda *_: (layer,) + (0,) * nd, pipeline_mode=pl.Buffered(1))


def _bdot(a, b):
    return jnp.dot(a.astype(BF16), b.astype(BF16), preferred_element_type=F32)


def _dot_nt(a, b):
    return lax.dot_general(a.astype(BF16), b.astype(BF16), (((1,), (1,)), ((), ())), preferred_element_type=F32)


def _dot_tn(a, b):
    return lax.dot_general(a.astype(BF16), b.astype(BF16), (((0,), (0,)), ((), ())), preferred_element_type=F32)


def _split2(x):
    hi = x.astype(BF16)
    lo = (x - hi.astype(F32)).astype(BF16)
    return hi, lo


def _split3(x):
    h1 = x.astype(BF16)
    r1 = x - h1.astype(F32)
    h2 = r1.astype(BF16)
    h3 = (r1 - h2.astype(F32)).astype(BF16)
    return h1, h2, h3


def _dot3(a, b):
    ah, al = _split2(a)
    bh, bl = _split2(b)
    return (jnp.dot(ah, bh, preferred_element_type=F32) + jnp.dot(ah, bl, preferred_element_type=F32)
            + jnp.dot(al, bh, preferred_element_type=F32))


def _rms(x, g):
    return x * lax.rsqrt(jnp.mean(x * x, axis=-1, keepdims=True) + EPS) * g


def _silu(x):
    return x * jax.nn.sigmoid(x)


def _dwconv(x, w, b, period):
    n = x.shape[0]
    row = lax.broadcasted_iota(jnp.int32, (n, 1), 0) % period
    prev = jnp.where(row == 0, 0.0, pltpu.roll(x, 1, 0))
    nxt = jnp.where(row == period - 1, 0.0, pltpu.roll(x, n - 1, 0))
    return prev * w[0:1, :] + x * w[1:2, :] + nxt * w[2:3, :] + b


def _ada_kernel(cond_ref, w_ref, b_ref, o_ref):
    o_ref[0] = _bdot(_silu(cond_ref[...]), w_ref[0]) + b_ref[0]


def _ada_call(cond, ada_w, ada_b):
    n = 6 * D_MODEL
    tn = n // 4
    rows = cond.shape[0]
    return pl.pallas_call(
        _ada_kernel,
        grid=(DEPTH, n // tn),
        in_specs=[
            pl.BlockSpec((rows, D_MODEL), lambda l, j: (0, 0)),
            pl.BlockSpec((1, D_MODEL, tn), lambda l, j: (l, 0, j)),
            pl.BlockSpec((1, 1, tn), lambda l, j: (l, 0, j)),
        ],
        out_specs=pl.BlockSpec((1, rows, tn), lambda l, j: (l, 0, j)),
        out_shape=jax.ShapeDtypeStruct((DEPTH, rows, n), F32),
        compiler_params=_params(2),
        name="adaln",
    )(cond, ada_w, ada_b.reshape(DEPTH, 1, n))


def _in_kernel(x_ref, mod_ref, g_ref, wt_ref, *o_refs):
    h = _rms(x_ref[...], g_ref[...]) * (1.0 + mod_ref[0, 1:2, :]) + mod_ref[0, 0:1, :]
    hb = h.astype(BF16)
    off = 0
    for o_ref, wd, padded in zip(o_refs, IN_SPLIT, IN_WIDTHS):
        out = lax.dot_general(hb, wt_ref[off:off + padded, :], (((1,), (1,)), ((), ())), preferred_element_type=F32)
        if padded != wd:
            out = jnp.where(lax.broadcasted_iota(jnp.int32, (1, padded), 1) < wd, out, 0.0)
        o_ref[...] = out
        off += wd


def _mod_spec(shared, sl):
    if shared:
        return pl.BlockSpec((1, 6, D_MODEL), lambda i: (0, 0, 0))
    assert sl % ROW_TILE == 0
    return pl.BlockSpec((1, 6, D_MODEL), lambda i: (i // (sl // ROW_TILE), 0, 0))


def _in_call(x, mod, g, wt, layer, nb, sl, shared_mod):
    n_tok = nb * sl
    assert n_tok % ROW_TILE == 0
    row = lambda i: (i, 0)
    return pl.pallas_call(
        _in_kernel,
        grid=(n_tok // ROW_TILE,),
        in_specs=[
            pl.BlockSpec((ROW_TILE, D_MODEL), row),
            _mod_spec(shared_mod, sl),
            _resident((1, D_MODEL)),
            _resident((sum(IN_SPLIT), D_MODEL), layer),
        ],
        out_specs=[pl.BlockSpec((ROW_TILE, wd), row) for wd in IN_WIDTHS],
        out_shape=[jax.ShapeDtypeStruct((n_tok, wd), F32) for wd in IN_WIDTHS],
        compiler_params=_params(1),
        name="in_proj",
    )(x, mod, g, wt)


def _filter_kernel(feat_ref, w1, b1, f1, w2, b2, f2, w3, dec, cos_ref, sin_ref, kre_ref, kim_ref, *, sl_len):
    feat = feat_ref[...]
    h = jnp.sin(f1[...] * (_dot3(feat, w1[...]) + b1[...]))
    h = jnp.sin(f2[...] * (_dot3(h, w2[...]) + b2[...]))
    t = feat[:, 0:1]
    row = lax.broadcasted_iota(jnp.int32, (sl_len, 1), 0)
    scale = jnp.where(row == 0, 0.5 / sl_len, 1.0 / sl_len)
    gw = HY_W // HY_GROUPS

    def taps(o, c):
        fwd = slice(2 * o * HY_W + c * gw, 2 * o * HY_W + (c + 1) * gw)
        bwd = slice((2 * o + 1) * HY_W + c * gw, (2 * o + 1) * HY_W + (c + 1) * gw)
        hf = _dot3(h, w3[:, fwd]) * jnp.exp(-t * dec[:, fwd])
        hb = jnp.where(row == 0, 0.0, _dot3(h, w3[:, bwd]) * jnp.exp(-t * dec[:, bwd]))
        return hf + hb, hb - hf

    def spectrum(o, c, ksum, kdiff):
        kre = _bdot(cos_ref[...], ksum)
        kim = _bdot(sin_ref[...], kdiff)
        nyq = jnp.sum(jnp.where(row % 2 == 1, -ksum, ksum), axis=0, keepdims=True)
        kre_ref[o, :, c * gw:(c + 1) * gw] = kre * scale
        kim_ref[o, :, c * gw:(c + 1) * gw] = jnp.where(row == 0, nyq, kim) * scale

    pending = None
    for o in range(2):
        for c in range(HY_GROUPS):
            ready = (o, c, *taps(o, c))
            if pending is not None:
                spectrum(*pending)
            pending = ready
    spectrum(*pending)


def _filter_call(feat, w1, b1, f1, w2, b2, f2, w3, dec, cos, sinp, sl):
    hid = HY_HIDDEN
    return pl.pallas_call(
        functools.partial(_filter_kernel, sl_len=sl),
        grid=(1,),
        in_specs=[
            _resident((sl, LANES)), _resident((LANES, hid)), _resident((1, hid)), _resident((1, hid)),
            _resident((hid, hid)), _resident((1, hid)), _resident((1, hid)),
            _resident((hid, 4 * HY_W)), _resident((1, 4 * HY_W)),
            _resident((sl, sl)), _resident((sl, sl)),
        ],
        out_specs=[pl.BlockSpec((2, sl, HY_W), lambda i: (0, 0, 0))] * 2,
        out_shape=[jax.ShapeDtypeStruct((2, sl, HY_W), F32)] * 2,
        compiler_params=_params(1),
        name="hyena_filter",
    )(feat, w1, b1, f1, w2, b2, f2, w3, dec, cos, sinp)


def _hyena_kernel(x_ref, cw, cb, bias, kre, kim, cos, sin, sin_t, o_ref, hc, u, pre_s, pim_s, *, sl_len, n_seq):
    tile = min(DFT_TILE, sl_len)
    hc[...] = _dwconv(x_ref[...], cw[...], cb[...], sl_len)
    u[...] = hc[:, 0:HY_W].astype(BF16)

    gw = HY_W // HY_GROUPS
    chains = [(s, c) for s in range(n_seq) for c in range(HY_GROUPS)]
    held = {}

    def phase(chain, k):
        s, c = chains[chain]
        order, kind = divmod(k, 4)
        cols = slice(c * gw, (c + 1) * gw)
        x1_cols = slice(HY_W + c * gw, HY_W + (c + 1) * gw)
        x2_cols = slice(2 * HY_W + c * gw, 2 * HY_W + (c + 1) * gw)
        seq_rows = slice(s * sl_len, (s + 1) * sl_len)

        def spectrum(t, trows, rows):
            held[chain, t] = (jnp.dot(cos[trows, :], u[seq_rows, cols], preferred_element_type=F32),
                              jnp.dot(sin[trows, :], u[seq_rows, cols], preferred_element_type=F32))

        def multiply(t, trows, rows):
            a, b = held.pop((chain, t))
            kr = kre[order, trows, cols]
            ki = kim[order, trows, cols]
            if t == 0:
                first = lax.broadcasted_iota(jnp.int32, (tile, 1), 0) == 0
                pre = jnp.where(first, a * kr, a * kr + b * ki)
                pim = jnp.where(first, b * ki, b * kr - a * ki)
            else:
                pre = a * kr + b * ki
                pim = b * kr - a * ki
            pre_s[rows, cols] = pre.astype(BF16)
            pim_s[rows, cols] = pim.astype(BF16)

        def back(t, trows, rows):
            held[chain, t] = (jnp.dot(cos[trows, :], pre_s[seq_rows, cols], preferred_element_type=F32)
                              + jnp.dot(sin_t[trows, :], pim_s[seq_rows, cols], preferred_element_type=F32))

        def gate(t, trows, rows):
            y = held.pop((chain, t)) + hc[rows, cols] * bias[order:order + 1, cols]
            if order == 0:
                z = hc[rows, x1_cols] * y
                hc[rows, cols] = z
                u[rows, cols] = z.astype(BF16)
            else:
                o_ref[rows, cols] = (hc[rows, x2_cols] * y).astype(BF16)

        body = (spectrum, multiply, back, gate)[kind]
        return [functools.partial(body, t, slice(t, t + tile), slice(s * sl_len + t, s * sl_len + t + tile))
                for t in range(0, sl_len, tile)]

    n_phase = 8
    for slot in range(n_phase + 1):
        ahead = [op for ch in range(0, len(chains), 2) if slot < n_phase for op in phase(ch, slot)]
        behind = [op for ch in range(1, len(chains), 2) if slot >= 1 for op in phase(ch, slot - 1)]
        for i in range(max(len(ahead), len(behind))):
            for ops in (ahead, behind):
                if i < len(ops):
                    ops[i]()


def _hyena_call(hy_in, cw, cb, bias, kre, kim, mats, nb, sl, n_seq):
    assert nb % n_seq == 0
    seq = lambda b: (b, 0)
    mat = _resident((sl, sl))
    rows = n_seq * sl
    return pl.pallas_call(
        functools.partial(_hyena_kernel, sl_len=sl, n_seq=n_seq),
        grid=(nb // n_seq,),
        in_specs=[
            pl.BlockSpec((rows, 3 * HY_W), seq),
            _resident((3, 3 * HY_W)), _resident((1, 3 * HY_W)), _resident((2, HY_W)),
            _resident((2, sl, HY_W)), _resident((2, sl, HY_W)),
            mat, mat, mat,
        ],
        out_specs=pl.BlockSpec((rows, HY_W), seq),
        out_shape=jax.ShapeDtypeStruct((nb * sl, HY_W), BF16),
        scratch_shapes=[pltpu.VMEM((rows, 3 * HY_W), F32)] + [pltpu.VMEM((rows, HY_W), BF16)] * 3,
        compiler_params=_params(1),
        name="hyena_conv",
    )(hy_in, cw, cb, bias, kre, kim, *mats)


SSM_PAIRS = SSM_HEADS // 2
PAIR_W = 2 * SSM_HEAD_DIM


def _ssd_kernel(xbc_ref, dt_ref, h0_ref, cw, cb, dtb, alog, dsk, expand, o_ref, hfin_ref, xbc, dt, dta, hst, y_bwd, *,
                sl_len):
    nc = sl_len // SSM_CHUNK
    q = SSM_CHUNK
    xbc[...] = _silu(_dwconv(xbc_ref[...], cw[...], cb[...], sl_len))
    dtv = jax.nn.softplus(dt_ref[...] + dtb[...])
    dt[...] = dtv
    dta[...] = dtv * (-jnp.exp(alog[...]))
    hst[...] = h0_ref[0]
    ii = lax.broadcasted_iota(jnp.int32, (q, q), 0)
    jj = lax.broadcasted_iota(jnp.int32, (q, q), 1)
    low_lanes = lax.broadcasted_iota(jnp.int32, (1, PAIR_W), 1) < SSM_HEAD_DIM
    low_rows = lax.broadcasted_iota(jnp.int32, (2 * SSM_STATE, 1), 0) < SSM_STATE

    def prepare(c, d):
        rows = pl.ds(pl.multiple_of(c * q, q), q)
        xs = xbc[rows, 0:SSM_W]
        bm = xbc[rows, SSM_W:SSM_W + SSM_GROUPS * SSM_STATE]
        cm = xbc[rows, SSM_W + SSM_GROUPS * SSM_STATE:XBC_W]
        tri = (jj <= ii) if d == 0 else (jj >= ii)
        trib = tri.astype(BF16)
        a1, a2, a3 = _split3(dta[rows, :])
        acum = (jnp.dot(trib, a1, preferred_element_type=F32) + jnp.dot(trib, a2, preferred_element_type=F32)
                + jnp.dot(trib, a3, preferred_element_type=F32))
        dtc = dt[rows, :]
        last = acum[q - 1:q, :] if d == 0 else acum[0:1, :]
        w1, w2 = _split2(dtc * jnp.exp(last - acum))
        w_wide = (jnp.dot(w1, expand[d], preferred_element_type=F32)
                  + jnp.dot(w2, expand[d], preferred_element_type=F32))
        bm_t = bm.T.astype(BF16)
        cb_t = [_bdot(jnp.where(low_lanes, cm, 0.0), bm_t), _bdot(jnp.where(low_lanes, 0.0, cm), bm_t)]
        return dict(d=d, rows=rows, xs=xs, cm=cm, tri=tri, acum=acum, acum_t=acum.T, dt_t=dtc.T,
                    cdec=jnp.exp(last), xdd=xs * w_wide, bm_t=bm_t, cb_t=cb_t)

    def step(c, carry):
        chunks = [prepare(c, 0), prepare(nc - 1 - c, 1)]
        units = [(ch, p) for ch in chunks for p in range(SSM_PAIRS)]
        scores, grow = [], []
        for ch, p in units:
            g = p // (SSM_PAIRS // SSM_GROUPS)
            sc, gr = [], []
            for hh in range(2):
                col = ch["d"] * SSM_HEADS + 2 * p + hh
                a_col = jnp.broadcast_to(ch["acum"][:, col:col + 1], (q, q))
                lmat = jnp.exp(jnp.where(ch["tri"], a_col - ch["acum_t"][col:col + 1, :], -jnp.inf))
                sc.append((ch["cb_t"][g] * lmat * ch["dt_t"][col:col + 1, :]).astype(BF16))
                gr.append(jnp.exp(a_col))
            scores.append(jnp.concatenate(sc, axis=1))
            grow.append(jnp.where(low_lanes, gr[0], gr[1]))
        y_diag, y_off, states = [], [], []
        for (ch, p), sc in zip(units, scores):
            xp = ch["xs"][:, p * PAIR_W:(p + 1) * PAIR_W]
            x_blocks = jnp.concatenate([jnp.where(low_lanes, xp, 0.0), jnp.where(low_lanes, 0.0, xp)], axis=0)
            y_diag.append(jnp.dot(sc, x_blocks.astype(BF16), preferred_element_type=F32))
        for ch, p in units:
            y_off.append(_bdot(ch["cm"], hst[ch["d"] * SSM_PAIRS + p]))
            states.append(_bdot(ch["bm_t"], ch["xdd"][:, p * PAIR_W:(p + 1) * PAIR_W]))
        for i, (ch, p) in enumerate(units):
            slab = ch["d"] * SSM_PAIRS + p
            col = ch["d"] * SSM_HEADS + 2 * p
            cd = jnp.where(low_lanes, ch["cdec"][:, col:col + 1], ch["cdec"][:, col + 1:col + 2])
            own_rows = low_rows if p < SSM_PAIRS // SSM_GROUPS else jnp.logical_not(low_rows)
            hst[slab] = hst[slab] * cd + jnp.where(own_rows, states[i], 0.0)
        for k, ch in enumerate(chunks):
            y = jnp.concatenate([y_diag[k * SSM_PAIRS + p] + y_off[k * SSM_PAIRS + p] * grow[k * SSM_PAIRS + p]
                                 for p in range(SSM_PAIRS)], axis=1)
            if ch["d"] == 0:
                o_ref[ch["rows"], :] = y + dsk[...] * ch["xs"]
            else:
                y_bwd[ch["rows"], :] = y
        return carry

    lax.fori_loop(0, nc, step, 0, unroll=4 if nc % 4 == 0 else 2)
    o_ref[...] += y_bwd[...]
    for slab in range(2 * SSM_PAIRS):
        g = (slab % SSM_PAIRS) // (SSM_PAIRS // SSM_GROUPS)
        st_t = hst[slab].T
        for hh in range(2):
            hfin_ref[0, 2 * slab + hh] = st_t[hh * SSM_HEAD_DIM:(hh + 1) * SSM_HEAD_DIM,
                                              g * SSM_STATE:(g + 1) * SSM_STATE]


def _ssd_call(s_xbc, s_dt, h0, cw, cb, dtb, alog, dsk, expand, nb, sl, shared_h0):
    seq = lambda b: (b, 0)
    st_shape = (1, 2 * SSM_PAIRS, 2 * SSM_STATE, PAIR_W)
    fin_shape = (1, 2 * SSM_HEADS, SSM_HEAD_DIM, SSM_STATE)
    h0_map =(lambda b: (0, 0, 0, 0)) if shared_h0 else (lambda b: (b, 0, 0, 0))
    return pl.pallas_call(
        functools.partial(_ssd_kernel, sl_len=sl),
        grid=(nb,),
        in_specs=[
            pl.BlockSpec((sl, XBC_W), seq), pl.BlockSpec((sl, LANES), seq), pl.BlockSpec(st_shape, h0_map),
            _resident((3, XBC_W)), _resident((1, XBC_W)), _resident((1, LANES)), _resident((1, LANES)),
            _resident((1, SSM_W)), _resident((2, LANES, SSM_W)),
        ],
        out_specs=[pl.BlockSpec((sl, SSM_W), seq), pl.BlockSpec(fin_shape, lambda b: (b, 0, 0, 0))],
        out_shape=[jax.ShapeDtypeStruct((nb * sl, SSM_W), F32), jax.ShapeDtypeStruct((nb, *fin_shape[1:]), F32)],
        scratch_shapes=[pltpu.VMEM((sl, XBC_W), F32), pltpu.VMEM((sl, LANES), F32), pltpu.VMEM((sl, LANES), F32),
                        pltpu.VMEM(st_shape[1:], F32), pltpu.VMEM((sl, SSM_W), F32)],
        compiler_params=_params(1),
        name="ssd_scan",
    )(s_xbc, s_dt, h0, cw, cb, dtb, alog, dsk, expand)


def _pack_states(h):
    nb = h.shape[0]
    t = h.reshape(nb, 2, SSM_PAIRS, 2, SSM_HEAD_DIM, SSM_STATE).transpose(0, 1, 2, 5, 3, 4)
    t = t.reshape(nb, 2, SSM_PAIRS, SSM_STATE, PAIR_W)
    z = jnp.zeros_like(t)
    first = (jnp.arange(SSM_PAIRS) < SSM_PAIRS // SSM_GROUPS)[None, None, :, None, None]
    slabs = jnp.where(first, jnp.concatenate([t, z], axis=3), jnp.concatenate([z, t], axis=3))
    return slabs.reshape(nb, 2 * SSM_PAIRS, 2 * SSM_STATE, PAIR_W)


def _expand_table():
    r = jnp.arange(LANES)[:, None]
    head = jnp.arange(SSM_W)[None, :] // SSM_HEAD_DIM
    return jnp.stack([r == head, r == SSM_HEADS + head]).astype(BF16)


def _attend_groups(units):
    logits = []
    for keys, q_t, _, _, g, _ in units:
        z = jnp.zeros_like(q_t)
        rhs = jnp.concatenate([q_t, z] if g == 0 else [z, q_t], axis=0)
        logits.append(jnp.dot(keys, rhs, preferred_element_type=F32))
    logits = [s if u[5] is None else jnp.where(u[5], s, -jnp.inf) for s, u in zip(logits, units)]
    sinks = [u[3] for u in units]
    tops = [jnp.maximum(jnp.max(s, axis=0, keepdims=True), sk) for s, sk in zip(logits, sinks)]
    exps = [jnp.exp(s - m) for s, m in zip(logits, tops)]
    denoms = [jnp.sum(e, axis=0, keepdims=True) + jnp.exp(sk - m) for e, m, sk in zip(exps, tops, sinks)]
    return [jnp.dot(u[2], e.astype(BF16), preferred_element_type=F32) / d for u, e, d in zip(units, exps, denoms)]


def _ctx_attn_kernel(q_ref, k_ref, v_ref, sink_ref, o_ref, *, sl_len, n_seq):
    scale = HEAD_DIM ** -0.5
    units = []
    for s in range(n_seq):
        rows = slice(s * sl_len, (s + 1) * sl_len)
        q_t = (q_ref[rows, :] * scale).T.astype(BF16)
        keys = k_ref[rows, :].astype(BF16)
        v_t = v_ref[rows, :].T.astype(BF16)
        for g in range(KV_HEADS):
            q_g = jnp.concatenate([q_t[(g * Q_PER_KV + hq) * HEAD_DIM:(g * Q_PER_KV + hq + 1) * HEAD_DIM, :]
                                   for hq in range(Q_PER_KV)], axis=1)
            units.append((keys, q_g, v_t[g * HEAD_DIM:(g + 1) * HEAD_DIM, :], sink_ref[g], g, None))
    o_ts = _attend_groups(units)
    for s in range(n_seq):
        outs = [o_t[:, hq * sl_len:(hq + 1) * sl_len] for o_t in o_ts[s * KV_HEADS:(s + 1) * KV_HEADS]
                for hq in range(Q_PER_KV)]
        o_ref[s * sl_len:(s + 1) * sl_len, :] = jnp.concatenate(outs, axis=0).T.astype(BF16)


def _ctx_attn_call(q, k, v, sink_rows, nb, sl, n_seq):
    assert nb % n_seq == 0
    seq = lambda b: (b, 0)
    rows = n_seq * sl
    return pl.pallas_call(
        functools.partial(_ctx_attn_kernel, sl_len=sl, n_seq=n_seq),
        grid=(nb // n_seq,),
        in_specs=[pl.BlockSpec((rows, ATT_W), seq), pl.BlockSpec((rows, KV_W), seq), pl.BlockSpec((rows, KV_W), seq),
                  _resident((KV_HEADS, 1, Q_PER_KV * sl))],
        out_specs=pl.BlockSpec((rows, ATT_W), seq),
        out_shape=jax.ShapeDtypeStruct((nb * sl, ATT_W), BF16),
        compiler_params=_params(1),
        name="ctx_attention",
    )(q, k, v, sink_rows)


def _rope(x, cos, sin_signed):
    w = x.shape[1]
    lane = lax.broadcasted_iota(jnp.int32, (1, w), 1)
    partner = jnp.where(lane % 32 < 16, pltpu.roll(x, w - 16, 1), pltpu.roll(x, 16, 1))
    return x * cos + partner * sin_signed


def _lat_attn_kernel(q_ref, k_ref, v_ref, ck_ref, cv_ref, cos_ref, sin_ref, sink_ref, o_ref, q_t, kcat, v_t, o_t, *,
                     sl_len, past):
    blk = ATT_BLOCK
    nblk = sl_len // blk
    scale = HEAD_DIM ** -0.5
    q_t[...] = (_rope(q_ref[...], cos_ref[...], sin_ref[...]) * scale).T.astype(BF16)
    kcat[0:past, :] = ck_ref[0].astype(BF16)
    kcat[past:, :] = _rope(k_ref[...], cos_ref[:, 0:KV_W], sin_ref[:, 0:KV_W]).astype(BF16)
    v_t[:, 0:past] = cv_ref[0].T.astype(BF16)
    v_t[:, past:] = v_ref[...].T.astype(BF16)
    for i0 in range(0, nblk, LAT_BLOCKS_PER_ROUND):
        units, places = [], []
        for i in range(i0, min(i0 + LAT_BLOCKS_PER_ROUND, nblk)):
            cols = slice(i * blk, (i + 1) * blk)
            lo, hi = max(i - 1, 0), min(i + 2, nblk)
            band = slice(past + lo * blk, past + hi * blk)
            n_keys = past + (hi - lo) * blk
            kk = lax.broadcasted_iota(jnp.int32, (n_keys, 1), 0)
            qi = lax.broadcasted_iota(jnp.int32, (1, Q_PER_KV * blk), 1) % blk
            rel = kk - past + (lo - i) * blk
            mask = (kk < past) | (jnp.abs(qi - rel) <= WINDOW)
            keys = jnp.concatenate([kcat[0:past, :], kcat[band, :]], axis=0)
            for g in range(KV_HEADS):
                drows = slice(g * HEAD_DIM, (g + 1) * HEAD_DIM)
                q_g = jnp.concatenate([q_t[(g * Q_PER_KV + hq) * HEAD_DIM:(g * Q_PER_KV + hq + 1) * HEAD_DIM, cols]
                                       for hq in range(Q_PER_KV)], axis=1)
                vals = jnp.concatenate([v_t[drows, 0:past], v_t[drows, band]], axis=1)
                units.append((keys, q_g, vals, sink_ref[g], g, mask))
                places.append((g, cols))
        for (g, cols), out in zip(places, _attend_groups(units)):
            for hq in range(Q_PER_KV):
                h = g * Q_PER_KV + hq
                o_t[h * HEAD_DIM:(h + 1) * HEAD_DIM, cols] = out[:, hq * blk:(hq + 1) * blk]
    o_ref[...] = o_t[...].T.astype(BF16)


def _lat_attn_call(q, k, v, ck, cv, cos, sin, sink_rows, nb, sl):
    seq = lambda b: (b, 0)
    past = ck.shape[1]
    cache = pl.BlockSpec((1, past, KV_W), lambda b: (b, 0, 0))
    return pl.pallas_call(
        functools.partial(_lat_attn_kernel, sl_len=sl, past=past),
        grid=(nb,),
        in_specs=[pl.BlockSpec((sl, ATT_W), seq), pl.BlockSpec((sl, KV_W), seq), pl.BlockSpec((sl, KV_W), seq),
                  cache, cache, _resident((sl, ATT_W)), _resident((sl, ATT_W)),
                  _resident((KV_HEADS, 1, Q_PER_KV * ATT_BLOCK))],
        out_specs=pl.BlockSpec((sl, ATT_W), seq),
        out_shape=jax.ShapeDtypeStruct((nb * sl, ATT_W), BF16),
        scratch_shapes=[pltpu.VMEM((ATT_W, sl), BF16), pltpu.VMEM((past + sl, KV_W), BF16),
                        pltpu.VMEM((KV_W, past + sl), BF16), pltpu.VMEM((ATT_W, sl), F32)],
        compiler_params=_params(1),
        name="latent_attention",
    )(q, k, v, ck, cv, cos, sin, sink_rows)


def _sink_rows(sink, n_q):
    return jnp.repeat(sink.reshape(KV_HEADS, Q_PER_KV), n_q, axis=1).reshape(KV_HEADS, 1, Q_PER_KV * n_q)


def _post_kernel(x_ref, hy_ref, sy_ref, sz_ref, at_ref, g_ref, mod_ref, snorm, npost, nfpre, nfpost,
                 wh, ws, wa, wo, wg, wu, wd, o_ref):
    m = mod_ref[0]
    halves = [slice(r * (ROW_TILE // 2), (r + 1) * (ROW_TILE // 2)) for r in range(2)]
    ssm_y = [_rms(sy_ref[r, :] * _silu(sz_ref[r, :]), snorm[...]) for r in halves]
    p_hy = [_bdot(hy_ref[r, :], wh[...]) for r in halves]
    p_ss = [_bdot(y, ws[...]) for y in ssm_y]
    p_at = [_bdot(at_ref[r, :], wa[...]) for r in halves]
    merged = [jax.nn.sigmoid(g_ref[r, 0:D_MODEL]) * a + jax.nn.sigmoid(g_ref[r, D_MODEL:2 * D_MODEL]) * b
              + jax.nn.sigmoid(g_ref[r, 2 * D_MODEL:3 * D_MODEL]) * c for r, a, b, c in zip(halves, p_hy, p_ss, p_at)]
    mix = [_bdot(v, wo[...]) for v in merged]
    x1 = [x_ref[r, :] + m[2:3, :] * _rms(v, npost[...]) for r, v in zip(halves, mix)]
    hb = [(_rms(v, nfpre[...]) * (1.0 + m[4:5, :]) + m[3:4, :]).astype(BF16) for v in x1]
    f = [None, None]
    for lo, hi in FFN_CHUNKS:
        gate = [jnp.dot(v, wg[:, lo:hi], preferred_element_type=F32) for v in hb]
        up = [jnp.dot(v, wu[:, lo:hi], preferred_element_type=F32) for v in hb]
        part = [_bdot(_silu(a) * b, wd[lo:hi, :]) for a, b in zip(gate, up)]
        f = [p if acc is None else acc + p for acc, p in zip(f, part)]
    for r, a, b in zip(halves, x1, f):
        o_ref[r, :] = a + m[5:6, :] * _rms(b, nfpost[...])


def _post_call(x, hy, sy, sz, at, g, mod, snorm, npost, nfpre, nfpost, wh, ws, wa, wo, wg, wu, wd, layer, nb, sl,
               shared_mod):
    n_tok = nb * sl
    assert n_tok % ROW_TILE == 0
    tok = lambda wdt: pl.BlockSpec((ROW_TILE, wdt), lambda i: (i, 0))
    vec = lambda wdt: _resident((1, wdt))
    return pl.pallas_call(
        _post_kernel,
        grid=(n_tok // ROW_TILE,),
        in_specs=[tok(D_MODEL), tok(HY_W), tok(SSM_W), tok(SSM_W), tok(ATT_W), tok(3 * D_MODEL),
                  _mod_spec(shared_mod, sl),
                  vec(SSM_W), vec(D_MODEL), vec(D_MODEL), vec(D_MODEL),
                  _resident((HY_W, D_MODEL), layer), _resident((SSM_W, D_MODEL), layer),
                  _resident((ATT_W, D_MODEL), layer), _resident((D_MODEL, D_MODEL), layer),
                  _resident((D_MODEL, FFN_HIDDEN), layer), _resident((D_MODEL, FFN_HIDDEN), layer),
                  _resident((FFN_HIDDEN, D_MODEL), layer)],
        out_specs=tok(D_MODEL),
        out_shape=jax.ShapeDtypeStruct((n_tok, D_MODEL), F32),
        compiler_params=_params(1),
        name="merge_ffn",
    )(x, hy, sy, sz, at, g, mod, snorm, npost, nfpre, nfpost, wh, ws, wa, wo, wg, wu, wd)


def _dft_mats(sl):
    radix = 32
    assert sl % radix == 0
    s = jnp.arange(sl, dtype=jnp.int32)

    def angles(f):
        return ((f[:, None] * s[None, :]) % (2 * sl)).astype(F32) * (math.pi / sl)

    a_hi = angles(jnp.arange(sl // radix, dtype=jnp.int32) * radix)[:, None, :]
    a_lo = angles(jnp.arange(radix, dtype=jnp.int32))[None, :, :]
    cos = (jnp.cos(a_hi) * jnp.cos(a_lo) - jnp.sin(a_hi) * jnp.sin(a_lo)).reshape(sl, sl)
    sin = (jnp.sin(a_hi) * jnp.cos(a_lo) + jnp.cos(a_hi) * jnp.sin(a_lo)).reshape(sl, sl)
    sign = jnp.where(s % 2 == 1, -1.0, 1.0).astype(F32)
    sinp = jnp.where(s[:, None] == 0, sign[None, :], sin).astype(BF16)
    return cos.astype(BF16), sinp, sinp.T


def _filter_features(sl):
    t = jnp.arange(sl, dtype=F32) / sl
    ang = 2.0 * jnp.pi * t[:, None] * jnp.arange(1, HY_BANDS + 1, dtype=F32)[None]
    feat = jnp.concatenate([t[:, None], jnp.cos(ang), jnp.sin(ang)], axis=-1)
    return jnp.pad(feat, ((0, 0), (0, LANES - HY_FEAT)))


def _rope_tables(sl):
    rows = sl // GRID_W
    row = jnp.repeat(jnp.arange(rows), GRID_W).astype(F32)
    col = jnp.tile(jnp.arange(GRID_W), rows).astype(F32)
    nq = HEAD_DIM // 4
    inv = ROPE_BASE ** (-jnp.arange(nq, dtype=F32) / nq)
    ar = row[:, None] * inv
    ac = col[:, None] * inv
    cos = jnp.concatenate([jnp.cos(ar), jnp.cos(ar), jnp.cos(ac), jnp.cos(ac)], axis=-1)
    sin = jnp.concatenate([-jnp.sin(ar), jnp.sin(ar), -jnp.sin(ac), jnp.sin(ac)], axis=-1)
    return jnp.tile(cos, (1, N_HEADS)), jnp.tile(sin, (1, N_HEADS))


def _pad_lanes(v):
    return jnp.pad(v.reshape(1, -1), ((0, 0), (0, LANES - v.size)))


def kernel(x_prompt, x_sample, c, cache_k, cache_v, state_ssm, c_ctx, ada_w, ada_b,
           norm_mix_pre, norm_mix_post, norm_ffn_pre, norm_ffn_post, w_in,
           hy_conv_w, hy_conv_b, hy_w1, hy_b1, hy_freq1, hy_w2, hy_b2, hy_freq2, hy_w3,
           hy_decay, hy_bias, ssm_conv_w, ssm_conv_b, ssm_a_log, ssm_dt_bias, ssm_d, ssm_norm,
           attn_sink, hy_proj, ssm_proj, attn_proj, w_out, ffn_w_gate, ffn_w_up, ffn_w_down):
    nb_c, sl_c, _ = x_prompt.shape
    nb_l, sl_l, _ = x_sample.shape
    past = cache_k.shape[2]

    cond = jnp.concatenate([c_ctx[None], c, jnp.zeros((8 - 1 - nb_l, D_MODEL), F32)], axis=0)
    mod = _ada_call(cond, ada_w, ada_b).reshape(DEPTH, 8, 6, D_MODEL)

    streams = {}
    for name, x in (("ctx", x_prompt), ("lat", x_sample)):
        nb, sl, _ = x.shape
        streams[name] = dict(nb=nb, sl=sl, x=x.reshape(nb * sl, D_MODEL), mats=_dft_mats(sl),
                             feat=_filter_features(sl))
    rope_cos, rope_sin = _rope_tables(sl_l)
    ctx_per_step = max(1, min(nb_c, SEQ_ROWS_PER_STEP // sl_c))
    while nb_c % ctx_per_step:
        ctx_per_step -= 1
    zero_state = jnp.zeros((1, 2 * SSM_PAIRS, 2 * SSM_STATE, PAIR_W), F32)
    expand = _expand_table()
    row = lambda v: v.reshape(1, -1)

    w_in_t = jnp.swapaxes(w_in, 1, 2).astype(BF16)
    wh, ws, wa, wo = (w.astype(BF16) for w in (hy_proj, ssm_proj, attn_proj, w_out))
    wg, wu, wd = (w.astype(BF16) for w in (ffn_w_gate, ffn_w_up, ffn_w_down))

    ks, vs, ss = [], [], []
    for l in range(DEPTH):
        w1p = jnp.pad(hy_w1[l], ((0, LANES - HY_FEAT), (0, 0)))
        dtb = _pad_lanes(ssm_dt_bias[l])
        alog = _pad_lanes(ssm_a_log[l])
        dsk = jnp.repeat(ssm_d[l, 0] + ssm_d[l, 1], SSM_HEAD_DIM).reshape(1, SSM_W)
        sink = {"ctx": _sink_rows(attn_sink[l], sl_c), "lat": _sink_rows(attn_sink[l], ATT_BLOCK)}
        for name in ("ctx", "lat"):
            st = streams[name]
            nb, sl, is_ctx = st["nb"], st["sl"], name == "ctx"
            m = mod[l, 0:1] if is_ctx else mod[l, 1:1 + nb]
            kre, kim = _filter_call(st["feat"], w1p, row(hy_b1[l]), row(hy_freq1[l]), hy_w2[l], row(hy_b2[l]),
                                    row(hy_freq2[l]), hy_w3[l], row(hy_decay[l]), st["mats"][0], st["mats"][1], sl)
            hy_in, s_z, s_xbc, s_dt, a_q, a_k, a_v, gate = _in_call(
                st["x"], m, row(norm_mix_pre[l]), w_in_t, l, nb, sl, is_ctx)
            hy_y = _hyena_call(hy_in, hy_conv_w[l], row(hy_conv_b[l]), hy_bias[l], kre, kim, st["mats"], nb, sl,
                               ctx_per_step if is_ctx else 1)
            h0 = zero_state if is_ctx else _pack_states(state_ssm[:, l])
            ssm_y, h_fin = _ssd_call(s_xbc, s_dt, h0, ssm_conv_w[l], row(ssm_conv_b[l]), dtb, alog, dsk, expand,
                                     nb, sl, is_ctx)
            if is_ctx:
                att = _ctx_attn_call(a_q, a_k, a_v, sink[name], nb, sl, ctx_per_step)
                ks.append(a_k.reshape(nb, sl, KV_HEADS, HEAD_DIM))
                vs.append(a_v.reshape(nb, sl, KV_HEADS, HEAD_DIM))
                ss.append(h_fin.reshape(nb, 2, SSM_HEADS, SSM_HEAD_DIM, SSM_STATE))
            else:
                att = _lat_attn_call(a_q, a_k, a_v, cache_k[:, l].reshape(nb, past, KV_W),
                                     cache_v[:, l].reshape(nb, past, KV_W), rope_cos, rope_sin, sink[name], nb, sl)
            st["x"] = _post_call(st["x"], hy_y, ssm_y, s_z, att, gate, m, row(ssm_norm[l]), row(norm_mix_post[l]),
                                 row(norm_ffn_pre[l]), row(norm_ffn_post[l]), wh, ws, wa, wo, wg, wu, wd, l, nb, sl,
                                 is_ctx)

    y_prompt = streams["ctx"]["x"].reshape(nb_c, sl_c, D_MODEL)
    y_sample = streams["lat"]["x"].reshape(nb_l, sl_l, D_MODEL)
    return (y_prompt, y_sample, jnp.stack(ks, axis=1), jnp.stack(vs, axis=1), jnp.stack(ss, axis=1))
```

```python
import functools
import math

import jax
import jax.numpy as jnp
from jax import lax
from jax.experimental import pallas as pl
from jax.experimental.pallas import tpu as pltpu

F32 = jnp.float32
BF16 = jnp.bfloat16

D_MODEL = 1024
DEPTH = 2
GRID_W = 64
HY_W = 512
HY_BANDS = 16
HY_FEAT = 1 + 2 * HY_BANDS
HY_HIDDEN = 64
SSM_W = 512
SSM_HEAD_DIM = 64
SSM_HEADS = 8
SSM_GROUPS = 2
SSM_STATE = 64
SSM_CHUNK = 128
XBC_W = SSM_W + 2 * SSM_GROUPS * SSM_STATE
N_HEADS = 8
KV_HEADS = 2
HEAD_DIM = 64
Q_PER_KV = N_HEADS // KV_HEADS
ATT_W = N_HEADS * HEAD_DIM
KV_W = KV_HEADS * HEAD_DIM
WINDOW = 128
ATT_BLOCK = 128
ROPE_BASE = 10000.0
FFN_HIDDEN = ((8 * D_MODEL + 3 * 256 - 1) // (3 * 256)) * 256
EPS = 1e-6

LANES = 128
ROW_TILE = 512
DFT_TILE = 512
HY_GROUPS = 2
SEQ_ROWS_PER_STEP = 1024
LAT_BLOCKS_PER_ROUND = 4
MXU_TILE = 256
FFN_SPLIT = (FFN_HIDDEN // MXU_TILE + 1) // 2 * MXU_TILE
FFN_CHUNKS = ((0, FFN_SPLIT), (FFN_SPLIT, FFN_HIDDEN))
IN_SPLIT = (3 * HY_W, SSM_W, XBC_W, 2 * SSM_HEADS, ATT_W, KV_W, KV_W, 3 * D_MODEL)
IN_WIDTHS = (3 * HY_W, SSM_W, XBC_W, LANES, ATT_W, KV_W, KV_W, 3 * D_MODEL)
VMEM_LIMIT = 56 * 1024 * 1024


def _params(n_grid):
    return pltpu.CompilerParams(dimension_semantics=("arbitrary",) * n_grid, vmem_limit_bytes=VMEM_LIMIT)


def _resident(shape, layer=None):
    nd = len(shape)
    if layer is None:
        return pl.BlockSpec(shape, lambda *_: (0,) * nd, pipeline_mode=pl.Buffered(1))
    return pl.BlockSpec((None, *shape), lambda *_: (layer,) + (0,) * nd, pipeline_mode=pl.Buffered(1))


def _bdot(a, b):
    return jnp.dot(a.astype(BF16), b.astype(BF16), preferred_element_type=F32)


def _split2(x):
    hi = x.astype(BF16)
    lo = (x - hi.astype(F32)).astype(BF16)
    return hi, lo


def _split3(x):
    h1 = x.astype(BF16)
    r1 = x - h1.astype(F32)
    h2 = r1.astype(BF16)
    h3 = (r1 - h2.astype(F32)).astype(BF16)
    return h1, h2, h3


def _dot3(a, b):
    ah, al = _split2(a)
    bh, bl = _split2(b)
    return (jnp.dot(ah, bh, preferred_element_type=F32) + jnp.dot(ah, bl, preferred_element_type=F32)
            + jnp.dot(al, bh, preferred_element_type=F32))


def _rms(x, g):
    return x * lax.rsqrt(jnp.mean(x * x, axis=-1, keepdims=True) + EPS) * g


def _silu(x):
    return x * jax.nn.sigmoid(x)


def _dwconv(x, w, b, period):
    n = x.shape[0]
    row = lax.broadcasted_iota(jnp.int32, (n, 1), 0) % period
    prev = jnp.where(row == 0, 0.0, pltpu.roll(x, 1, 0))
    nxt = jnp.where(row == period - 1, 0.0, pltpu.roll(x, n - 1, 0))
    return prev * w[0:1, :] + x * w[1:2, :] + nxt * w[2:3, :] + b


def _ada_kernel(cond_ref, w_ref, b_ref, o_ref):
    o_ref[0] = _bdot(_silu(cond_ref[...]), w_ref[0]) + b_ref[0]


def _ada_call(cond, ada_w, ada_b):
    n = 6 * D_MODEL
    tn = n // 4
    rows = cond.shape[0]
    return pl.pallas_call(
        _ada_kernel,
        grid=(DEPTH, n // tn),
        in_specs=[
            pl.BlockSpec((rows, D_MODEL), lambda l, j: (0, 0)),
            pl.BlockSpec((1, D_MODEL, tn), lambda l, j: (l, 0, j)),
            pl.BlockSpec((1, 1, tn), lambda l, j: (l, 0, j)),
        ],
        out_specs=pl.BlockSpec((1, rows, tn), lambda l, j: (l, 0, j)),
        out_shape=jax.ShapeDtypeStruct((DEPTH, rows, n), F32),
        compiler_params=_params(2),
        name="adaln",
    )(cond, ada_w, ada_b.reshape(DEPTH, 1, n))


def _in_kernel(x_ref, mod_ref, g_ref, wt_ref, *o_refs):
    h = _rms(x_ref[...], g_ref[...]) * (1.0 + mod_ref[0, 1:2, :]) + mod_ref[0, 0:1, :]
    hb = h.astype(BF16)
    off = 0
    for o_ref, wd, padded in zip(o_refs, IN_SPLIT, IN_WIDTHS):
        out = lax.dot_general(hb, wt_ref[off:off + padded, :], (((1,), (1,)), ((), ())), preferred_element_type=F32)
        if padded != wd:
            out = jnp.where(lax.broadcasted_iota(jnp.int32, (1, padded), 1) < wd, out, 0.0)
        o_ref[...] = out
        off += wd


def _mod_spec(shared, sl):
    if shared:
        return pl.BlockSpec((1, 6, D_MODEL), lambda i: (0, 0, 0))
    assert sl % ROW_TILE == 0
    return pl.BlockSpec((1, 6, D_MODEL), lambda i: (i // (sl // ROW_TILE), 0, 0))


def _in_call(x, mod, g, wt, layer, nb, sl, shared_mod):
    n_tok = nb * sl
    assert n_tok % ROW_TILE == 0
    row = lambda i: (i, 0)
    return pl.pallas_call(
        _in_kernel,
        grid=(n_tok // ROW_TILE,),
        in_specs=[
            pl.BlockSpec((ROW_TILE, D_MODEL), row),
            _mod_spec(shared_mod, sl),
            _resident((1, D_MODEL)),
            _resident((sum(IN_SPLIT), D_MODEL), layer),
        ],
        out_specs=[pl.BlockSpec((ROW_TILE, wd), row) for wd in IN_WIDTHS],
        out_shape=[jax.ShapeDtypeStruct((n_tok, wd), F32) for wd in IN_WIDTHS],
        compiler_params=_params(1),
        name="in_proj",
    )(x, mod, g, wt)


def _filter_kernel(feat_ref, w1, b1, f1, w2, b2, f2, w3, dec, cos_ref, sin_ref, kre_ref, kim_ref, *, sl_len):
    feat = feat_ref[...]
    h = jnp.sin(f1[...] * (_dot3(feat, w1[...]) + b1[...]))
    h = jnp.sin(f2[...] * (_dot3(h, w2[...]) + b2[...]))
    t = feat[:, 0:1]
    row = lax.broadcasted_iota(jnp.int32, (sl_len, 1), 0)
    scale = jnp.where(row == 0, 0.5 / sl_len, 1.0 / sl_len)
    gw = HY_W // HY_GROUPS

    def taps(o, c):
        fwd = slice(2 * o * HY_W + c * gw, 2 * o * HY_W + (c + 1) * gw)
        bwd = slice((2 * o + 1) * HY_W + c * gw, (2 * o + 1) * HY_W + (c + 1) * gw)
        hf = _dot3(h, w3[:, fwd]) * jnp.exp(-t * dec[:, fwd])
        hb = jnp.where(row == 0, 0.0, _dot3(h, w3[:, bwd]) * jnp.exp(-t * dec[:, bwd]))
        return hf + hb, hb - hf

    def spectrum(o, c, ksum, kdiff):
        kre = _bdot(cos_ref[...], ksum)
        kim = _bdot(sin_ref[...], kdiff)
        nyq = jnp.sum(jnp.where(row % 2 == 1, -ksum, ksum), axis=0, keepdims=True)
        kre_ref[o, :, c * gw:(c + 1) * gw] = kre * scale
        kim_ref[o, :, c * gw:(c + 1) * gw] = jnp.where(row == 0, nyq, kim) * scale

    pending = None
    for o in range(2):
        for c in range(HY_GROUPS):
            ready = (o, c, *taps(o, c))
            if pending is not None:
                spectrum(*pending)
            pending = ready
    spectrum(*pending)


def _filter_call(feat, w1, b1, f1, w2, b2, f2, w3, dec, cos, sinp, sl):
    hid = HY_HIDDEN
    return pl.pallas_call(
        functools.partial(_filter_kernel, sl_len=sl),
        grid=(1,),
        in_specs=[
            _resident((sl, LANES)), _resident((LANES, hid)), _resident((1, hid)), _resident((1, hid)),
            _resident((hid, hid)), _resident((1, hid)), _resident((1, hid)),
            _resident((hid, 4 * HY_W)), _resident((1, 4 * HY_W)),
            _resident((sl, sl)), _resident((sl, sl)),
        ],
        out_specs=[pl.BlockSpec((2, sl, HY_W), lambda i: (0, 0, 0))] * 2,
        out_shape=[jax.ShapeDtypeStruct((2, sl, HY_W), F32)] * 2,
        compiler_params=_params(1),
        name="hyena_filter",
    )(feat, w1, b1, f1, w2, b2, f2, w3, dec, cos, sinp)


def _hyena_kernel(x_ref, cw, cb, bias, kre, kim, cos, sin, sin_t, o_ref, hc, u, pre_s, pim_s, *, sl_len, n_seq):
    tile = min(DFT_TILE, sl_len)
    hc[...] = _dwconv(x_ref[...], cw[...], cb[...], sl_len)
    u[...] = hc[:, 0:HY_W].astype(BF16)

    gw = HY_W // HY_GROUPS
    chains = [(s, c) for s in range(n_seq) for c in range(HY_GROUPS)]
    held = {}

    def phase(chain, k):
        s, c = chains[chain]
        order, kind = divmod(k, 4)
        cols = slice(c * gw, (c + 1) * gw)
        x1_cols = slice(HY_W + c * gw, HY_W + (c + 1) * gw)
        x2_cols = slice(2 * HY_W + c * gw, 2 * HY_W + (c + 1) * gw)
        seq_rows = slice(s * sl_len, (s + 1) * sl_len)

        def spectrum(t, trows, rows):
            held[chain, t] = (jnp.dot(cos[trows, :], u[seq_rows, cols], preferred_element_type=F32),
                              jnp.dot(sin[trows, :], u[seq_rows, cols], preferred_element_type=F32))

        def multiply(t, trows, rows):
            a, b = held.pop((chain, t))
            kr = kre[order, trows, cols]
            ki = kim[order, trows, cols]
            if t == 0:
                first = lax.broadcasted_iota(jnp.int32, (tile, 1), 0) == 0
                pre = jnp.where(first, a * kr, a * kr + b * ki)
                pim = jnp.where(first, b * ki, b * kr - a * ki)
            else:
                pre = a * kr + b * ki
                pim = b * kr - a * ki
            pre_s[rows, cols] = pre.astype(BF16)
            pim_s[rows, cols] = pim.astype(BF16)

        def back(t, trows, rows):
            held[chain, t] = (jnp.dot(cos[trows, :], pre_s[seq_rows, cols], preferred_element_type=F32)
                              + jnp.dot(sin_t[trows, :], pim_s[seq_rows, cols], preferred_element_type=F32))

        def gate(t, trows, rows):
            y = held.pop((chain, t)) + hc[rows, cols] * bias[order:order + 1, cols]
            if order == 0:
                z = hc[rows, x1_cols] * y
                hc[rows, cols] = z
                u[rows, cols] = z.astype(BF16)
            else:
                o_ref[rows, cols] = (hc[rows, x2_cols] * y).astype(BF16)

        body = (spectrum, multiply, back, gate)[kind]
        return [functools.partial(body, t, slice(t, t + tile), slice(s * sl_len + t, s * sl_len + t + tile))
                for t in range(0, sl_len, tile)]

    n_phase = 8
    for slot in range(n_phase + 1):
        ahead = [op for ch in range(0, len(chains), 2) if slot < n_phase for op in phase(ch, slot)]
        behind = [op for ch in range(1, len(chains), 2) if slot >= 1 for op in phase(ch, slot - 1)]
        for i in range(max(len(ahead), len(behind))):
            for ops in (ahead, behind):
                if i < len(ops):
                    ops[i]()


def _hyena_call(hy_in, cw, cb, bias, kre, kim, mats, nb, sl, n_seq):
    assert nb % n_seq == 0
    seq = lambda b: (b, 0)
    mat = _resident((sl, sl))
    rows = n_seq * sl
    return pl.pallas_call(
        functools.partial(_hyena_kernel, sl_len=sl, n_seq=n_seq),
        grid=(nb // n_seq,),
        in_specs=[
            pl.BlockSpec((rows, 3 * HY_W), seq),
            _resident((3, 3 * HY_W)), _resident((1, 3 * HY_W)), _resident((2, HY_W)),
            _resident((2, sl, HY_W)), _resident((2, sl, HY_W)),
            mat, mat, mat,
        ],
        out_specs=pl.BlockSpec((rows, HY_W), seq),
        out_shape=jax.ShapeDtypeStruct((nb * sl, HY_W), BF16),
        scratch_shapes=[pltpu.VMEM((rows, 3 * HY_W), F32)] + [pltpu.VMEM((rows, HY_W), BF16)] * 3,
        compiler_params=_params(1),
        name="hyena_conv",
    )(hy_in, cw, cb, bias, kre, kim, *mats)


SSM_PAIRS = SSM_HEADS // 2
PAIR_W = 2 * SSM_HEAD_DIM


def _ssd_kernel(xbc_ref, dt_ref, h0_ref, cw, cb, dtb, alog, dsk, expand, o_ref, hfin_ref, xbc, dt, dta, hst, y_bwd, *,
                sl_len):
    nc = sl_len // SSM_CHUNK
    q = SSM_CHUNK
    xbc[...] = _silu(_dwconv(xbc_ref[...], cw[...], cb[...], sl_len))
    dtv = jax.nn.softplus(dt_ref[...] + dtb[...])
    dt[...] = dtv
    dta[...] = dtv * (-jnp.exp(alog[...]))
    hst[...] = h0_ref[0]
    ii = lax.broadcasted_iota(jnp.int32, (q, q), 0)
    jj = lax.broadcasted_iota(jnp.int32, (q, q), 1)
    low_lanes = lax.broadcasted_iota(jnp.int32, (1, PAIR_W), 1) < SSM_HEAD_DIM
    low_rows = lax.broadcasted_iota(jnp.int32, (2 * SSM_STATE, 1), 0) < SSM_STATE

    def prepare(c, d):
        rows = pl.ds(pl.multiple_of(c * q, q), q)
        xs = xbc[rows, 0:SSM_W]
        bm = xbc[rows, SSM_W:SSM_W + SSM_GROUPS * SSM_STATE]
        cm = xbc[rows, SSM_W + SSM_GROUPS * SSM_STATE:XBC_W]
        tri = (jj <= ii) if d == 0 else (jj >= ii)
        trib = tri.astype(BF16)
        a1, a2, a3 = _split3(dta[rows, :])
        acum = (jnp.dot(trib, a1, preferred_element_type=F32) + jnp.dot(trib, a2, preferred_element_type=F32)
                + jnp.dot(trib, a3, preferred_element_type=F32))
        dtc = dt[rows, :]
        last = acum[q - 1:q, :] if d == 0 else acum[0:1, :]
        w1, w2 = _split2(dtc * jnp.exp(last - acum))
        w_wide = (jnp.dot(w1, expand[d], preferred_element_type=F32)
                  + jnp.dot(w2, expand[d], preferred_element_type=F32))
        bm_t = bm.T.astype(BF16)
        cb_t = [_bdot(jnp.where(low_lanes, cm, 0.0), bm_t), _bdot(jnp.where(low_lanes, 0.0, cm), bm_t)]
        return dict(d=d, rows=rows, xs=xs, cm=cm, tri=tri, acum=acum, acum_t=acum.T, dt_t=dtc.T,
                    cdec=jnp.exp(last), xdd=xs * w_wide, bm_t=bm_t, cb_t=cb_t)

    def step(c, carry):
        chunks = [prepare(c, 0), prepare(nc - 1 - c, 1)]
        units = [(ch, p) for ch in chunks for p in range(SSM_PAIRS)]
        scores, grow = [], []
        for ch, p in units:
            g = p // (SSM_PAIRS // SSM_GROUPS)
            sc, gr = [], []
            for hh in range(2):
                col = ch["d"] * SSM_HEADS + 2 * p + hh
                a_col = jnp.broadcast_to(ch["acum"][:, col:col + 1], (q, q))
                lmat = jnp.exp(jnp.where(ch["tri"], a_col - ch["acum_t"][col:col + 1, :], -jnp.inf))
                sc.append((ch["cb_t"][g] * lmat * ch["dt_t"][col:col + 1, :]).astype(BF16))
                gr.append(jnp.exp(a_col))
            scores.append(jnp.concatenate(sc, axis=1))
            grow.append(jnp.where(low_lanes, gr[0], gr[1]))
        y_diag, y_off, states = [], [], []
        for (ch, p), sc in zip(units, scores):
            xp = ch["xs"][:, p * PAIR_W:(p + 1) * PAIR_W]
            x_blocks = jnp.concatenate([jnp.where(low_lanes, xp, 0.0), jnp.where(low_lanes, 0.0, xp)], axis=0)
            y_diag.append(jnp.dot(sc, x_blocks.astype(BF16), preferred_element_type=F32))
        for ch, p in units:
            y_off.append(_bdot(ch["cm"], hst[ch["d"] * SSM_PAIRS + p]))
            states.append(_bdot(ch["bm_t"], ch["xdd"][:, p * PAIR_W:(p + 1) * PAIR_W]))
        for i, (ch, p) in enumerate(units):
            slab = ch["d"] * SSM_PAIRS + p
            col = ch["d"] * SSM_HEADS + 2 * p
            cd = jnp.where(low_lanes, ch["cdec"][:, col:col + 1], ch["cdec"][:, col + 1:col + 2])
            own_rows = low_rows if p < SSM_PAIRS // SSM_GROUPS else jnp.logical_not(low_rows)
            hst[slab] = hst[slab] * cd + jnp.where(own_rows, states[i], 0.0)
        for k, ch in enumerate(chunks):
            y = jnp.concatenate([y_diag[k * SSM_PAIRS + p] + y_off[k * SSM_PAIRS + p] * grow[k * SSM_PAIRS + p]
                                 for p in range(SSM_PAIRS)], axis=1)
            if ch["d"] == 0:
                o_ref[ch["rows"], :] = y + dsk[...] * ch["xs"]
            else:
                y_bwd[ch["rows"], :] = y
        return carry

    lax.fori_loop(0, nc, step, 0, unroll=4 if nc % 4 == 0 else 2)
    o_ref[...] += y_bwd[...]
    for slab in range(2 * SSM_PAIRS):
        g = (slab % SSM_PAIRS) // (SSM_PAIRS // SSM_GROUPS)
        st_t = hst[slab].T
        for hh in range(2):
            hfin_ref[0, 2 * slab + hh] = st_t[hh * SSM_HEAD_DIM:(hh + 1) * SSM_HEAD_DIM,
                                              g * SSM_STATE:(g + 1) * SSM_STATE]


def _ssd_call(s_xbc, s_dt, h0, cw, cb, dtb, alog, dsk, expand, nb, sl, shared_h0):
    seq = lambda b: (b, 0)
    st_shape = (1, 2 * SSM_PAIRS, 2 * SSM_STATE, PAIR_W)
    fin_shape = (1, 2 * SSM_HEADS, SSM_HEAD_DIM, SSM_STATE)
    h0_map =(lambda b: (0, 0, 0, 0)) if shared_h0 else (lambda b: (b, 0, 0, 0))
    return pl.pallas_call(
        functools.partial(_ssd_kernel, sl_len=sl),
        grid=(nb,),
        in_specs=[
            pl.BlockSpec((sl, XBC_W), seq), pl.BlockSpec((sl, LANES), seq), pl.BlockSpec(st_shape, h0_map),
            _resident((3, XBC_W)), _resident((1, XBC_W)), _resident((1, LANES)), _resident((1, LANES)),
            _resident((1, SSM_W)), _resident((2, LANES, SSM_W)),
        ],
        out_specs=[pl.BlockSpec((sl, SSM_W), seq), pl.BlockSpec(fin_shape, lambda b: (b, 0, 0, 0))],
        out_shape=[jax.ShapeDtypeStruct((nb * sl, SSM_W), F32), jax.ShapeDtypeStruct((nb, *fin_shape[1:]), F32)],
        scratch_shapes=[pltpu.VMEM((sl, XBC_W), F32), pltpu.VMEM((sl, LANES), F32), pltpu.VMEM((sl, LANES), F32),
                        pltpu.VMEM(st_shape[1:], F32), pltpu.VMEM((sl, SSM_W), F32)],
        compiler_params=_params(1),
        name="ssd_scan",
    )(s_xbc, s_dt, h0, cw, cb, dtb, alog, dsk, expand)


def _pack_states(h):
    nb = h.shape[0]
    t = h.reshape(nb, 2, SSM_PAIRS, 2, SSM_HEAD_DIM, SSM_STATE).transpose(0, 1, 2, 5, 3, 4)
    t = t.reshape(nb, 2, SSM_PAIRS, SSM_STATE, PAIR_W)
    z = jnp.zeros_like(t)
    first = (jnp.arange(SSM_PAIRS) < SSM_PAIRS // SSM_GROUPS)[None, None, :, None, None]
    slabs = jnp.where(first, jnp.concatenate([t, z], axis=3), jnp.concatenate([z, t], axis=3))
    return slabs.reshape(nb, 2 * SSM_PAIRS, 2 * SSM_STATE, PAIR_W)


def _expand_table():
    r = jnp.arange(LANES)[:, None]
    head = jnp.arange(SSM_W)[None, :] // SSM_HEAD_DIM
    return jnp.stack([r == head, r == SSM_HEADS + head]).astype(BF16)


def _attend_groups(units):
    logits = []
    for keys, q_t, _, _, g, _ in units:
        z = jnp.zeros_like(q_t)
        rhs = jnp.concatenate([q_t, z] if g == 0 else [z, q_t], axis=0)
        logits.append(jnp.dot(keys, rhs, preferred_element_type=F32))
    logits = [s if u[5] is None else jnp.where(u[5], s, -jnp.inf) for s, u in zip(logits, units)]
    sinks = [u[3] for u in units]
    tops = [jnp.maximum(jnp.max(s, axis=0, keepdims=True), sk) for s, sk in zip(logits, sinks)]
    exps = [jnp.exp(s - m) for s, m in zip(logits, tops)]
    denoms = [jnp.sum(e, axis=0, keepdims=True) + jnp.exp(sk - m) for e, m, sk in zip(exps, tops, sinks)]
    return [jnp.dot(u[2], e.astype(BF16), preferred_element_type=F32) / d for u, e, d in zip(units, exps, denoms)]


def _ctx_attn_kernel(q_ref, k_ref, v_ref, sink_ref, o_ref, *, sl_len, n_seq):
    scale = HEAD_DIM ** -0.5
    units = []
    for s in range(n_seq):
        rows = slice(s * sl_len, (s + 1) * sl_len)
        q_t = (q_ref[rows, :] * scale).T.astype(BF16)
        keys = k_ref[rows, :].astype(BF16)
        v_t = v_ref[rows, :].T.astype(BF16)
        for g in range(KV_HEADS):
            q_g = jnp.concatenate([q_t[(g * Q_PER_KV + hq) * HEAD_DIM:(g * Q_PER_KV + hq + 1) * HEAD_DIM, :]
                                   for hq in range(Q_PER_KV)], axis=1)
            units.append((keys, q_g, v_t[g * HEAD_DIM:(g + 1) * HEAD_DIM, :], sink_ref[g], g, None))
    o_ts = _attend_groups(units)
    for s in range(n_seq):
        outs = [o_t[:, hq * sl_len:(hq + 1) * sl_len] for o_t in o_ts[s * KV_HEADS:(s + 1) * KV_HEADS]
                for hq in range(Q_PER_KV)]
        o_ref[s * sl_len:(s + 1) * sl_len, :] = jnp.concatenate(outs, axis=0).T.astype(BF16)


def _ctx_attn_call(q, k, v, sink_rows, nb, sl, n_seq):
    assert nb % n_seq == 0
    seq = lambda b: (b, 0)
    rows = n_seq * sl
    return pl.pallas_call(
        functools.partial(_ctx_attn_kernel, sl_len=sl, n_seq=n_seq),
        grid=(nb // n_seq,),
        in_specs=[pl.BlockSpec((rows, ATT_W), seq), pl.BlockSpec((rows, KV_W), seq), pl.BlockSpec((rows, KV_W), seq),
                  _resident((KV_HEADS, 1, Q_PER_KV * sl))],
        out_specs=pl.BlockSpec((rows, ATT_W), seq),
        out_shape=jax.ShapeDtypeStruct((nb * sl, ATT_W), BF16),
        compiler_params=_params(1),
        name="ctx_attention",
    )(q, k, v, sink_rows)


def _rope(x, cos, sin_signed):
    w = x.shape[1]
    lane = lax.broadcasted_iota(jnp.int32, (1, w), 1)
    partner = jnp.where(lane % 32 < 16, pltpu.roll(x, w - 16, 1), pltpu.roll(x, 16, 1))
    return x * cos + partner * sin_signed


def _lat_attn_kernel(q_ref, k_ref, v_ref, ck_ref, cv_ref, cos_ref, sin_ref, sink_ref, o_ref, q_t, kcat, v_t, o_t, *,
                     sl_len, past):
    blk = ATT_BLOCK
    nblk = sl_len // blk
    scale = HEAD_DIM ** -0.5
    q_t[...] = (_rope(q_ref[...], cos_ref[...], sin_ref[...]) * scale).T.astype(BF16)
    kcat[0:past, :] = ck_ref[0].astype(BF16)
    kcat[past:, :] = _rope(k_ref[...], cos_ref[:, 0:KV_W], sin_ref[:, 0:KV_W]).astype(BF16)
    v_t[:, 0:past] = cv_ref[0].T.astype(BF16)
    v_t[:, past:] = v_ref[...].T.astype(BF16)
    for i0 in range(0, nblk, LAT_BLOCKS_PER_ROUND):
        units, places = [], []
        for i in range(i0, min(i0 + LAT_BLOCKS_PER_ROUND, nblk)):
            cols = slice(i * blk, (i + 1) * blk)
            lo, hi = max(i - 1, 0), min(i + 2, nblk)
            band = slice(past + lo * blk, past + hi * blk)
            n_keys = past + (hi - lo) * blk
            kk = lax.broadcasted_iota(jnp.int32, (n_keys, 1), 0)
            qi = lax.broadcasted_iota(jnp.int32, (1, Q_PER_KV * blk), 1) % blk
            rel = kk - past + (lo - i) * blk
            mask = (kk < past) | (jnp.abs(qi - rel) <= WINDOW)
            keys = jnp.concatenate([kcat[0:past, :], kcat[band, :]], axis=0)
            for g in range(KV_HEADS):
                drows = slice(g * HEAD_DIM, (g + 1) * HEAD_DIM)
                q_g = jnp.concatenate([q_t[(g * Q_PER_KV + hq) * HEAD_DIM:(g * Q_PER_KV + hq + 1) * HEAD_DIM, cols]
                                       for hq in range(Q_PER_KV)], axis=1)
                vals = jnp.concatenate([v_t[drows, 0:past], v_t[drows, band]], axis=1)
                units.append((keys, q_g, vals, sink_ref[g], g, mask))
                places.append((g, cols))
        for (g, cols), out in zip(places, _attend_groups(units)):
            for hq in range(Q_PER_KV):
                h = g * Q_PER_KV + hq
                o_t[h * HEAD_DIM:(h + 1) * HEAD_DIM, cols] = out[:, hq * blk:(hq + 1) * blk]
    o_ref[...] = o_t[...].T.astype(BF16)


def _lat_attn_call(q, k, v, ck, cv, cos, sin, sink_rows, nb, sl):
    seq = lambda b: (b, 0)
    past = ck.shape[1]
    cache = pl.BlockSpec((1, past, KV_W), lambda b: (b, 0, 0))
    return pl.pallas_call(
        functools.partial(_lat_attn_kernel, sl_len=sl, past=past),
        grid=(nb,),
        in_specs=[pl.BlockSpec((sl, ATT_W), seq), pl.BlockSpec((sl, KV_W), seq), pl.BlockSpec((sl, KV_W), seq),
                  cache, cache, _resident((sl, ATT_W)), _resident((sl, ATT_W)),
                  _resident((KV_HEADS, 1, Q_PER_KV * ATT_BLOCK))],
        out_specs=pl.BlockSpec((sl, ATT_W), seq),
        out_shape=jax.ShapeDtypeStruct((nb * sl, ATT_W), BF16),
        scratch_shapes=[pltpu.VMEM((ATT_W, sl), BF16), pltpu.VMEM((past + sl, KV_W), BF16),
                        pltpu.VMEM((KV_W, past + sl), BF16), pltpu.VMEM((ATT_W, sl), F32)],
        compiler_params=_params(1),
        name="latent_attention",
    )(q, k, v, ck, cv, cos, sin, sink_rows)


def _sink_rows(sink, n_q):
    return jnp.repeat(sink.reshape(KV_HEADS, Q_PER_KV), n_q, axis=1).reshape(KV_HEADS, 1, Q_PER_KV * n_q)


def _post_kernel(x_ref, hy_ref, sy_ref, sz_ref, at_ref, g_ref, mod_ref, snorm, npost, nfpre, nfpost,
                 wh, ws, wa, wo, wg, wu, wd, o_ref):
    m = mod_ref[0]
    halves = [slice(r * (ROW_TILE // 2), (r + 1) * (ROW_TILE // 2)) for r in range(2)]
    ssm_y = [_rms(sy_ref[r, :] * _silu(sz_ref[r, :]), snorm[...]) for r in halves]
    p_hy = [_bdot(hy_ref[r, :], wh[...]) for r in halves]
    p_ss = [_bdot(y, ws[...]) for y in ssm_y]
    p_at = [_bdot(at_ref[r, :], wa[...]) for r in halves]
    merged = [jax.nn.sigmoid(g_ref[r, 0:D_MODEL]) * a + jax.nn.sigmoid(g_ref[r, D_MODEL:2 * D_MODEL]) * b
              + jax.nn.sigmoid(g_ref[r, 2 * D_MODEL:3 * D_MODEL]) * c for r, a, b, c in zip(halves, p_hy, p_ss, p_at)]
    mix = [_bdot(v, wo[...]) for v in merged]
    x1 = [x_ref[r, :] + m[2:3, :] * _rms(v, npost[...]) for r, v in zip(halves, mix)]
    hb = [(_rms(v, nfpre[...]) * (1.0 + m[4:5, :]) + m[3:4, :]).astype(BF16) for v in x1]
    f = [None, None]
    for lo, hi in FFN_CHUNKS:
        gate = [jnp.dot(v, wg[:, lo:hi], preferred_element_type=F32) for v in hb]
        up = [jnp.dot(v, wu[:, lo:hi], preferred_element_type=F32) for v in hb]
        part = [_bdot(_silu(a) * b, wd[lo:hi, :]) for a, b in zip(gate, up)]
        f = [p if acc is None else acc + p for acc, p in zip(f, part)]
    for r, a, b in zip(halves, x1, f):
        o_ref[r, :] = a + m[5:6, :] * _rms(b, nfpost[...])


def _post_call(x, hy, sy, sz, at, g, mod, snorm, npost, nfpre, nfpost, wh, ws, wa, wo, wg, wu, wd, layer, nb, sl,
               shared_mod):
    n_tok = nb * sl
    assert n_tok % ROW_TILE == 0
    tok = lambda wdt: pl.BlockSpec((ROW_TILE, wdt), lambda i: (i, 0))
    vec = lambda wdt: _resident((1, wdt))
    return pl.pallas_call(
        _post_kernel,
        grid=(n_tok // ROW_TILE,),
        in_specs=[tok(D_MODEL), tok(HY_W), tok(SSM_W), tok(SSM_W), tok(ATT_W), tok(3 * D_MODEL),
                  _mod_spec(shared_mod, sl),
                  vec(SSM_W), vec(D_MODEL), vec(D_MODEL), vec(D_MODEL),
                  _resident((HY_W, D_MODEL), layer), _resident((SSM_W, D_MODEL), layer),
                  _resident((ATT_W, D_MODEL), layer), _resident((D_MODEL, D_MODEL), layer),
                  _resident((D_MODEL, FFN_HIDDEN), layer), _resident((D_MODEL, FFN_HIDDEN), layer),
                  _resident((FFN_HIDDEN, D_MODEL), layer)],
        out_specs=tok(D_MODEL),
        out_shape=jax.ShapeDtypeStruct((n_tok, D_MODEL), F32),
        compiler_params=_params(1),
        name="merge_ffn",
    )(x, hy, sy, sz, at, g, mod, snorm, npost, nfpre, nfpost, wh, ws, wa, wo, wg, wu, wd)


def _dft_mats(sl):
    radix = 32
    assert sl % radix == 0
    s = jnp.arange(sl, dtype=jnp.int32)

    def angles(f):
        return ((f[:, None] * s[None, :]) % (2 * sl)).astype(F32) * (math.pi / sl)

    a_hi = angles(jnp.arange(sl // radix, dtype=jnp.int32) * radix)[:, None, :]
    a_lo = angles(jnp.arange(radix, dtype=jnp.int32))[None, :, :]
    cos = (jnp.cos(a_hi) * jnp.cos(a_lo) - jnp.sin(a_hi) * jnp.sin(a_lo)).reshape(sl, sl)
    sin = (jnp.sin(a_hi) * jnp.cos(a_lo) + jnp.cos(a_hi) * jnp.sin(a_lo)).reshape(sl, sl)
    sign = jnp.where(s % 2 == 1, -1.0, 1.0).astype(F32)
    sinp = jnp.where(s[:, None] == 0, sign[None, :], sin).astype(BF16)
    return cos.astype(BF16), sinp, sinp.T


def _filter_features(sl):
    t = jnp.arange(sl, dtype=F32) / sl
    ang = 2.0 * jnp.pi * t[:, None] * jnp.arange(1, HY_BANDS + 1, dtype=F32)[None]
    feat = jnp.concatenate([t[:, None], jnp.cos(ang), jnp.sin(ang)], axis=-1)
    return jnp.pad(feat, ((0, 0), (0, LANES - HY_FEAT)))


def _rope_tables(sl):
    rows = sl // GRID_W
    row = jnp.repeat(jnp.arange(rows), GRID_W).astype(F32)
    col = jnp.tile(jnp.arange(GRID_W), rows).astype(F32)
    nq = HEAD_DIM // 4
    inv = ROPE_BASE ** (-jnp.arange(nq, dtype=F32) / nq)
    ar = row[:, None] * inv
    ac = col[:, None] * inv
    cos = jnp.concatenate([jnp.cos(ar), jnp.cos(ar), jnp.cos(ac), jnp.cos(ac)], axis=-1)
    sin = jnp.concatenate([-jnp.sin(ar), jnp.sin(ar), -jnp.sin(ac), jnp.sin(ac)], axis=-1)
    return jnp.tile(cos, (1, N_HEADS)), jnp.tile(sin, (1, N_HEADS))


def _pad_lanes(v):
    return jnp.pad(v.reshape(1, -1), ((0, 0), (0, LANES - v.size)))


def kernel(x_prompt, x_sample, c, cache_k, cache_v, state_ssm, c_ctx, ada_w, ada_b,
           norm_mix_pre, norm_mix_post, norm_ffn_pre, norm_ffn_post, w_in,
           hy_conv_w, hy_conv_b, hy_w1, hy_b1, hy_freq1, hy_w2, hy_b2, hy_freq2, hy_w3,
           hy_decay, hy_bias, ssm_conv_w, ssm_conv_b, ssm_a_log, ssm_dt_bias, ssm_d, ssm_norm,
           attn_sink, hy_proj, ssm_proj, attn_proj, w_out, ffn_w_gate, ffn_w_up, ffn_w_down):
    nb_c, sl_c, _ = x_prompt.shape
    nb_l, sl_l, _ = x_sample.shape
    past = cache_k.shape[2]

    cond = jnp.concatenate([c_ctx[None], c, jnp.zeros((8 - 1 - nb_l, D_MODEL), F32)], axis=0)
    mod = _ada_call(cond, ada_w, ada_b).reshape(DEPTH, 8, 6, D_MODEL)

    streams = {}
    for name, x in (("ctx", x_prompt), ("lat", x_sample)):
        nb, sl, _ = x.shape
        streams[name] = dict(nb=nb, sl=sl, x=x.reshape(nb * sl, D_MODEL), mats=_dft_mats(sl),
                             feat=_filter_features(sl))
    rope_cos, rope_sin = _rope_tables(sl_l)
    ctx_per_step = max(1, min(nb_c, SEQ_ROWS_PER_STEP // sl_c))
    while nb_c % ctx_per_step:
        ctx_per_step -= 1
    zero_state = jnp.zeros((1, 2 * SSM_PAIRS, 2 * SSM_STATE, PAIR_W), F32)
    expand = _expand_table()
    row = lambda v: v.reshape(1, -1)

    w_in_t = jnp.swapaxes(w_in, 1, 2).astype(BF16)
    wh, ws, wa, wo = (w.astype(BF16) for w in (hy_proj, ssm_proj, attn_proj, w_out))
    wg, wu, wd = (w.astype(BF16) for w in (ffn_w_gate, ffn_w_up, ffn_w_down))

    ks, vs, ss = [], [], []
    for l in range(DEPTH):
        w1p = jnp.pad(hy_w1[l], ((0, LANES - HY_FEAT), (0, 0)))
        dtb = _pad_lanes(ssm_dt_bias[l])
        alog = _pad_lanes(ssm_a_log[l])
        dsk = jnp.repeat(ssm_d[l, 0] + ssm_d[l, 1], SSM_HEAD_DIM).reshape(1, SSM_W)
        sink = {"ctx": _sink_rows(attn_sink[l], sl_c), "lat": _sink_rows(attn_sink[l], ATT_BLOCK)}
        for name in ("ctx", "lat"):
            st = streams[name]
            nb, sl, is_ctx = st["nb"], st["sl"], name == "ctx"
            m = mod[l, 0:1] if is_ctx else mod[l, 1:1 + nb]
            kre, kim = _filter_call(st["feat"], w1p, row(hy_b1[l]), row(hy_freq1[l]), hy_w2[l], row(hy_b2[l]),
                                    row(hy_freq2[l]), hy_w3[l], row(hy_decay[l]), st["mats"][0], st["mats"][1], sl)
            hy_in, s_z, s_xbc, s_dt, a_q, a_k, a_v, gate = _in_call(
                st["x"], m, row(norm_mix_pre[l]), w_in_t, l, nb, sl, is_ctx)
            hy_y = _hyena_call(hy_in, hy_conv_w[l], row(hy_conv_b[l]), hy_bias[l], kre, kim, st["mats"], nb, sl,
                               ctx_per_step if is_ctx else 1)
            h0 = zero_state if is_ctx else _pack_states(state_ssm[:, l])
            ssm_y, h_fin = _ssd_call(s_xbc, s_dt, h0, ssm_conv_w[l], row(ssm_conv_b[l]), dtb, alog, dsk, expand,
                                     nb, sl, is_ctx)
            if is_ctx:
                att = _ctx_attn_call(a_q, a_k, a_v, sink[name], nb, sl, ctx_per_step)
                ks.append(a_k.reshape(nb, sl, KV_HEADS, HEAD_DIM))
                vs.append(a_v.reshape(nb, sl, KV_HEADS, HEAD_DIM))
                ss.append(h_fin.reshape(nb, 2, SSM_HEADS, SSM_HEAD_DIM, SSM_STATE))
            else:
                att = _lat_attn_call(a_q, a_k, a_v, cache_k[:, l].reshape(nb, past, KV_W),
                                     cache_v[:, l].reshape(nb, past, KV_W), rope_cos, rope_sin, sink[name], nb, sl)
            st["x"] = _post_call(st["x"], hy_y, ssm_y, s_z, att, gate, m, row(ssm_norm[l]), row(norm_mix_post[l]),
                                 row(norm_ffn_pre[l]), row(norm_ffn_post[l]), wh, ws, wa, wo, wg, wu, wd, l, nb, sl,
                                 is_ctx)

    y_prompt = streams["ctx"]["x"].reshape(nb_c, sl_c, D_MODEL)
    y_sample = streams["lat"]["x"].reshape(nb_l, sl_l, D_MODEL)
    return (y_prompt, y_sample, jnp.stack(ks, axis=1), jnp.stack(vs, axis=1), jnp.stack(ss, axis=1))
```
